```python
import math
import jax, jax.numpy as jnp
from jax import lax
import numpy as np

D_MODEL = 1024
BATCH = 4
SEQ = 8192
DEPTH = 2

D_CONV = D_MODEL // 2
D_GLA_V = D_MODEL - D_CONV
GLA_HEADS = 4
D_GLA_K = D_GLA_V // 2
HEAD_K = D_GLA_K // GLA_HEADS
HEAD_V = D_GLA_V // GLA_HEADS
GATE_RANK = 16
GATE_NORMALIZER = 16.0
CHUNK = 64
CONV_WIDTH = 31
D_MIX = D_CONV + D_GLA_V
D_IN = 2 * D_CONV + 2 * D_GLA_K + 2 * D_GLA_V + GATE_RANK
N_GROUPS = 4
EXPERTS_PER_GROUP = 8
N_EXPERTS = N_GROUPS * EXPERTS_PER_GROUP
TOP_K = 2
D_EXPERT = D_MODEL // 2
MOE_BLOCK = 256
EPS = 1e-6

kernel_name = "hymba_conformer_gla_hmoe"


def rmsnorm(x, g):
    xf = x.astype(jnp.float32)
    y = xf * lax.rsqrt(jnp.mean(xf * xf, axis=-1, keepdims=True) + EPS)
    return (y * g.astype(jnp.float32)).astype(x.dtype)


def conformer_conv(u_a, u_gate, conv_w, conv_b, ln_g, ln_b):
    u = u_a * jax.nn.sigmoid(u_gate)
    y = lax.conv_general_dilated(
        u, conv_w[:, None, :].astype(u.dtype), window_strides=(1,),
        padding=((CONV_WIDTH - 1, 0),), dimension_numbers=('NWC', 'WIO', 'NWC'),
        feature_group_count=D_CONV) + conv_b
    yf = y.astype(jnp.float32)
    mu = jnp.mean(yf, axis=-1, keepdims=True)
    var = jnp.mean(jnp.square(yf - mu), axis=-1, keepdims=True)
    yf = (yf - mu) * lax.rsqrt(var + EPS) * ln_g.astype(jnp.float32) + ln_b.astype(jnp.float32)
    return jax.nn.silu(yf).astype(u.dtype)


def gla(q, k, v, g_out, gk_low, gk_w, gk_b, head_norm_g):
    B, S = q.shape[0], q.shape[1]
    n = S // CHUNK
    gk = jax.nn.log_sigmoid((gk_low @ gk_w + gk_b).astype(jnp.float32)) / GATE_NORMALIZER

    def heads(t, d):
        return t.reshape(B, n, CHUNK, GLA_HEADS, d).transpose(1, 0, 3, 2, 4).astype(jnp.float32)

    qh = heads(q, HEAD_K) * (HEAD_K ** -0.5)
    kh = heads(k, HEAD_K)
    vh = heads(v, HEAD_V)
    b = jnp.cumsum(heads(gk, HEAD_K), axis=3)
    b_last = b[..., -1:, :]
    q_t = qh * jnp.exp(b)
    k_t = kh * jnp.exp(-b)
    k_s = kh * jnp.exp(b_last - b)
    causal = jnp.tril(jnp.ones((CHUNK, CHUNK), dtype=bool))
    scores = jnp.where(causal, jnp.einsum('nbhik,nbhjk->nbhij', q_t, k_t), 0.0)
    o_intra = jnp.einsum('nbhij,nbhjv->nbhiv', scores, vh)

    def step(state, xs):
        q_c, k_c, v_c, dec = xs
        o = jnp.einsum('bhik,bhkv->bhiv', q_c, state)
        state = state * dec[:, :, 0, :, None] + jnp.einsum('bhjk,bhjv->bhkv', k_c, v_c)
        return state, o

    s0 = jnp.zeros((B, GLA_HEADS, HEAD_K, HEAD_V), jnp.float32)
    _, o_inter = lax.scan(step, s0, (q_t, k_s, vh, jnp.exp(b_last)))
    o = o_intra + o_inter
    o = o * lax.rsqrt(jnp.mean(o * o, axis=-1, keepdims=True) + EPS) * head_norm_g.astype(jnp.float32)
    o = o.transpose(1, 0, 3, 2, 4).reshape(B, S, D_GLA_V).astype(g_out.dtype)
    return o * jax.nn.silu(g_out)


def hier_moe(h, rg_w, rg_b, re_w, re_b, w_gate, w_up, w_down):
    B, S, D = h.shape
    T = B * S
    ht = h.reshape(T, D)
    grp_prob = jax.nn.softmax((ht @ rg_w).astype(jnp.float32) + rg_b.astype(jnp.float32), axis=-1)
    grp_p, grp_idx = lax.top_k(grp_prob, 1)
    exp_logits = (ht @ re_w).astype(jnp.float32).reshape(T, N_GROUPS, EXPERTS_PER_GROUP) + re_b.astype(jnp.float32)
    sel = exp_logits[jnp.arange(T), grp_idx[:, 0]]
    w_k, e_k = lax.top_k(jax.nn.softmax(sel, axis=-1), TOP_K)
    gate = grp_p * (w_k / jnp.sum(w_k, axis=-1, keepdims=True))
    eid = (grp_idx * EXPERTS_PER_GROUP + e_k).reshape(-1)

    A = T * TOP_K
    tok = jnp.repeat(jnp.arange(T, dtype=jnp.int32), TOP_K)
    order = jnp.argsort(eid)
    e_sorted = eid[order]
    counts = jnp.bincount(eid, length=N_EXPERTS)
    padded = (counts + MOE_BLOCK - 1) // MOE_BLOCK * MOE_BLOCK
    start = jnp.cumsum(counts) - counts
    pend = jnp.cumsum(padded)
    pstart = pend - padded
    dest_sorted = pstart[e_sorted] + (jnp.arange(A) - start[e_sorted])
    dest = jnp.zeros((A,), jnp.int32).at[order].set(dest_sorted.astype(jnp.int32))
    n_blocks = (A + MOE_BLOCK - 1) // MOE_BLOCK + N_EXPERTS
    cap = n_blocks * MOE_BLOCK
    row_tok = jnp.zeros((cap,), jnp.int32).at[dest].set(tok)
    block_expert = jnp.minimum(
        jnp.searchsorted(pend, jnp.arange(n_blocks) * MOE_BLOCK, side='right'), N_EXPERTS - 1)
    xb = ht[row_tok].reshape(n_blocks, MOE_BLOCK, D)

    def expert_block(args):
        xblk, e = args
        return (jax.nn.silu(xblk @ w_gate[e]) * (xblk @ w_up[e])) @ w_down[e]

    yb = lax.map(expert_block, (xb, block_expert)).reshape(cap, D)
    y = jnp.sum(yb[dest].reshape(T, TOP_K, D) * gate[..., None].astype(h.dtype), axis=1)
    return y.reshape(B, S, D)


def setup_inputs(seed: int = 0) -> dict:
    key = jax.random.key(seed)
    ks = jax.random.split(key, 24)
    f32 = jnp.float32
    nrm = lambda k, shape, s: jax.random.normal(k, shape, f32) * s
    L = DEPTH
    return {
        "x": nrm(ks[0], (BATCH, SEQ, D_MODEL), 1.0),
        "attn_norm_g": 1.0 + nrm(ks[1], (L, D_MODEL), 0.02),
        "w_in": nrm(ks[2], (L, D_MODEL, D_IN), D_MODEL ** -0.5),
        "conv_w": nrm(ks[3], (L, CONV_WIDTH, D_CONV), CONV_WIDTH ** -0.5),
        "conv_b": nrm(ks[4], (L, D_CONV), 0.01),
        "conv_ln_g": 1.0 + nrm(ks[5], (L, D_CONV), 0.02),
        "conv_ln_b": nrm(ks[6], (L, D_CONV), 0.01),
        "gk_w": nrm(ks[7], (L, GATE_RANK, D_GLA_K), GATE_RANK ** -0.5),
        "gk_b": nrm(ks[8], (L, D_GLA_K), 0.01),
        "gla_norm_g": 1.0 + nrm(ks[9], (L, HEAD_V), 0.02),
        "w_out": nrm(ks[10], (L, D_MIX, D_MODEL), D_MIX ** -0.5),
        "ffn_norm_g": 1.0 + nrm(ks[11], (L, D_MODEL), 0.02),
        "router_group_w": nrm(ks[12], (L, D_MODEL, N_GROUPS), D_MODEL ** -0.5),
        "router_group_b": nrm(ks[13], (L, N_GROUPS), 0.01),
        "router_expert_w": nrm(ks[14], (L, D_MODEL, N_EXPERTS), D_MODEL ** -0.5),
        "router_expert_b": nrm(ks[15], (L, N_GROUPS, EXPERTS_PER_GROUP), 0.01),
        "expert_w_gate": nrm(ks[16], (L, N_EXPERTS, D_MODEL, D_EXPERT), D_MODEL ** -0.5),
        "expert_w_up": nrm(ks[17], (L, N_EXPERTS, D_MODEL, D_EXPERT), D_MODEL ** -0.5),
        "expert_w_down": nrm(ks[18], (L, N_EXPERTS, D_EXPERT, D_MODEL), D_EXPERT ** -0.5),
        "final_norm_g": 1.0 + nrm(ks[19], (D_MODEL,), 0.02),
    }


def reference(x, attn_norm_g, w_in, conv_w, conv_b, conv_ln_g, conv_ln_b, gk_w, gk_b, gla_norm_g,
              w_out, ffn_norm_g, router_group_w, router_group_b, router_expert_w, router_expert_b,
              expert_w_gate, expert_w_up, expert_w_down, final_norm_g):
    splits = [D_CONV, 2 * D_CONV, 2 * D_CONV + D_GLA_K, 2 * D_CONV + 2 * D_GLA_K,
              2 * D_CONV + 2 * D_GLA_K + D_GLA_V, 2 * D_CONV + 2 * D_GLA_K + 2 * D_GLA_V]
    for l in range(DEPTH):
        hn = rmsnorm(x, attn_norm_g[l])
        proj = hn @ w_in[l]
        u_a, u_gate, q, k, v, g_out, gk_low = jnp.split(proj, splits, axis=-1)
        conv_out = conformer_conv(u_a, u_gate, conv_w[l], conv_b[l], conv_ln_g[l], conv_ln_b[l])
        gla_out = gla(q, k, v, g_out, gk_low, gk_w[l], gk_b[l], gla_norm_g[l])
        x = x + jnp.concatenate([conv_out, gla_out], axis=-1) @ w_out[l]
        hf = rmsnorm(x, ffn_norm_g[l])
        x = x + hier_moe(hf, router_group_w[l], router_group_b[l], router_expert_w[l], router_expert_b[l],
                         expert_w_gate[l], expert_w_up[l], expert_w_down[l])
    return rmsnorm(x, final_norm_g)
```

```python
import functools

import jax
import jax.numpy as jnp
from jax import lax
from jax.experimental import pallas as pl
from jax.experimental.pallas import tpu as pltpu

D_MODEL = 1024
D_CONV = 512
D_GLA_V = 512
GLA_HEADS = 4
D_GLA_K = 256
HEAD_K = 64
HEAD_V = 128
GATE_RANK = 16
GATE_NORMALIZER = 16.0
CHUNK = 64
CONV_WIDTH = 31
N_GROUPS = 4
EXPERTS_PER_GROUP = 8
N_EXPERTS = 32
TOP_K = 2
D_EXPERT = 512
MOE_BLOCK = 256
EPS = 1e-6

LANES = 128
D_IN_MAIN = 2 * D_CONV + 2 * D_GLA_K + 2 * D_GLA_V
D_IN_PAD = D_IN_MAIN + LANES
TM = 256
HALO = 32
CONV_ROWS = 32
TD = 256
EXPERT_LANE0 = N_GROUPS
VMEM_LIMIT = 48 * 1024 * 1024

F32 = jnp.float32
BF16 = jnp.bfloat16


def _dot(a, b):
    return jnp.dot(a, b, preferred_element_type=F32)


def _dot_nt(a, b):
    return lax.dot_general(a, b, (((1,), (1,)), ((), ())), preferred_element_type=F32)


def _dot_tn(a, b):
    return lax.dot_general(a, b, (((0,), (0,)), ((), ())), preferred_element_type=F32)


def _split_bf16(x):
    hi = x.astype(BF16)
    lo = (x - hi.astype(F32)).astype(BF16)
    return hi, lo


def _rms(x, g):
    return x * lax.rsqrt(jnp.mean(x * x, axis=-1, keepdims=True) + EPS) * g


def _mixer_kernel(x_ref, ang_ref, win_ref, cw_ref, cb_ref, lng_ref, lnb_ref, gkw_ref, gkb_ref, gng_ref,
                  wout_ref, fng_ref, rw_ref, rb_ref,
                  xo_ref, hf_ref, meta_ref, cnt_ref,
                  ubuf, ybuf, st_ref, carry_ref):
    bi = pl.program_id(0)
    si = pl.program_id(1)

    @pl.when(si == 0)
    def _():
        ubuf[0:HALO, :] = jnp.zeros((HALO, D_CONV), F32)
        st_ref[...] = jnp.zeros_like(st_ref)

    @pl.when((bi == 0) & (si == 0))
    def _():
        carry_ref[...] = jnp.zeros_like(carry_ref)

    x = x_ref[...]
    hn = _rms(x, ang_ref[...]).astype(BF16)
    proj = _dot(hn, win_ref[...])
    o0 = 0
    ua = proj[:, o0:o0 + D_CONV]; o0 += D_CONV
    ug = proj[:, o0:o0 + D_CONV]; o0 += D_CONV
    q = proj[:, o0:o0 + D_GLA_K]; o0 += D_GLA_K
    k = proj[:, o0:o0 + D_GLA_K]; o0 += D_GLA_K
    v = proj[:, o0:o0 + D_GLA_V]; o0 += D_GLA_V
    go = proj[:, o0:o0 + D_GLA_V]; o0 += D_GLA_V
    gkl = proj[:, o0:o0 + LANES]

    ubuf[HALO:HALO + TM, :] = ua * jax.nn.sigmoid(ug)
    for c in range(TM // CONV_ROWS):
        acc = jnp.broadcast_to(cb_ref[...], (CONV_ROWS, D_CONV))
        for j in range(CONV_WIDTH):
            off = HALO - (CONV_WIDTH - 1) + j + c * CONV_ROWS
            acc = acc + cw_ref[j:j + 1, :] * ubuf[off:off + CONV_ROWS, :]
        ybuf[c * CONV_ROWS:(c + 1) * CONV_ROWS, :] = acc
    ubuf[0:HALO, :] = ubuf[TM:TM + HALO, :]
    y = ybuf[...]
    mu = jnp.mean(y, axis=-1, keepdims=True)
    yc = y - mu
    var = jnp.mean(yc * yc, axis=-1, keepdims=True)
    yn = yc * lax.rsqrt(var + EPS) * lng_ref[...] + lnb_ref[...]
    conv_out = yn * jax.nn.sigmoid(yn)

    gk = jax.nn.log_sigmoid(_dot(gkl.astype(BF16), gkw_ref[...]) + gkb_ref[...]) / GATE_NORMALIZER
    row = lax.broadcasted_iota(jnp.int32, (TM, TM), 0)
    col = lax.broadcasted_iota(jnp.int32, (TM, TM), 1)
    same_chunk = (row // CHUNK) == (col // CHUNK)
    causal = same_chunk & (col <= row)
    gk_hi, gk_lo = _split_bf16(gk)
    l_cum = causal.astype(BF16)
    l_all = same_chunk.astype(BF16)
    b = _dot(l_cum, gk_hi) + _dot(l_cum, gk_lo)
    bl = _dot(l_all, gk_hi) + _dot(l_all, gk_lo)
    qt = (q * (HEAD_K ** -0.5) * jnp.exp(b)).astype(BF16)
    kt = (k * jnp.exp(-b)).astype(BF16)
    ks = (k * jnp.exp(bl - b)).astype(BF16)
    dec = jnp.exp(bl)
    vb = v.astype(BF16)

    klane = lax.broadcasted_iota(jnp.int32, (1, D_GLA_K), 1) // HEAD_K
    o_parts = []
    for h in range(GLA_HEADS):
        qh = jnp.where(klane == h, qt, jnp.zeros_like(qt))
        a = jnp.where(causal, _dot_nt(qh, kt), 0.0).astype(BF16)
        o_parts.append(_dot(a, vb[:, h * HEAD_V:(h + 1) * HEAD_V]))
    o_intra = jnp.concatenate(o_parts, axis=-1)

    srow = lax.broadcasted_iota(jnp.int32, (D_GLA_V, D_GLA_K), 0) // HEAD_V
    scol = lax.broadcasted_iota(jnp.int32, (D_GLA_V, D_GLA_K), 1) // HEAD_K
    head_diag = srow == scol
    o_inter = []
    for c in range(TM // CHUNK):
        r0 = c * CHUNK
        st = st_ref[...]
        o_inter.append(_dot_nt(qt[r0:r0 + CHUNK], st.astype(BF16)))
        ut = _dot_tn(vb[r0:r0 + CHUNK], ks[r0:r0 + CHUNK])
        st_ref[...] = st * dec[r0:r0 + 1, :] + jnp.where(head_diag, ut, 0.0)
    o = o_intra + jnp.concatenate(o_inter, axis=0)
    o_n = []
    for h in range(GLA_HEADS):
        oh = o[:, h * HEAD_V:(h + 1) * HEAD_V]
        o_n.append(_rms(oh, gng_ref[...]))
    gla_out = jnp.concatenate(o_n, axis=-1) * (go * jax.nn.sigmoid(go))

    xn = x + _dot(conv_out.astype(BF16), wout_ref[0:D_CONV, :]) + _dot(gla_out.astype(BF16), wout_ref[D_CONV:, :])
    xo_ref[...] = xn
    hf = _rms(xn, fng_ref[...])
    hf_ref[...] = hf

    lg = _dot(hf.astype(BF16), rw_ref[...]) + rb_ref[...]
    lane = lax.broadcasted_iota(jnp.int32, (TM, LANES), 1).astype(F32)
    neg = jnp.float32(-1e30)
    is_g = lane < N_GROUPS
    gm = jnp.max(jnp.where(is_g, lg, neg), axis=-1, keepdims=True)
    ge = jnp.where(is_g, jnp.exp(jnp.where(is_g, lg, neg) - gm), 0.0)
    gp = ge / jnp.sum(ge, axis=-1, keepdims=True)
    grp_p = jnp.max(gp, axis=-1, keepdims=True)
    gidx = jnp.min(jnp.where(is_g & (gp == grp_p), lane, float(LANES)), axis=-1, keepdims=True)
    lo = EXPERT_LANE0 + gidx * EXPERTS_PER_GROUP
    in_sel = (lane >= lo) & (lane < lo + EXPERTS_PER_GROUP)
    sl = jnp.where(in_sel, lg, neg)
    sm = jnp.max(sl, axis=-1, keepdims=True)
    se = jnp.where(in_sel, jnp.exp(sl - sm), 0.0)
    sp = se / jnp.sum(se, axis=-1, keepdims=True)
    w1 = jnp.max(jnp.where(in_sel, sp, -1.0), axis=-1, keepdims=True)
    i1 = jnp.min(jnp.where(in_sel & (sp == w1), lane, float(LANES)), axis=-1, keepdims=True)
    rest = in_sel & (lane != i1)
    w2 = jnp.max(jnp.where(rest, sp, -1.0), axis=-1, keepdims=True)
    i2 = jnp.min(jnp.where(rest & (sp == w2), lane, float(LANES)), axis=-1, keepdims=True)
    den = w1 + w2
    g1 = grp_p * (w1 / den)
    g2 = grp_p * (w2 / den)
    oh1 = lane == i1
    oh2 = lane == i2
    oh_any = jnp.where(oh1 | oh2, 1.0, 0.0)
    strict_lower = (col < row).astype(BF16)
    prefix = _dot(strict_lower, oh_any.astype(BF16)) + carry_ref[0:1, :]
    r1 = jnp.sum(jnp.where(oh1, prefix, 0.0), axis=-1, keepdims=True)
    r2 = jnp.sum(jnp.where(oh2, prefix, 0.0), axis=-1, keepdims=True)
    carry = carry_ref[0:1, :] + jnp.sum(oh_any, axis=0, keepdims=True)
    carry_ref[...] = jnp.broadcast_to(carry, carry_ref.shape)
    cnt_ref[...] = jnp.broadcast_to(carry, cnt_ref.shape)
    meta = jnp.where(lane == 0, i1 - EXPERT_LANE0,
           jnp.where(lane == 1, i2 - EXPERT_LANE0,
           jnp.where(lane == 2, g1,
           jnp.where(lane == 3, g2,
           jnp.where(lane == 4, r1,
           jnp.where(lane == 5, r2, 0.0))))))
    meta_ref[...] = meta


def _mixer(x, ang, win, cw, cb, lng, lnb, gkw, gkb, gng, wout, fng, rw, rb):
    B, S, D = x.shape
    n_s = S // TM
    const = lambda shape: pl.BlockSpec(shape, lambda b, s: (0,) * len(shape))
    tile = lambda w: pl.BlockSpec((None, TM, w), lambda b, s: (b, s, 0))
    return pl.pallas_call(
        _mixer_kernel,
        grid=(B, n_s),
        in_specs=[
            tile(D),
            const((1, D)), const((D, D_IN_PAD)), const((HALO, D_CONV)), const((1, D_CONV)), const((1, D_CONV)),
            const((1, D_CONV)), const((LANES, D_GLA_K)), const((1, D_GLA_K)), const((1, HEAD_V)),
            const((D, D)), const((1, D)), const((D, LANES)), const((1, LANES)),
        ],
        out_specs=[tile(D), tile(D), tile(LANES), pl.BlockSpec((8, LANES), lambda b, s: (0, 0))],
        out_shape=[
            jax.ShapeDtypeStruct((B, S, D), F32),
            jax.ShapeDtypeStruct((B, S, D), F32),
            jax.ShapeDtypeStruct((B, S, LANES), F32),
            jax.ShapeDtypeStruct((8, LANES), F32),
        ],
        scratch_shapes=[
            pltpu.VMEM((TM + HALO, D_CONV), F32),
            pltpu.VMEM((TM, D_CONV), F32),
            pltpu.VMEM((D_GLA_V, D_GLA_K), F32),
            pltpu.VMEM((8, LANES), F32),
        ],
        compiler_params=pltpu.CompilerParams(
            dimension_semantics=("arbitrary", "arbitrary"), vmem_limit_bytes=VMEM_LIMIT),
        name="mixer",
    )(x, ang, win, cw, cb, lng, lnb, gkw, gkb, gng, wout, fng, rw, rb)


def _dispatch_kernel(dest_ref, hf_ref, xs_in_ref, xs_ref, sem):
    del xs_in_ref

    def row_copy(t, d):
        return pltpu.make_async_copy(hf_ref.at[pl.ds(t, 1)], xs_ref.at[pl.ds(d, 1)], sem)

    def issue(t, carry):
        for kk in range(TOP_K):
            row_copy(t, dest_ref[0, 0, TOP_K * t + kk]).start()
        return carry

    lax.fori_loop(0, TD, issue, 0)
    for _ in range(TOP_K):
        pltpu.make_async_copy(hf_ref, xs_ref.at[pl.ds(0, TD)], sem).wait()


def _dispatch(dest, hf, cap):
    T, D = hf.shape
    n = T // TD
    xs0 = jnp.zeros((cap, D), F32)
    return pl.pallas_call(
        _dispatch_kernel,
        grid=(n,),
        in_specs=[
            pl.BlockSpec((1, 1, TOP_K * TD), lambda i: (i, 0, 0), memory_space=pltpu.SMEM),
            pl.BlockSpec((TD, D), lambda i: (i, 0)),
            pl.BlockSpec(memory_space=pl.ANY),
        ],
        out_specs=pl.BlockSpec(memory_space=pl.ANY),
        out_shape=jax.ShapeDtypeStruct((cap, D), F32),
        scratch_shapes=[pltpu.SemaphoreType.DMA(())],
        input_output_aliases={2: 0},
        compiler_params=pltpu.CompilerParams(dimension_semantics=("arbitrary",)),
        name="dispatch",
    )(dest.reshape(n, 1, TOP_K * TD), hf, xs0)


def _expert_kernel(be_ref, xs_ref, wg_ref, wu_ref, wd_ref, yb_ref):
    del be_ref
    xb = xs_ref[...].astype(BF16)
    g = _dot(xb, wg_ref[...])
    u = _dot(xb, wu_ref[...])
    h = (g * jax.nn.sigmoid(g) * u).astype(BF16)
    yb_ref[...] = _dot(h, wd_ref[...])


def _experts(block_expert, xs, wg, wu, wd):
    cap, D = xs.shape
    n_blocks = cap // MOE_BLOCK
    grid_spec = pltpu.PrefetchScalarGridSpec(
        num_scalar_prefetch=1,
        grid=(n_blocks,),
        in_specs=[
            pl.BlockSpec((MOE_BLOCK, D), lambda i, be: (i, 0)),
            pl.BlockSpec((None, D, D_EXPERT), lambda i, be: (be[i], 0, 0)),
            pl.BlockSpec((None, D, D_EXPERT), lambda i, be: (be[i], 0, 0)),
            pl.BlockSpec((None, D_EXPERT, D), lambda i, be: (be[i], 0, 0)),
        ],
        out_specs=pl.BlockSpec((MOE_BLOCK, D), lambda i, be: (i, 0)),
    )
    return pl.pallas_call(
        _expert_kernel,
        grid_spec=grid_spec,
        out_shape=jax.ShapeDtypeStruct((cap, D), F32),
        compiler_params=pltpu.CompilerParams(
            dimension_semantics=("arbitrary",), vmem_limit_bytes=VMEM_LIMIT),
        name="experts",
    )(block_expert, xs, wg, wu, wd)


def _combine_kernel(dest_ref, x_ref, meta_ref, fg_ref, yb_ref, o_ref, buf, sem, *, final_norm):
    def row_copy(kk, t, d):
        return pltpu.make_async_copy(yb_ref.at[pl.ds(d, 1)], buf.at[kk, pl.ds(t, 1)], sem)

    def issue(t, carry):
        for kk in range(TOP_K):
            row_copy(kk, t, dest_ref[0, 0, TOP_K * t + kk]).start()
        return carry

    lax.fori_loop(0, TD, issue, 0)
    for kk in range(TOP_K):
        pltpu.make_async_copy(yb_ref.at[pl.ds(0, TD)], buf.at[kk], sem).wait()
    meta = meta_ref[...]
    out = x_ref[...] + buf[0] * meta[:, 2:3] + buf[1] * meta[:, 3:4]
    if final_norm:
        out = _rms(out, fg_ref[...])
    o_ref[...] = out


def _combine(dest, x, meta, fg, yb, final_norm):
    T, D = x.shape
    n = T // TD
    return pl.pallas_call(
        functools.partial(_combine_kernel, final_norm=final_norm),
        grid=(n,),
        in_specs=[
            pl.BlockSpec((1, 1, TOP_K * TD), lambda i: (i, 0, 0), memory_space=pltpu.SMEM),
            pl.BlockSpec((TD, D), lambda i: (i, 0)),
            pl.BlockSpec((TD, LANES), lambda i: (i, 0)),
            pl.BlockSpec((1, D), lambda i: (0, 0)),
            pl.BlockSpec(memory_space=pl.ANY),
        ],
        out_specs=pl.BlockSpec((TD, D), lambda i: (i, 0)),
        out_shape=jax.ShapeDtypeStruct((T, D), F32),
        scratch_shapes=[pltpu.VMEM((TOP_K, TD, D), F32), pltpu.SemaphoreType.DMA(())],
        compiler_params=pltpu.CompilerParams(
            dimension_semantics=("arbitrary",), vmem_limit_bytes=VMEM_LIMIT),
        name="combine",
    )(dest.reshape(n, 1, TOP_K * TD), x, meta, fg, yb)


def _routing_tables(meta, cnt):
    T = meta.shape[0]
    eid = meta[:, 0:TOP_K].astype(jnp.int32)
    rank = meta[:, 4:4 + TOP_K].astype(jnp.int32)
    counts = cnt[0, EXPERT_LANE0:EXPERT_LANE0 + N_EXPERTS].astype(jnp.int32)
    padded = (counts + MOE_BLOCK - 1) // MOE_BLOCK * MOE_BLOCK
    pend = jnp.cumsum(padded)
    pstart = pend - padded
    dest = (pstart[eid] + rank).reshape(-1)
    n_blocks = (T * TOP_K + MOE_BLOCK - 1) // MOE_BLOCK + N_EXPERTS
    block_expert = jnp.minimum(
        jnp.searchsorted(pend, jnp.arange(n_blocks, dtype=jnp.int32) * MOE_BLOCK, side='right'),
        N_EXPERTS - 1).astype(jnp.int32)
    return dest, block_expert, n_blocks * MOE_BLOCK


def kernel(x, attn_norm_g, w_in, conv_w, conv_b, conv_ln_g, conv_ln_b, gk_w, gk_b, gla_norm_g, w_out, ffn_norm_g,
           router_group_w, router_group_b, router_expert_w, router_expert_b, expert_w_gate, expert_w_up,
           expert_w_down, final_norm_g):
    B, S, D = x.shape
    T = B * S
    depth = w_in.shape[0]
    for l in range(depth):
        win = jnp.pad(w_in[l], ((0, 0), (0, D_IN_PAD - w_in.shape[2]))).astype(BF16)
        cw = jnp.pad(conv_w[l], ((0, HALO - CONV_WIDTH), (0, 0)))
        gkw = jnp.pad(gk_w[l], ((0, LANES - GATE_RANK), (0, 0))).astype(BF16)
        rw = jnp.pad(jnp.concatenate([router_group_w[l], router_expert_w[l]], axis=1),
                     ((0, 0), (0, LANES - N_GROUPS - N_EXPERTS))).astype(BF16)
        rb = jnp.pad(jnp.concatenate([router_group_b[l], router_expert_b[l].reshape(-1)]),
                     (0, LANES - N_GROUPS - N_EXPERTS)).reshape(1, LANES)
        xn, hf, meta, cnt = _mixer(
            x, attn_norm_g[l].reshape(1, D), win, cw, conv_b[l].reshape(1, -1), conv_ln_g[l].reshape(1, -1),
            conv_ln_b[l].reshape(1, -1), gkw, gk_b[l].reshape(1, -1), gla_norm_g[l].reshape(1, -1),
            w_out[l].astype(BF16), ffn_norm_g[l].reshape(1, D), rw, rb)
        meta = meta.reshape(T, LANES)
        dest, block_expert, cap = _routing_tables(meta, cnt)
        xs = _dispatch(dest, hf.reshape(T, D), cap)
        yb = _experts(block_expert, xs, expert_w_gate[l].astype(BF16), expert_w_up[l].astype(BF16),
                      expert_w_down[l].astype(BF16))
        x = _combine(dest, xn.reshape(T, D), meta, final_norm_g.reshape(1, D), yb,
                     final_norm=(l == depth - 1)).reshape(B, S, D)
    return x
```

```python
import functools

import jax
import jax.numpy as jnp
from jax import lax
from jax.experimental import pallas as pl
from jax.experimental.pallas import tpu as pltpu

D_MODEL = 1024
D_CONV = 512
D_GLA_V = 512
GLA_HEADS = 4
D_GLA_K = 256
HEAD_K = 64
HEAD_V = 128
GATE_RANK = 16
GATE_NORMALIZER = 16.0
CHUNK = 64
CONV_WIDTH = 31
N_GROUPS = 4
EXPERTS_PER_GROUP = 8
N_EXPERTS = 32
TOP_K = 2
D_EXPERT = 512
MOE_BLOCK = 256
EPS = 1e-6

LANES = 128
SUBLANES = 8
D_IN_MAIN = 2 * D_CONV + 2 * D_GLA_K + 2 * D_GLA_V
D_IN_PAD = D_IN_MAIN + LANES
TM = 256
HALO = 32
CONV_ROWS = 32
SHIFT_ROWS = TM + HALO - SUBLANES
TD = 256
EXPERT_ROW0 = 8
R_ROWS = 48
VMEM_LIMIT = 48 * 1024 * 1024

F32 = jnp.float32
BF16 = jnp.bfloat16


def _dot(a, b):
    return jnp.dot(a, b, preferred_element_type=F32)


def _dot_nt(a, b):
    return lax.dot_general(a, b, (((1,), (1,)), ((), ())), preferred_element_type=F32)


def _dot_tn(a, b):
    return lax.dot_general(a, b, (((0,), (0,)), ((), ())), preferred_element_type=F32)


def _split_bf16(x):
    hi = x.astype(BF16)
    lo = (x - hi.astype(F32)).astype(BF16)
    return hi, lo


def _rms(x, g):
    return x * lax.rsqrt(jnp.mean(x * x, axis=-1, keepdims=True) + EPS) * g


def _mixer_kernel(x_ref, ang_ref, win_ref, cw_ref, cb_ref, lng_ref, lnb_ref, gkw_ref, gkb_ref, gng_ref,
                  wout_ref, fng_ref, rwt_ref, rbt_ref,
                  xo_ref, hf_ref, meta_ref, cnt_ref,
                  ubuf, sbuf, ybuf, st_ref, carry_ref):
    bi = pl.program_id(0)
    si = pl.program_id(1)

    @pl.when(si == 0)
    def _():
        ubuf[0:HALO, :] = jnp.zeros((HALO, D_CONV), F32)
        st_ref[...] = jnp.zeros_like(st_ref)

    @pl.when((bi == 0) & (si == 0))
    def _():
        carry_ref[...] = jnp.zeros_like(carry_ref)

    x = x_ref[...]
    hn = _rms(x, ang_ref[...]).astype(BF16)
    proj = _dot(hn, win_ref[...])
    o0 = 0
    ua = proj[:, o0:o0 + D_CONV]; o0 += D_CONV
    ug = proj[:, o0:o0 + D_CONV]; o0 += D_CONV
    q = proj[:, o0:o0 + D_GLA_K]; o0 += D_GLA_K
    k = proj[:, o0:o0 + D_GLA_K]; o0 += D_GLA_K
    v = proj[:, o0:o0 + D_GLA_V]; o0 += D_GLA_V
    go = proj[:, o0:o0 + D_GLA_V]; o0 += D_GLA_V
    gkl = proj[:, o0:o0 + LANES]

    ubuf[HALO:HALO + TM, :] = ua * jax.nn.sigmoid(ug)
    for r in range(1, SUBLANES):
        sbuf[r - 1] = ubuf[r:r + SHIFT_ROWS, :]
    for c in range(TM // CONV_ROWS):
        acc = jnp.broadcast_to(cb_ref[...], (CONV_ROWS, D_CONV))
        for j in range(CONV_WIDTH):
            off = HALO - (CONV_WIDTH - 1) + j + c * CONV_ROWS
            a0, r = off - off % SUBLANES, off % SUBLANES
            tap = ubuf[a0:a0 + CONV_ROWS, :] if r == 0 else sbuf[r - 1, a0:a0 + CONV_ROWS, :]
            acc = acc + cw_ref[j] * tap
        ybuf[c * CONV_ROWS:(c + 1) * CONV_ROWS, :] = acc
    ubuf[0:HALO, :] = ubuf[TM:TM + HALO, :]
    y = ybuf[...]
    mu = jnp.mean(y, axis=-1, keepdims=True)
    yc = y - mu
    var = jnp.mean(yc * yc, axis=-1, keepdims=True)
    yn = yc * lax.rsqrt(var + EPS) * lng_ref[...] + lnb_ref[...]
    conv_out = yn * jax.nn.sigmoid(yn)

    gk = jax.nn.log_sigmoid(_dot(gkl.astype(BF16), gkw_ref[...]) + gkb_ref[...]) / GATE_NORMALIZER
    row = lax.broadcasted_iota(jnp.int32, (TM, TM), 0)
    col = lax.broadcasted_iota(jnp.int32, (TM, TM), 1)
    same_chunk = (row // CHUNK) == (col // CHUNK)
    causal = same_chunk & (col <= row)
    gk_hi, gk_lo = _split_bf16(gk)
    l_cum = causal.astype(BF16)
    l_all = same_chunk.astype(BF16)
    b = _dot(l_cum, gk_hi) + _dot(l_cum, gk_lo)
    bl = _dot(l_all, gk_hi) + _dot(l_all, gk_lo)
    qt = (q * (HEAD_K ** -0.5) * jnp.exp(b)).astype(BF16)
    kt = (k * jnp.exp(-b)).astype(BF16)
    ks = (k * jnp.exp(bl - b)).astype(BF16)
    dec = jnp.exp(bl)
    vb = v.astype(BF16)

    klane = lax.broadcasted_iota(jnp.int32, (1, D_GLA_K), 1) // HEAD_K
    o_parts = []
    for h in range(GLA_HEADS):
        qh = jnp.where(klane == h, qt, jnp.zeros_like(qt))
        a = jnp.where(causal, _dot_nt(qh, kt), 0.0).astype(BF16)
        o_parts.append(_dot(a, vb[:, h * HEAD_V:(h + 1) * HEAD_V]))
    o_intra = jnp.concatenate(o_parts, axis=-1)

    srow = lax.broadcasted_iota(jnp.int32, (D_GLA_V, D_GLA_K), 0) // HEAD_V
    scol = lax.broadcasted_iota(jnp.int32, (D_GLA_V, D_GLA_K), 1) // HEAD_K
    head_diag = srow == scol
    o_inter = []
    for c in range(TM // CHUNK):
        r0 = c * CHUNK
        st = st_ref[...]
        o_inter.append(_dot_nt(qt[r0:r0 + CHUNK], st.astype(BF16)))
        ut = _dot_tn(vb[r0:r0 + CHUNK], ks[r0:r0 + CHUNK])
        st_ref[...] = st * dec[r0:r0 + 1, :] + jnp.where(head_diag, ut, 0.0)
    o = o_intra + jnp.concatenate(o_inter, axis=0)
    o_n = []
    for h in range(GLA_HEADS):
        oh = o[:, h * HEAD_V:(h + 1) * HEAD_V]
        o_n.append(_rms(oh, gng_ref[...]))
    gla_out = jnp.concatenate(o_n, axis=-1) * (go * jax.nn.sigmoid(go))

    xn = x + _dot(conv_out.astype(BF16), wout_ref[0:D_CONV, :]) + _dot(gla_out.astype(BF16), wout_ref[D_CONV:, :])
    xo_ref[...] = xn
    hf = _rms(xn, fng_ref[...])
    hf_ref[...] = hf

    lgt = _dot_nt(rwt_ref[...], hf.astype(BF16))[0:R_ROWS] + rbt_ref[0:R_ROWS]
    rowf = lax.broadcasted_iota(jnp.int32, (R_ROWS, TM), 0).astype(F32)
    neg = jnp.float32(-1e30)
    big = jnp.float32(R_ROWS)
    is_g = rowf < N_GROUPS
    gl = jnp.where(is_g, lgt, neg)
    gm = jnp.max(gl, axis=0, keepdims=True)
    ge = jnp.where(is_g, jnp.exp(gl - gm), 0.0)
    gp = ge / jnp.sum(ge, axis=0, keepdims=True)
    grp_p = jnp.max(gp, axis=0, keepdims=True)
    gidx = jnp.min(jnp.where(is_g & (gp == grp_p), rowf, big), axis=0, keepdims=True)
    lo = EXPERT_ROW0 + gidx * EXPERTS_PER_GROUP
    in_sel = (rowf >= lo) & (rowf < lo + EXPERTS_PER_GROUP)
    sl = jnp.where(in_sel, lgt, neg)
    sm = jnp.max(sl, axis=0, keepdims=True)
    se = jnp.where(in_sel, jnp.exp(sl - sm), 0.0)
    sp = se / jnp.sum(se, axis=0, keepdims=True)
    w1 = jnp.max(jnp.where(in_sel, sp, -1.0), axis=0, keepdims=True)
    i1 = jnp.min(jnp.where(in_sel & (sp == w1), rowf, big), axis=0, keepdims=True)
    rest = in_sel & (rowf != i1)
    w2 = jnp.max(jnp.where(rest, sp, -1.0), axis=0, keepdims=True)
    i2 = jnp.min(jnp.where(rest & (sp == w2), rowf, big), axis=0, keepdims=True)
    den = w1 + w2
    g1 = grp_p * (w1 / den)
    g2 = grp_p * (w2 / den)
    oh1 = rowf == i1
    oh2 = rowf == i2
    oh_any = jnp.where(oh1 | oh2, 1.0, 0.0).astype(BF16)
    earlier = (row < col).astype(BF16)
    carry = carry_ref[...]
    prefix = _dot(oh_any, earlier) + carry
    r1 = jnp.sum(jnp.where(oh1, prefix, 0.0), axis=0, keepdims=True)
    r2 = jnp.sum(jnp.where(oh2, prefix, 0.0), axis=0, keepdims=True)
    carry = carry + _dot(oh_any, jnp.ones((TM, TM), BF16))
    carry_ref[...] = carry
    cnt_ref[...] = carry[:, 0:LANES]
    mrow = lax.broadcasted_iota(jnp.int32, (SUBLANES, TM), 0)
    meta_ref[...] = jnp.where(mrow == 0, i1 - EXPERT_ROW0,
                    jnp.where(mrow == 1, i2 - EXPERT_ROW0,
                    jnp.where(mrow == 2, g1,
                    jnp.where(mrow == 3, g2,
                    jnp.where(mrow == 4, r1,
                    jnp.where(mrow == 5, r2, 0.0))))))


def _mixer(x, ang, win, cw, cb, lng, lnb, gkw, gkb, gng, wout, fng, rw, rb):
    B, S, D = x.shape
    n_s = S // TM
    const = lambda shape: pl.BlockSpec(shape, lambda b, s: (0,) * len(shape))
    tile = lambda w: pl.BlockSpec((None, TM, w), lambda b, s: (b, s, 0))
    return pl.pallas_call(
        _mixer_kernel,
        grid=(B, n_s),
        in_specs=[
            tile(D),
            const((1, D)), const((D, D_IN_PAD)), const((CONV_WIDTH, CONV_ROWS, D_CONV)), const((1, D_CONV)),
            const((1, D_CONV)), const((1, D_CONV)), const((LANES, D_GLA_K)), const((1, D_GLA_K)), const((1, HEAD_V)),
            const((D, D)), const((1, D)), const((LANES, D)), const((LANES, TM)),
        ],
        out_specs=[tile(D), tile(D),
                   pl.BlockSpec((SUBLANES, TM), lambda b, s: (0, b * n_s + s)),
                   pl.BlockSpec((R_ROWS, LANES), lambda b, s: (0, 0))],
        out_shape=[
            jax.ShapeDtypeStruct((B, S, D), F32),
            jax.ShapeDtypeStruct((B, S, D), F32),
            jax.ShapeDtypeStruct((SUBLANES, B * S), F32),
            jax.ShapeDtypeStruct((R_ROWS, LANES), F32),
        ],
        scratch_shapes=[
            pltpu.VMEM((TM + HALO, D_CONV), F32),
            pltpu.VMEM((SUBLANES - 1, SHIFT_ROWS, D_CONV), F32),
            pltpu.VMEM((TM, D_CONV), F32),
            pltpu.VMEM((D_GLA_V, D_GLA_K), F32),
            pltpu.VMEM((R_ROWS, TM), F32),
        ],
        compiler_params=pltpu.CompilerParams(
            dimension_semantics=("arbitrary", "arbitrary"), vmem_limit_bytes=VMEM_LIMIT),
        name="mixer",
    )(x, ang, win, cw, cb, lng, lnb, gkw, gkb, gng, wout, fng, rw, rb)


def _dispatch_kernel(dest_ref, hf_ref, xs_in_ref, xs_ref, sem):
    del xs_in_ref

    def row_copy(t, d):
        return pltpu.make_async_copy(hf_ref.at[pl.ds(t, 1)], xs_ref.at[pl.ds(d, 1)], sem)

    def issue(t, carry):
        for kk in range(TOP_K):
            row_copy(t, dest_ref[kk, t]).start()
        return carry

    lax.fori_loop(0, TD, issue, 0)
    for _ in range(TOP_K):
        pltpu.make_async_copy(hf_ref, xs_ref.at[pl.ds(0, TD)], sem).wait()


def _dispatch(dest, hf, cap):
    T, D = hf.shape
    n = T // TD
    xs0 = jnp.zeros((cap, D), F32)
    return pl.pallas_call(
        _dispatch_kernel,
        grid=(n,),
        in_specs=[
            pl.BlockSpec((TOP_K, TD), lambda i: (0, i), memory_space=pltpu.SMEM),
            pl.BlockSpec((TD, D), lambda i: (i, 0)),
            pl.BlockSpec(memory_space=pl.ANY),
        ],
        out_specs=pl.BlockSpec(memory_space=pl.ANY),
        out_shape=jax.ShapeDtypeStruct((cap, D), F32),
        scratch_shapes=[pltpu.SemaphoreType.DMA(())],
        input_output_aliases={2: 0},
        compiler_params=pltpu.CompilerParams(dimension_semantics=("arbitrary",)),
        name="dispatch",
    )(dest, hf, xs0)


def _expert_kernel(be_ref, xs_ref, wg_ref, wu_ref, wd_ref, yb_ref, wg_b, wu_b, wd_b):
    i = pl.program_id(0)

    @pl.when((i == 0) | (be_ref[i] != be_ref[jnp.maximum(i - 1, 0)]))
    def _():
        wg_b[...] = wg_ref[...].astype(BF16)
        wu_b[...] = wu_ref[...].astype(BF16)
        wd_b[...] = wd_ref[...].astype(BF16)

    xb = xs_ref[...].astype(BF16)
    g = _dot(xb, wg_b[...])
    u = _dot(xb, wu_b[...])
    h = (g * jax.nn.sigmoid(g) * u).astype(BF16)
    yb_ref[...] = _dot(h, wd_b[...])


def _experts(block_expert, xs, wg, wu, wd, layer):
    cap, D = xs.shape
    n_blocks = cap // MOE_BLOCK
    grid_spec = pltpu.PrefetchScalarGridSpec(
        num_scalar_prefetch=1,
        grid=(n_blocks,),
        in_specs=[
            pl.BlockSpec((MOE_BLOCK, D), lambda i, be: (i, 0)),
            pl.BlockSpec((None, None, D, D_EXPERT), lambda i, be: (layer, be[i], 0, 0)),
            pl.BlockSpec((None, None, D, D_EXPERT), lambda i, be: (layer, be[i], 0, 0)),
            pl.BlockSpec((None, None, D_EXPERT, D), lambda i, be: (layer, be[i], 0, 0)),
        ],
        out_specs=pl.BlockSpec((MOE_BLOCK, D), lambda i, be: (i, 0)),
        scratch_shapes=[pltpu.VMEM((D, D_EXPERT), BF16), pltpu.VMEM((D, D_EXPERT), BF16),
                        pltpu.VMEM((D_EXPERT, D), BF16)],
    )
    return pl.pallas_call(
        _expert_kernel,
        grid_spec=grid_spec,
        out_shape=jax.ShapeDtypeStruct((cap, D), F32),
        compiler_params=pltpu.CompilerParams(
            dimension_semantics=("arbitrary",), vmem_limit_bytes=VMEM_LIMIT),
        name="experts",
    )(block_expert, xs, wg, wu, wd)


def _combine_kernel(dest_ref, x_ref, meta_ref, fg_ref, yb_ref, o_ref, buf, sem, *, final_norm):
    def row_copy(kk, t, d):
        return pltpu.make_async_copy(yb_ref.at[pl.ds(d, 1)], buf.at[kk, pl.ds(t, 1)], sem)

    def issue(t, carry):
        for kk in range(TOP_K):
            row_copy(kk, t, dest_ref[kk, t]).start()
        return carry

    lax.fori_loop(0, TD, issue, 0)
    for kk in range(TOP_K):
        pltpu.make_async_copy(yb_ref.at[pl.ds(0, TD)], buf.at[kk], sem).wait()
    meta = jnp.transpose(jnp.concatenate([meta_ref[...], jnp.zeros((LANES - SUBLANES, TD), F32)], axis=0))
    out = x_ref[...] + buf[0] * meta[:, 2:3] + buf[1] * meta[:, 3:4]
    if final_norm:
        out = _rms(out, fg_ref[...])
    o_ref[...] = out


def _combine(dest, x, meta, fg, yb, final_norm):
    T, D = x.shape
    n = T // TD
    return pl.pallas_call(
        functools.partial(_combine_kernel, final_norm=final_norm),
        grid=(n,),
        in_specs=[
            pl.BlockSpec((TOP_K, TD), lambda i: (0, i), memory_space=pltpu.SMEM),
            pl.BlockSpec((TD, D), lambda i: (i, 0)),
            pl.BlockSpec((SUBLANES, TD), lambda i: (0, i)),
            pl.BlockSpec((1, D), lambda i: (0, 0)),
            pl.BlockSpec(memory_space=pl.ANY),
        ],
        out_specs=pl.BlockSpec((TD, D), lambda i: (i, 0)),
        out_shape=jax.ShapeDtypeStruct((T, D), F32),
        scratch_shapes=[pltpu.VMEM((TOP_K, TD, D), F32), pltpu.SemaphoreType.DMA(())],
        compiler_params=pltpu.CompilerParams(
            dimension_semantics=("arbitrary",), vmem_limit_bytes=VMEM_LIMIT),
        name="combine",
    )(dest, x, meta, fg, yb)


def _routing_tables(meta_t, cnt):
    T = meta_t.shape[1]
    eid = meta_t[0:TOP_K].astype(jnp.int32)
    rank = meta_t[4:4 + TOP_K].astype(jnp.int32)
    counts = cnt[EXPERT_ROW0:EXPERT_ROW0 + N_EXPERTS, 0].astype(jnp.int32)
    padded = (counts + MOE_BLOCK - 1) // MOE_BLOCK * MOE_BLOCK
    pend = jnp.cumsum(padded)
    pstart = pend - padded
    dest = jnp.take(pstart, eid) + rank
    n_blocks = (T * TOP_K + MOE_BLOCK - 1) // MOE_BLOCK + N_EXPERTS
    block_row0 = jnp.arange(n_blocks, dtype=jnp.int32) * MOE_BLOCK
    block_expert = jnp.minimum(
        jnp.sum((pend[None, :] <= block_row0[:, None]).astype(jnp.int32), axis=1), N_EXPERTS - 1)
    return dest, block_expert, n_blocks * MOE_BLOCK


def kernel(x, attn_norm_g, w_in, conv_w, conv_b, conv_ln_g, conv_ln_b, gk_w, gk_b, gla_norm_g, w_out, ffn_norm_g,
           router_group_w, router_group_b, router_expert_w, router_expert_b, expert_w_gate, expert_w_up,
           expert_w_down, final_norm_g):
    B, S, D = x.shape
    T = B * S
    depth = w_in.shape[0]
    for l in range(depth):
        win = jnp.pad(w_in[l], ((0, 0), (0, D_IN_PAD - w_in.shape[2]))).astype(BF16)
        cw = jnp.broadcast_to(conv_w[l][:, None, :], (CONV_WIDTH, CONV_ROWS, D_CONV))
        gkw = jnp.pad(gk_w[l], ((0, LANES - GATE_RANK), (0, 0))).astype(BF16)
        pad_g = EXPERT_ROW0 - N_GROUPS
        pad_e = LANES - EXPERT_ROW0 - N_EXPERTS
        rwt = jnp.concatenate([router_group_w[l].T, jnp.zeros((pad_g, D), F32), router_expert_w[l].T,
                               jnp.zeros((pad_e, D), F32)], axis=0).astype(BF16)
        rbt = jnp.concatenate([router_group_b[l], jnp.zeros((pad_g,), F32), router_expert_b[l].reshape(-1),
                               jnp.zeros((pad_e,), F32)])
        rbt = jnp.broadcast_to(rbt[:, None], (LANES, TM))
        xn, hf, meta, cnt = _mixer(
            x, attn_norm_g[l].reshape(1, D), win, cw, conv_b[l].reshape(1, -1), conv_ln_g[l].reshape(1, -1),
            conv_ln_b[l].reshape(1, -1), gkw, gk_b[l].reshape(1, -1), gla_norm_g[l].reshape(1, -1),
            w_out[l].astype(BF16), ffn_norm_g[l].reshape(1, D), rwt, rbt)
        dest, block_expert, cap = _routing_tables(meta, cnt)
        xs = _dispatch(dest, hf.reshape(T, D), cap)
        yb = _experts(block_expert, xs, expert_w_gate, expert_w_up, expert_w_down, l)
        x = _combine(dest, xn.reshape(T, D), meta, final_norm_g.reshape(1, D), yb,
                     final_norm=(l == depth - 1)).reshape(B, S, D)
    return x
```

```python
import functools

import jax
import jax.numpy as jnp
from jax import lax
from jax.experimental import pallas as pl
from jax.experimental.pallas import tpu as pltpu

D_MODEL = 1024
D_CONV = 512
D_GLA_V = 512
GLA_HEADS = 4
D_GLA_K = 256
HEAD_K = 64
HEAD_V = 128
GATE_RANK = 16
GATE_NORMALIZER = 16.0
CHUNK = 64
CONV_WIDTH = 31
N_GROUPS = 4
EXPERTS_PER_GROUP = 8
N_EXPERTS = 32
TOP_K = 2
D_EXPERT = 512
MOE_BLOCK = 256
EPS = 1e-6

LANES = 128
SUBLANES = 8
D_IN_MAIN = 2 * D_CONV + 2 * D_GLA_K + 2 * D_GLA_V
D_IN_PAD = D_IN_MAIN + LANES
TM = 256
HALO = 32
CONV_ROWS = 32
SHIFT_ROWS = TM + HALO - SUBLANES
TD = 256
EXPERT_ROW0 = 8
R_ROWS = 48
VMEM_LIMIT = 48 * 1024 * 1024

F32 = jnp.float32
BF16 = jnp.bfloat16


def _dot(a, b):
    return jnp.dot(a, b, preferred_element_type=F32)


def _dot_nt(a, b):
    return lax.dot_general(a, b, (((1,), (1,)), ((), ())), preferred_element_type=F32)


def _dot_tn(a, b):
    return lax.dot_general(a, b, (((0,), (0,)), ((), ())), preferred_element_type=F32)


def _split_bf16(x):
    hi = x.astype(BF16)
    lo = (x - hi.astype(F32)).astype(BF16)
    return hi, lo


def _rms(x, g):
    return x * lax.rsqrt(jnp.mean(x * x, axis=-1, keepdims=True) + EPS) * g


def _mixer_kernel(x_ref, ang_ref, win_ref, cw_ref, cb_ref, lng_ref, lnb_ref, gkw_ref, gkb_ref, gng_ref,
                  wout_ref, fng_ref, rwt_ref, rbt_ref,
                  xo_ref, hf_ref, meta_ref, cnt_ref,
                  ubuf, sbuf, ybuf, st_ref):
    si = pl.program_id(1)

    @pl.when(si == 0)
    def _():
        ubuf[0:HALO, :] = jnp.zeros((HALO, D_CONV), F32)
        st_ref[...] = jnp.zeros_like(st_ref)

    x = x_ref[...]
    hn = _rms(x, ang_ref[...]).astype(BF16)
    proj = _dot(hn, win_ref[...])
    o0 = 0
    ua = proj[:, o0:o0 + D_CONV]; o0 += D_CONV
    ug = proj[:, o0:o0 + D_CONV]; o0 += D_CONV
    q = proj[:, o0:o0 + D_GLA_K]; o0 += D_GLA_K
    k = proj[:, o0:o0 + D_GLA_K]; o0 += D_GLA_K
    v = proj[:, o0:o0 + D_GLA_V]; o0 += D_GLA_V
    go = proj[:, o0:o0 + D_GLA_V]; o0 += D_GLA_V
    gkl = proj[:, o0:o0 + LANES]

    ubuf[HALO:HALO + TM, :] = ua * jax.nn.sigmoid(ug)
    for r in range(1, SUBLANES):
        sbuf[r - 1] = ubuf[r:r + SHIFT_ROWS, :]
    for c in range(TM // CONV_ROWS):
        acc = jnp.broadcast_to(cb_ref[...], (CONV_ROWS, D_CONV))
        for j in range(CONV_WIDTH):
            off = HALO - (CONV_WIDTH - 1) + j + c * CONV_ROWS
            a0, r = off - off % SUBLANES, off % SUBLANES
            tap = ubuf[a0:a0 + CONV_ROWS, :] if r == 0 else sbuf[r - 1, a0:a0 + CONV_ROWS, :]
            acc = acc + cw_ref[j] * tap
        ybuf[c * CONV_ROWS:(c + 1) * CONV_ROWS, :] = acc
    ubuf[0:HALO, :] = ubuf[TM:TM + HALO, :]
    y = ybuf[...]
    mu = jnp.mean(y, axis=-1, keepdims=True)
    yc = y - mu
    var = jnp.mean(yc * yc, axis=-1, keepdims=True)
    yn = yc * lax.rsqrt(var + EPS) * lng_ref[...] + lnb_ref[...]
    conv_out = yn * jax.nn.sigmoid(yn)

    gk = jax.nn.log_sigmoid(_dot(gkl.astype(BF16), gkw_ref[...]) + gkb_ref[...]) / GATE_NORMALIZER
    row = lax.broadcasted_iota(jnp.int32, (TM, TM), 0)
    col = lax.broadcasted_iota(jnp.int32, (TM, TM), 1)
    same_chunk = (row // CHUNK) == (col // CHUNK)
    causal = same_chunk & (col <= row)
    gk_hi, gk_lo = _split_bf16(gk)
    l_cum = causal.astype(BF16)
    l_all = same_chunk.astype(BF16)
    b = _dot(l_cum, gk_hi) + _dot(l_cum, gk_lo)
    bl = _dot(l_all, gk_hi) + _dot(l_all, gk_lo)
    qt = (q * (HEAD_K ** -0.5) * jnp.exp(b)).astype(BF16)
    kt = (k * jnp.exp(-b)).astype(BF16)
    ks = (k * jnp.exp(bl - b)).astype(BF16)
    dec = jnp.exp(bl)
    vb = v.astype(BF16)

    klane = lax.broadcasted_iota(jnp.int32, (1, D_GLA_K), 1) // HEAD_K
    o_parts = []
    for h in range(GLA_HEADS):
        qh = jnp.where(klane == h, qt, jnp.zeros_like(qt))
        a = jnp.where(causal, _dot_nt(qh, kt), 0.0).astype(BF16)
        o_parts.append(_dot(a, vb[:, h * HEAD_V:(h + 1) * HEAD_V]))
    o_intra = jnp.concatenate(o_parts, axis=-1)

    srow = lax.broadcasted_iota(jnp.int32, (D_GLA_V, D_GLA_K), 0) // HEAD_V
    scol = lax.broadcasted_iota(jnp.int32, (D_GLA_V, D_GLA_K), 1) // HEAD_K
    head_diag = srow == scol
    o_inter = []
    for c in range(TM // CHUNK):
        r0 = c * CHUNK
        st = st_ref[...]
        o_inter.append(_dot_nt(qt[r0:r0 + CHUNK], st.astype(BF16)))
        ut = _dot_tn(vb[r0:r0 + CHUNK], ks[r0:r0 + CHUNK])
        st_ref[...] = st * dec[r0:r0 + 1, :] + jnp.where(head_diag, ut, 0.0)
    o = o_intra + jnp.concatenate(o_inter, axis=0)
    o_n = []
    for h in range(GLA_HEADS):
        oh = o[:, h * HEAD_V:(h + 1) * HEAD_V]
        o_n.append(_rms(oh, gng_ref[...]))
    gla_out = jnp.concatenate(o_n, axis=-1) * (go * jax.nn.sigmoid(go))

    xn = x + _dot(conv_out.astype(BF16), wout_ref[0:D_CONV, :]) + _dot(gla_out.astype(BF16), wout_ref[D_CONV:, :])
    xo_ref[...] = xn
    hf = _rms(xn, fng_ref[...]).astype(BF16)
    hf_ref[...] = hf

    lgt = _dot_nt(rwt_ref[...], hf.astype(BF16))[0:R_ROWS] + rbt_ref[0:R_ROWS]
    rowf = lax.broadcasted_iota(jnp.int32, (R_ROWS, TM), 0).astype(F32)
    neg = jnp.float32(-1e30)
    big = jnp.float32(R_ROWS)
    is_g = rowf < N_GROUPS
    gl = jnp.where(is_g, lgt, neg)
    gm = jnp.max(gl, axis=0, keepdims=True)
    ge = jnp.where(is_g, jnp.exp(gl - gm), 0.0)
    gp = ge / jnp.sum(ge, axis=0, keepdims=True)
    grp_p = jnp.max(gp, axis=0, keepdims=True)
    gidx = jnp.min(jnp.where(is_g & (gp == grp_p), rowf, big), axis=0, keepdims=True)
    lo = EXPERT_ROW0 + gidx * EXPERTS_PER_GROUP
    in_sel = (rowf >= lo) & (rowf < lo + EXPERTS_PER_GROUP)
    sl = jnp.where(in_sel, lgt, neg)
    sm = jnp.max(sl, axis=0, keepdims=True)
    se = jnp.where(in_sel, jnp.exp(sl - sm), 0.0)
    sp = se / jnp.sum(se, axis=0, keepdims=True)
    w1 = jnp.max(jnp.where(in_sel, sp, -1.0), axis=0, keepdims=True)
    i1 = jnp.min(jnp.where(in_sel & (sp == w1), rowf, big), axis=0, keepdims=True)
    rest = in_sel & (rowf != i1)
    w2 = jnp.max(jnp.where(rest, sp, -1.0), axis=0, keepdims=True)
    i2 = jnp.min(jnp.where(rest & (sp == w2), rowf, big), axis=0, keepdims=True)
    den = w1 + w2
    g1 = grp_p * (w1 / den)
    g2 = grp_p * (w2 / den)
    oh1 = rowf == i1
    oh2 = rowf == i2
    oh_any = jnp.where(oh1 | oh2, 1.0, 0.0).astype(BF16)
    earlier = (row < col).astype(BF16)
    ones_tt = jnp.ones((TM, TM), BF16)
    below = (jnp.where(i1 < rowf, 1.0, 0.0) + jnp.where(i2 < rowf, 1.0, 0.0)).astype(BF16)
    base = _dot(below, ones_tt) + _dot(oh_any, earlier)
    p1 = jnp.sum(jnp.where(oh1, base, 0.0), axis=0, keepdims=True)
    p2 = jnp.sum(jnp.where(oh2, base, 0.0), axis=0, keepdims=True)
    cnt_ref[...] = _dot(oh_any, ones_tt)[:, 0:LANES]
    mrow = lax.broadcasted_iota(jnp.int32, (SUBLANES, TM), 0)
    meta_ref[...] = jnp.where(mrow == 0, i1 - EXPERT_ROW0,
                    jnp.where(mrow == 1, i2 - EXPERT_ROW0,
                    jnp.where(mrow == 2, g1,
                    jnp.where(mrow == 3, g2,
                    jnp.where(mrow == 4, p1,
                    jnp.where(mrow == 5, p2, 0.0))))))


def _mixer(x, ang, win, cw, cb, lng, lnb, gkw, gkb, gng, wout, fng, rw, rb):
    B, S, D = x.shape
    n_s = S // TM
    const = lambda shape: pl.BlockSpec(shape, lambda b, s: (0,) * len(shape))
    tile = lambda w: pl.BlockSpec((None, TM, w), lambda b, s: (b, s, 0))
    return pl.pallas_call(
        _mixer_kernel,
        grid=(B, n_s),
        in_specs=[
            tile(D),
            const((1, D)), const((D, D_IN_PAD)), const((CONV_WIDTH, CONV_ROWS, D_CONV)), const((1, D_CONV)),
            const((1, D_CONV)), const((1, D_CONV)), const((LANES, D_GLA_K)), const((1, D_GLA_K)), const((1, HEAD_V)),
            const((D, D)), const((1, D)), const((LANES, D)), const((LANES, TM)),
        ],
        out_specs=[tile(D), tile(D),
                   pl.BlockSpec((SUBLANES, TM), lambda b, s: (0, b * n_s + s)),
                   pl.BlockSpec((None, R_ROWS, LANES), lambda b, s: (b * n_s + s, 0, 0))],
        out_shape=[
            jax.ShapeDtypeStruct((B, S, D), F32),
            jax.ShapeDtypeStruct((B, S, D), BF16),
            jax.ShapeDtypeStruct((SUBLANES, B * S), F32),
            jax.ShapeDtypeStruct((B * n_s, R_ROWS, LANES), F32),
        ],
        scratch_shapes=[
            pltpu.VMEM((TM + HALO, D_CONV), F32),
            pltpu.VMEM((SUBLANES - 1, SHIFT_ROWS, D_CONV), F32),
            pltpu.VMEM((TM, D_CONV), F32),
            pltpu.VMEM((D_GLA_V, D_GLA_K), F32),
        ],
        compiler_params=pltpu.CompilerParams(
            dimension_semantics=("arbitrary", "arbitrary"), vmem_limit_bytes=VMEM_LIMIT),
        name="mixer",
    )(x, ang, win, cw, cb, lng, lnb, gkw, gkb, gng, wout, fng, rw, rb)


PIECES = D_MODEL // LANES
RUN_BITS = 9


def _to_row_tiles(ref, value):
    for c in range(PIECES):
        ref[pl.ds(c, value.shape[0], stride=PIECES), :] = value[:, c * LANES:(c + 1) * LANES]


def _from_row_tiles(ref, n_rows):
    return jnp.concatenate([ref[pl.ds(c, n_rows, stride=PIECES), :] for c in range(PIECES)], axis=1)


def _for_each_run_piece(tile, n_ref, lo_ref, gs_ref, start_copy):
    def run(e, carry):
        n = n_ref[tile * N_EXPERTS + e]
        lo = lo_ref[tile * N_EXPERTS + e]
        gs = gs_ref[tile * N_EXPERTS + e]
        for bit in range(RUN_BITS - 1, -1, -1):
            done = (n >> (bit + 1)) << (bit + 1)

            @pl.when(((n >> bit) & 1) == 1)
            def _():
                start_copy(lo + done, gs + done, 1 << bit)
        return carry

    lax.fori_loop(0, N_EXPERTS, run, 0)


def _slab(ref, row, rows):
    return ref.at[pl.ds(pl.multiple_of(row * PIECES, PIECES), rows * PIECES)]


def _dispatch_kernel(n_ref, lo_ref, gs_ref, hf_ref, meta_ref, xs_in_ref, xs_ref, srt, sem):
    del xs_in_ref
    i = pl.program_id(0)
    pos = meta_ref[4:4 + TOP_K, :]
    prow = lax.broadcasted_iota(jnp.int32, (TOP_K * TD, TD), 0).astype(F32)
    perm = jnp.where((prow == pos[0:1, :]) | (prow == pos[1:2, :]), 1.0, 0.0).astype(BF16)
    _to_row_tiles(srt, _dot(perm, hf_ref[...]))
    _for_each_run_piece(
        i, n_ref, lo_ref, gs_ref,
        lambda lrow, grow, rows: pltpu.make_async_copy(_slab(srt, lrow, rows), _slab(xs_ref, grow, rows), sem).start())
    pltpu.make_async_copy(srt, xs_ref.at[pl.ds(0, TOP_K * TD * PIECES)], sem).wait()


def _dispatch(tables, hf, meta, cap):
    T, D = hf.shape
    n = T // TD
    xs0 = jnp.zeros((cap * PIECES, LANES), F32)
    grid_spec = pltpu.PrefetchScalarGridSpec(
        num_scalar_prefetch=3,
        grid=(n,),
        in_specs=[
            pl.BlockSpec((TD, D), lambda i, *_: (i, 0)),
            pl.BlockSpec((SUBLANES, TD), lambda i, *_: (0, i)),
            pl.BlockSpec(memory_space=pl.ANY),
        ],
        out_specs=pl.BlockSpec(memory_space=pl.ANY),
        scratch_shapes=[pltpu.VMEM((TOP_K * TD * PIECES, LANES), F32), pltpu.SemaphoreType.DMA(())],
    )
    return pl.pallas_call(
        _dispatch_kernel,
        grid_spec=grid_spec,
        out_shape=jax.ShapeDtypeStruct((cap * PIECES, LANES), F32),
        input_output_aliases={5: 0},
        compiler_params=pltpu.CompilerParams(dimension_semantics=("arbitrary",), vmem_limit_bytes=VMEM_LIMIT),
        name="dispatch",
    )(*tables, hf, meta, xs0)


def _expert_kernel(be_ref, xs_ref, wg_ref, wu_ref, wd_ref, yb_ref, wg_b, wu_b, wd_b):
    i = pl.program_id(0)

    @pl.when((i == 0) | (be_ref[i] != be_ref[jnp.maximum(i - 1, 0)]))
    def _():
        wg_b[...] = wg_ref[...].astype(BF16)
        wu_b[...] = wu_ref[...].astype(BF16)
        wd_b[...] = wd_ref[...].astype(BF16)

    xb = _from_row_tiles(xs_ref, MOE_BLOCK).astype(BF16)
    g = _dot(xb, wg_b[...])
    u = _dot(xb, wu_b[...])
    h = (g * jax.nn.sigmoid(g) * u).astype(BF16)
    _to_row_tiles(yb_ref, _dot(h, wd_b[...]))


def _experts(block_expert, xs, wg, wu, wd, layer):
    D = D_MODEL
    n_blocks = xs.shape[0] // (MOE_BLOCK * PIECES)
    grid_spec = pltpu.PrefetchScalarGridSpec(
        num_scalar_prefetch=1,
        grid=(n_blocks,),
        in_specs=[
            pl.BlockSpec((MOE_BLOCK * PIECES, LANES), lambda i, be: (i, 0)),
            pl.BlockSpec((None, None, D, D_EXPERT), lambda i, be: (layer, be[i], 0, 0)),
            pl.BlockSpec((None, None, D, D_EXPERT), lambda i, be: (layer, be[i], 0, 0)),
            pl.BlockSpec((None, None, D_EXPERT, D), lambda i, be: (layer, be[i], 0, 0)),
        ],
        out_specs=pl.BlockSpec((MOE_BLOCK * PIECES, LANES), lambda i, be: (i, 0)),
        scratch_shapes=[pltpu.VMEM((D, D_EXPERT), BF16), pltpu.VMEM((D, D_EXPERT), BF16),
                        pltpu.VMEM((D_EXPERT, D), BF16)],
    )
    return pl.pallas_call(
        _expert_kernel,
        grid_spec=grid_spec,
        out_shape=jax.ShapeDtypeStruct(xs.shape, F32),
        compiler_params=pltpu.CompilerParams(
            dimension_semantics=("arbitrary",), vmem_limit_bytes=VMEM_LIMIT),
        name="experts",
    )(block_expert, xs, wg, wu, wd)


def _combine_kernel(n_ref, lo_ref, gs_ref, x_ref, meta_ref, fg_ref, yb_ref, o_ref, ys, sem, *, final_norm):
    i = pl.program_id(0)
    _for_each_run_piece(
        i, n_ref, lo_ref, gs_ref,
        lambda lrow, grow, rows: pltpu.make_async_copy(_slab(yb_ref, grow, rows), _slab(ys, lrow, rows), sem).start())
    meta = jnp.transpose(jnp.concatenate([meta_ref[...], jnp.zeros((LANES - SUBLANES, TD), F32)], axis=0))
    pcol = lax.broadcasted_iota(jnp.int32, (TD, TOP_K * TD), 1).astype(F32)
    gather = (jnp.where(pcol == meta[:, 4:5], meta[:, 2:3], 0.0)
              + jnp.where(pcol == meta[:, 5:6], meta[:, 3:4], 0.0)).astype(BF16)
    pltpu.make_async_copy(yb_ref.at[pl.ds(0, TOP_K * TD * PIECES)], ys, sem).wait()
    out = x_ref[...] + _dot(gather, _from_row_tiles(ys, TOP_K * TD).astype(BF16))
    if final_norm:
        out = _rms(out, fg_ref[...])
    o_ref[...] = out


def _combine(tables, x, meta, fg, yb, final_norm):
    T, D = x.shape
    n = T // TD
    grid_spec = pltpu.PrefetchScalarGridSpec(
        num_scalar_prefetch=3,
        grid=(n,),
        in_specs=[
            pl.BlockSpec((TD, D), lambda i, *_: (i, 0)),
            pl.BlockSpec((SUBLANES, TD), lambda i, *_: (0, i)),
            pl.BlockSpec((1, D), lambda i, *_: (0, 0)),
            pl.BlockSpec(memory_space=pl.ANY),
        ],
        out_specs=pl.BlockSpec((TD, D), lambda i, *_: (i, 0)),
        scratch_shapes=[pltpu.VMEM((TOP_K * TD * PIECES, LANES), F32), pltpu.SemaphoreType.DMA(())],
    )
    return pl.pallas_call(
        functools.partial(_combine_kernel, final_norm=final_norm),
        grid_spec=grid_spec,
        out_shape=jax.ShapeDtypeStruct((T, D), F32),
        compiler_params=pltpu.CompilerParams(
            dimension_semantics=("arbitrary",), vmem_limit_bytes=VMEM_LIMIT),
        name="combine",
    )(*tables, x, meta, fg, yb)


def _routing_tables(cnt):
    n_tiles = cnt.shape[0]
    n = cnt[:, EXPERT_ROW0:EXPERT_ROW0 + N_EXPERTS, 0].astype(jnp.int32)
    counts = jnp.sum(n, axis=0)
    padded = (counts + MOE_BLOCK - 1) // MOE_BLOCK * MOE_BLOCK
    pend = jnp.cumsum(padded)
    pstart = pend - padded
    local = jnp.cumsum(n, axis=1) - n
    first = pstart[None, :] + jnp.cumsum(n, axis=0) - n
    n_blocks = (n_tiles * TD * TOP_K + MOE_BLOCK - 1) // MOE_BLOCK + N_EXPERTS
    block_row0 = jnp.arange(n_blocks, dtype=jnp.int32) * MOE_BLOCK
    block_expert = jnp.minimum(
        jnp.sum((pend[None, :] <= block_row0[:, None]).astype(jnp.int32), axis=1), N_EXPERTS - 1)
    return (n.reshape(-1), local.reshape(-1), first.reshape(-1)), block_expert, n_blocks * MOE_BLOCK


def kernel(x, attn_norm_g, w_in, conv_w, conv_b, conv_ln_g, conv_ln_b, gk_w, gk_b, gla_norm_g, w_out, ffn_norm_g,
           router_group_w, router_group_b, router_expert_w, router_expert_b, expert_w_gate, expert_w_up,
           expert_w_down, final_norm_g):
    B, S, D = x.shape
    T = B * S
    depth = w_in.shape[0]
    for l in range(depth):
        win = jnp.pad(w_in[l], ((0, 0), (0, D_IN_PAD - w_in.shape[2]))).astype(BF16)
        cw = jnp.broadcast_to(conv_w[l][:, None, :], (CONV_WIDTH, CONV_ROWS, D_CONV))
        gkw = jnp.pad(gk_w[l], ((0, LANES - GATE_RANK), (0, 0))).astype(BF16)
        pad_g = EXPERT_ROW0 - N_GROUPS
        pad_e = LANES - EXPERT_ROW0 - N_EXPERTS
        rwt = jnp.concatenate([router_group_w[l].T, jnp.zeros((pad_g, D), F32), router_expert_w[l].T,
                               jnp.zeros((pad_e, D), F32)], axis=0).astype(BF16)
        rbt = jnp.concatenate([router_group_b[l], jnp.zeros((pad_g,), F32), router_expert_b[l].reshape(-1),
                               jnp.zeros((pad_e,), F32)])
        rbt = jnp.broadcast_to(rbt[:, None], (LANES, TM))
        xn, hf, meta, cnt = _mixer(
            x, attn_norm_g[l].reshape(1, D), win, cw, conv_b[l].reshape(1, -1), conv_ln_g[l].reshape(1, -1),
            conv_ln_b[l].reshape(1, -1), gkw, gk_b[l].reshape(1, -1), gla_norm_g[l].reshape(1, -1),
            w_out[l].astype(BF16), ffn_norm_g[l].reshape(1, D), rwt, rbt)
        tables, block_expert, cap = _routing_tables(cnt)
        xs = _dispatch(tables, hf.reshape(T, D), meta, cap)
        yb = _experts(block_expert, xs, expert_w_gate, expert_w_up, expert_w_down, l)
        x = _combine(tables, xn.reshape(T, D), meta, final_norm_g.reshape(1, D), yb,
                     final_norm=(l == depth - 1)).reshape(B, S, D)
    return x
```

```python
import functools

import jax
import jax.numpy as jnp
from jax import lax
from jax.experimental import pallas as pl
from jax.experimental.pallas import tpu as pltpu

D_MODEL = 1024
D_CONV = 512
D_GLA_V = 512
GLA_HEADS = 4
D_GLA_K = 256
HEAD_K = 64
HEAD_V = 128
GATE_RANK = 16
GATE_NORMALIZER = 16.0
CHUNK = 64
CONV_WIDTH = 31
N_GROUPS = 4
EXPERTS_PER_GROUP = 8
N_EXPERTS = 32
TOP_K = 2
D_EXPERT = 512
MOE_BLOCK = 256
EPS = 1e-6

LANES = 128
SUBLANES = 8
D_IN_MAIN = 2 * D_CONV + 2 * D_GLA_K + 2 * D_GLA_V
D_IN_PAD = D_IN_MAIN + LANES
TM = 256
HALO = 32
CONV_ROWS = 32
SHIFT_ROWS = TM + HALO - SUBLANES
TD = 256
EXPERT_ROW0 = 8
R_ROWS = 48
VMEM_LIMIT = 48 * 1024 * 1024

F32 = jnp.float32
BF16 = jnp.bfloat16


def _dot(a, b):
    return jnp.dot(a, b, preferred_element_type=F32)


def _dot_nt(a, b):
    return lax.dot_general(a, b, (((1,), (1,)), ((), ())), preferred_element_type=F32)


def _dot_tn(a, b):
    return lax.dot_general(a, b, (((0,), (0,)), ((), ())), preferred_element_type=F32)


def _split_bf16(x):
    hi = x.astype(BF16)
    lo = (x - hi.astype(F32)).astype(BF16)
    return hi, lo


def _rms(x, g):
    return x * lax.rsqrt(jnp.mean(x * x, axis=-1, keepdims=True) + EPS) * g


def _mixer_kernel(x_ref, ang_ref, win_ref, cw_ref, cb_ref, lng_ref, lnb_ref, gkw_ref, gkb_ref, gng_ref,
                  wout_ref, fng_ref, rwt_ref, rbt_ref,
                  xo_ref, hf_ref, meta_ref, cnt_ref,
                  ubuf, sbuf, ybuf, st_ref):
    si = pl.program_id(1)

    @pl.when(si == 0)
    def _():
        ubuf[0:HALO, :] = jnp.zeros((HALO, D_CONV), F32)
        st_ref[...] = jnp.zeros_like(st_ref)

    x = x_ref[...]
    hn = _rms(x, ang_ref[...]).astype(BF16)
    proj = _dot(hn, win_ref[...])
    o0 = 0
    ua = proj[:, o0:o0 + D_CONV]; o0 += D_CONV
    ug = proj[:, o0:o0 + D_CONV]; o0 += D_CONV
    q = proj[:, o0:o0 + D_GLA_K]; o0 += D_GLA_K
    k = proj[:, o0:o0 + D_GLA_K]; o0 += D_GLA_K
    v = proj[:, o0:o0 + D_GLA_V]; o0 += D_GLA_V
    go = proj[:, o0:o0 + D_GLA_V]; o0 += D_GLA_V
    gkl = proj[:, o0:o0 + LANES]

    ubuf[HALO:HALO + TM, :] = ua * jax.nn.sigmoid(ug)
    for r in range(1, SUBLANES):
        sbuf[r - 1] = ubuf[r:r + SHIFT_ROWS, :]
    for c in range(TM // CONV_ROWS):
        acc = jnp.broadcast_to(cb_ref[...], (CONV_ROWS, D_CONV))
        for j in range(CONV_WIDTH):
            off = HALO - (CONV_WIDTH - 1) + j + c * CONV_ROWS
            a0, r = off - off % SUBLANES, off % SUBLANES
            tap = ubuf[a0:a0 + CONV_ROWS, :] if r == 0 else sbuf[r - 1, a0:a0 + CONV_ROWS, :]
            acc = acc + cw_ref[j] * tap
        ybuf[c * CONV_ROWS:(c + 1) * CONV_ROWS, :] = acc
    ubuf[0:HALO, :] = ubuf[TM:TM + HALO, :]
    y = ybuf[...]
    mu = jnp.mean(y, axis=-1, keepdims=True)
    yc = y - mu
    var = jnp.mean(yc * yc, axis=-1, keepdims=True)
    yn = yc * lax.rsqrt(var + EPS) * lng_ref[...] + lnb_ref[...]
    conv_out = yn * jax.nn.sigmoid(yn)

    gk = jax.nn.log_sigmoid(_dot(gkl.astype(BF16), gkw_ref[...]) + gkb_ref[...]) / GATE_NORMALIZER
    row = lax.broadcasted_iota(jnp.int32, (TM, TM), 0)
    col = lax.broadcasted_iota(jnp.int32, (TM, TM), 1)
    same_chunk = (row // CHUNK) == (col // CHUNK)
    causal = same_chunk & (col <= row)
    gk_hi, gk_lo = _split_bf16(gk)
    l_cum = causal.astype(BF16)
    l_all = same_chunk.astype(BF16)
    b = _dot(l_cum, gk_hi) + _dot(l_cum, gk_lo)
    bl = _dot(l_all, gk_hi) + _dot(l_all, gk_lo)
    qt = (q * (HEAD_K ** -0.5) * jnp.exp(b)).astype(BF16)
    kt = (k * jnp.exp(-b)).astype(BF16)
    ks = (k * jnp.exp(bl - b)).astype(BF16)
    dec = jnp.exp(bl)
    vb = v.astype(BF16)

    klane = lax.broadcasted_iota(jnp.int32, (1, D_GLA_K), 1) // HEAD_K
    o_parts = []
    for h in range(GLA_HEADS):
        qh = jnp.where(klane == h, qt, jnp.zeros_like(qt))
        a = jnp.where(causal, _dot_nt(qh, kt), 0.0).astype(BF16)
        o_parts.append(_dot(a, vb[:, h * HEAD_V:(h + 1) * HEAD_V]))
    o_intra = jnp.concatenate(o_parts, axis=-1)

    srow = lax.broadcasted_iota(jnp.int32, (D_GLA_V, D_GLA_K), 0) // HEAD_V
    scol = lax.broadcasted_iota(jnp.int32, (D_GLA_V, D_GLA_K), 1) // HEAD_K
    head_diag = srow == scol
    o_inter = []
    for c in range(TM // CHUNK):
        r0 = c * CHUNK
        st = st_ref[...]
        o_inter.append(_dot_nt(qt[r0:r0 + CHUNK], st.astype(BF16)))
        ut = _dot_tn(vb[r0:r0 + CHUNK], ks[r0:r0 + CHUNK])
        st_ref[...] = st * dec[r0:r0 + 1, :] + jnp.where(head_diag, ut, 0.0)
    o = o_intra + jnp.concatenate(o_inter, axis=0)
    o_n = []
    for h in range(GLA_HEADS):
        oh = o[:, h * HEAD_V:(h + 1) * HEAD_V]
        o_n.append(_rms(oh, gng_ref[...]))
    gla_out = jnp.concatenate(o_n, axis=-1) * (go * jax.nn.sigmoid(go))

    xn = x + _dot(conv_out.astype(BF16), wout_ref[0:D_CONV, :]) + _dot(gla_out.astype(BF16), wout_ref[D_CONV:, :])
    xo_ref[...] = xn
    hf = _rms(xn, fng_ref[...]).astype(BF16)
    hf_ref[...] = hf

    lgt = _dot_nt(rwt_ref[...], hf.astype(BF16))[0:R_ROWS] + rbt_ref[0:R_ROWS]
    rowf = lax.broadcasted_iota(jnp.int32, (R_ROWS, TM), 0).astype(F32)
    neg = jnp.float32(-1e30)
    big = jnp.float32(R_ROWS)
    is_g = rowf < N_GROUPS
    gl = jnp.where(is_g, lgt, neg)
    gm = jnp.max(gl, axis=0, keepdims=True)
    ge = jnp.where(is_g, jnp.exp(gl - gm), 0.0)
    gp = ge / jnp.sum(ge, axis=0, keepdims=True)
    grp_p = jnp.max(gp, axis=0, keepdims=True)
    gidx = jnp.min(jnp.where(is_g & (gp == grp_p), rowf, big), axis=0, keepdims=True)
    lo = EXPERT_ROW0 + gidx * EXPERTS_PER_GROUP
    in_sel = (rowf >= lo) & (rowf < lo + EXPERTS_PER_GROUP)
    sl = jnp.where(in_sel, lgt, neg)
    sm = jnp.max(sl, axis=0, keepdims=True)
    se = jnp.where(in_sel, jnp.exp(sl - sm), 0.0)
    sp = se / jnp.sum(se, axis=0, keepdims=True)
    w1 = jnp.max(jnp.where(in_sel, sp, -1.0), axis=0, keepdims=True)
    i1 = jnp.min(jnp.where(in_sel & (sp == w1), rowf, big), axis=0, keepdims=True)
    rest = in_sel & (rowf != i1)
    w2 = jnp.max(jnp.where(rest, sp, -1.0), axis=0, keepdims=True)
    i2 = jnp.min(jnp.where(rest & (sp == w2), rowf, big), axis=0, keepdims=True)
    den = w1 + w2
    g1 = grp_p * (w1 / den)
    g2 = grp_p * (w2 / den)
    oh1 = rowf == i1
    oh2 = rowf == i2
    oh_any = jnp.where(oh1 | oh2, 1.0, 0.0).astype(BF16)
    earlier = (row < col).astype(BF16)
    ones_tt = jnp.ones((TM, TM), BF16)
    below = (jnp.where(i1 < rowf, 1.0, 0.0) + jnp.where(i2 < rowf, 1.0, 0.0)).astype(BF16)
    base = _dot(below, ones_tt) + _dot(oh_any, earlier)
    p1 = jnp.sum(jnp.where(oh1, base, 0.0), axis=0, keepdims=True)
    p2 = jnp.sum(jnp.where(oh2, base, 0.0), axis=0, keepdims=True)
    cnt_ref[...] = _dot(oh_any, ones_tt)[:, 0:LANES]
    mrow = lax.broadcasted_iota(jnp.int32, (SUBLANES, TM), 0)
    meta_ref[...] = jnp.where(mrow == 0, i1 - EXPERT_ROW0,
                    jnp.where(mrow == 1, i2 - EXPERT_ROW0,
                    jnp.where(mrow == 2, g1,
                    jnp.where(mrow == 3, g2,
                    jnp.where(mrow == 4, p1,
                    jnp.where(mrow == 5, p2, 0.0))))))


def _mixer(x, ang, win, cw, cb, lng, lnb, gkw, gkb, gng, wout, fng, rw, rb):
    B, S, D = x.shape
    n_s = S // TM
    const = lambda shape: pl.BlockSpec(shape, lambda b, s: (0,) * len(shape))
    tile = lambda w: pl.BlockSpec((None, TM, w), lambda b, s: (b, s, 0))
    return pl.pallas_call(
        _mixer_kernel,
        grid=(B, n_s),
        in_specs=[
            tile(D),
            const((1, D)), const((D, D_IN_PAD)), const((CONV_WIDTH, CONV_ROWS, D_CONV)), const((1, D_CONV)),
            const((1, D_CONV)), const((1, D_CONV)), const((LANES, D_GLA_K)), const((1, D_GLA_K)), const((1, HEAD_V)),
            const((D, D)), const((1, D)), const((LANES, D)), const((LANES, TM)),
        ],
        out_specs=[tile(D), tile(D),
                   pl.BlockSpec((SUBLANES, TM), lambda b, s: (0, b * n_s + s)),
                   pl.BlockSpec((None, R_ROWS, LANES), lambda b, s: (b * n_s + s, 0, 0))],
        out_shape=[
            jax.ShapeDtypeStruct((B, S, D), F32),
            jax.ShapeDtypeStruct((B, S, D), BF16),
            jax.ShapeDtypeStruct((SUBLANES, B * S), F32),
            jax.ShapeDtypeStruct((B * n_s, R_ROWS, LANES), F32),
        ],
        scratch_shapes=[
            pltpu.VMEM((TM + HALO, D_CONV), F32),
            pltpu.VMEM((SUBLANES - 1, SHIFT_ROWS, D_CONV), F32),
            pltpu.VMEM((TM, D_CONV), F32),
            pltpu.VMEM((D_GLA_V, D_GLA_K), F32),
        ],
        compiler_params=pltpu.CompilerParams(
            dimension_semantics=("arbitrary", "arbitrary"), vmem_limit_bytes=VMEM_LIMIT),
        name="mixer",
    )(x, ang, win, cw, cb, lng, lnb, gkw, gkb, gng, wout, fng, rw, rb)


PIECES = D_MODEL // LANES
RUN_BITS = 9
PAD_BITS = 8


def _to_row_tiles(ref, value):
    for c in range(PIECES):
        ref[pl.ds(c, value.shape[0], stride=PIECES), :] = value[:, c * LANES:(c + 1) * LANES]


def _from_row_tiles(ref, n_rows):
    return jnp.concatenate([ref[pl.ds(c, n_rows, stride=PIECES), :] for c in range(PIECES)], axis=1)


def _for_each_piece(n, bits, enabled, visit):
    for bit in range(bits - 1, -1, -1):
        done = (n >> (bit + 1)) << (bit + 1)

        @pl.when((((n >> bit) & 1) == 1) & enabled)
        def _():
            visit(done, 1 << bit)


def _for_each_run_piece(tile, enabled, n_ref, lo_ref, gs_ref, visit):
    for e in range(N_EXPERTS):
        lo = lo_ref[tile * N_EXPERTS + e]
        gs = gs_ref[tile * N_EXPERTS + e]
        _for_each_piece(n_ref[tile * N_EXPERTS + e], RUN_BITS, enabled,
                        lambda done, rows: visit(lo + done, gs + done, rows))


def _slab(ref, row, rows):
    return ref.at[pl.ds(pl.multiple_of(row * PIECES, PIECES), rows * PIECES)]


def _dispatch_kernel(n_ref, lo_ref, gs_ref, pad0_ref, padn_ref, used_ref, hf_ref, meta_ref, xs_ref,
                     srt, zeros, sems, zsem):
    i = pl.program_id(0)
    last = pl.num_programs(0) - 1
    slot = i % 2
    pos = meta_ref[4:4 + TOP_K, :]
    prow = lax.broadcasted_iota(jnp.int32, (TOP_K * TD, TD), 0).astype(F32)
    perm = jnp.where((prow == pos[0:1, :]) | (prow == pos[1:2, :]), 1.0, 0.0).astype(BF16)
    _to_row_tiles(srt.at[slot], _dot(perm, hf_ref[...]))
    _for_each_run_piece(
        i, True, n_ref, lo_ref, gs_ref,
        lambda lrow, grow, rows: pltpu.make_async_copy(
            _slab(srt.at[slot], lrow, rows), _slab(xs_ref, grow, rows), sems.at[slot]).start())

    def wait_tile(s):
        pltpu.make_async_copy(srt.at[s], xs_ref.at[pl.ds(0, TOP_K * TD * PIECES)], sems.at[s]).wait()

    @pl.when(i > 0)
    def _():
        wait_tile(1 - slot)

    @pl.when(i == last)
    def _():
        wait_tile(slot)
        zeros[...] = jnp.zeros_like(zeros)
        for e in range(N_EXPERTS):
            def pad_copy(done, rows, e=e):
                return pltpu.make_async_copy(_slab(zeros, 0, rows), _slab(xs_ref, pad0_ref[e] + done, rows), zsem)
            _for_each_piece(padn_ref[e], PAD_BITS, True, lambda done, rows: pad_copy(done, rows).start())
            _for_each_piece(padn_ref[e], PAD_BITS, True, lambda done, rows: pad_copy(done, rows).wait())

        def tail_block(j, carry):
            cp = pltpu.make_async_copy(zeros, _slab(xs_ref, j * MOE_BLOCK, MOE_BLOCK), zsem)
            cp.start()
            cp.wait()
            return carry

        lax.fori_loop(used_ref[0], xs_ref.shape[0] // (MOE_BLOCK * PIECES), tail_block, 0)


def _dispatch(tables, pads, hf, meta, cap):
    T, D = hf.shape
    n = T // TD
    grid_spec = pltpu.PrefetchScalarGridSpec(
        num_scalar_prefetch=6,
        grid=(n,),
        in_specs=[
            pl.BlockSpec((TD, D), lambda i, *_: (i, 0)),
            pl.BlockSpec((SUBLANES, TD), lambda i, *_: (0, i)),
        ],
        out_specs=pl.BlockSpec(memory_space=pl.ANY),
        scratch_shapes=[pltpu.VMEM((2, TOP_K * TD * PIECES, LANES), F32),
                        pltpu.VMEM((MOE_BLOCK * PIECES, LANES), F32),
                        pltpu.SemaphoreType.DMA((2,)), pltpu.SemaphoreType.DMA(())],
    )
    return pl.pallas_call(
        _dispatch_kernel,
        grid_spec=grid_spec,
        out_shape=jax.ShapeDtypeStruct((cap * PIECES, LANES), F32),
        compiler_params=pltpu.CompilerParams(dimension_semantics=("arbitrary",), vmem_limit_bytes=VMEM_LIMIT),
        name="dispatch",
    )(*tables, *pads, hf, meta)


def _expert_kernel(be_ref, used_ref, xs_ref, wg_ref, wu_ref, wd_ref, yb_ref, wg_b, wu_b, wd_b):
    i = pl.program_id(0)

    @pl.when((i == 0) | (be_ref[i] != be_ref[jnp.maximum(i - 1, 0)]))
    def _():
        wg_b[...] = wg_ref[...].astype(BF16)
        wu_b[...] = wu_ref[...].astype(BF16)
        wd_b[...] = wd_ref[...].astype(BF16)

    @pl.when(i < used_ref[0])
    def _():
        xb = _from_row_tiles(xs_ref, MOE_BLOCK).astype(BF16)
        g = _dot(xb, wg_b[...])
        u = _dot(xb, wu_b[...])
        h = (g * jax.nn.sigmoid(g) * u).astype(BF16)
        _to_row_tiles(yb_ref, _dot(h, wd_b[...]))

    @pl.when(i >= used_ref[0])
    def _():
        yb_ref[...] = jnp.zeros_like(yb_ref)


def _experts(block_expert, n_used, xs, wg, wu, wd, layer):
    D = D_MODEL
    n_blocks = xs.shape[0] // (MOE_BLOCK * PIECES)
    grid_spec = pltpu.PrefetchScalarGridSpec(
        num_scalar_prefetch=2,
        grid=(n_blocks,),
        in_specs=[
            pl.BlockSpec((MOE_BLOCK * PIECES, LANES), lambda i, be, used: (i, 0)),
            pl.BlockSpec((None, None, D, D_EXPERT), lambda i, be, used: (layer, be[i], 0, 0)),
            pl.BlockSpec((None, None, D, D_EXPERT), lambda i, be, used: (layer, be[i], 0, 0)),
            pl.BlockSpec((None, None, D_EXPERT, D), lambda i, be, used: (layer, be[i], 0, 0)),
        ],
        out_specs=pl.BlockSpec((MOE_BLOCK * PIECES, LANES), lambda i, be, used: (i, 0)),
        scratch_shapes=[pltpu.VMEM((D, D_EXPERT), BF16), pltpu.VMEM((D, D_EXPERT), BF16),
                        pltpu.VMEM((D_EXPERT, D), BF16)],
    )
    return pl.pallas_call(
        _expert_kernel,
        grid_spec=grid_spec,
        out_shape=jax.ShapeDtypeStruct(xs.shape, F32),
        compiler_params=pltpu.CompilerParams(
            dimension_semantics=("arbitrary",), vmem_limit_bytes=VMEM_LIMIT),
        name="experts",
    )(block_expert, n_used, xs, wg, wu, wd)


def _combine_kernel(n_ref, lo_ref, gs_ref, x_ref, meta_ref, fg_ref, yb_ref, o_ref, ys, sems, *, final_norm):
    i = pl.program_id(0)
    n_tiles = pl.num_programs(0)
    slot = i % 2

    def fetch(tile, s, enabled):
        _for_each_run_piece(
            tile, enabled, n_ref, lo_ref, gs_ref,
            lambda lrow, grow, rows: pltpu.make_async_copy(
                _slab(yb_ref, grow, rows), _slab(ys.at[s], lrow, rows), sems.at[s]).start())

    @pl.when(i == 0)
    def _():
        fetch(0, 0, True)

    nxt = jnp.minimum(i + 1, n_tiles - 1)
    fetch(nxt, 1 - slot, i + 1 < n_tiles)
    meta = jnp.transpose(jnp.concatenate([meta_ref[...], jnp.zeros((LANES - SUBLANES, TD), F32)], axis=0))
    pcol = lax.broadcasted_iota(jnp.int32, (TD, TOP_K * TD), 1).astype(F32)
    gather = (jnp.where(pcol == meta[:, 4:5], meta[:, 2:3], 0.0)
              + jnp.where(pcol == meta[:, 5:6], meta[:, 3:4], 0.0)).astype(BF16)
    pltpu.make_async_copy(yb_ref.at[pl.ds(0, TOP_K * TD * PIECES)], ys.at[slot], sems.at[slot]).wait()
    out = x_ref[...] + _dot(gather, _from_row_tiles(ys.at[slot], TOP_K * TD).astype(BF16))
    if final_norm:
        out = _rms(out, fg_ref[...])
    o_ref[...] = out


def _combine(tables, x, meta, fg, yb, final_norm):
    T, D = x.shape
    n = T // TD
    grid_spec = pltpu.PrefetchScalarGridSpec(
        num_scalar_prefetch=3,
        grid=(n,),
        in_specs=[
            pl.BlockSpec((TD, D), lambda i, *_: (i, 0)),
            pl.BlockSpec((SUBLANES, TD), lambda i, *_: (0, i)),
            pl.BlockSpec((1, D), lambda i, *_: (0, 0)),
            pl.BlockSpec(memory_space=pl.ANY),
        ],
        out_specs=pl.BlockSpec((TD, D), lambda i, *_: (i, 0)),
        scratch_shapes=[pltpu.VMEM((2, TOP_K * TD * PIECES, LANES), F32), pltpu.SemaphoreType.DMA((2,))],
    )
    return pl.pallas_call(
        functools.partial(_combine_kernel, final_norm=final_norm),
        grid_spec=grid_spec,
        out_shape=jax.ShapeDtypeStruct((T, D), F32),
        compiler_params=pltpu.CompilerParams(
            dimension_semantics=("arbitrary",), vmem_limit_bytes=VMEM_LIMIT),
        name="combine",
    )(*tables, x, meta, fg, yb)


def _routing_tables(cnt):
    n_tiles = cnt.shape[0]
    n = cnt[:, EXPERT_ROW0:EXPERT_ROW0 + N_EXPERTS, 0].astype(jnp.int32)
    counts = jnp.sum(n, axis=0)
    padded = (counts + MOE_BLOCK - 1) // MOE_BLOCK * MOE_BLOCK
    pend = jnp.cumsum(padded)
    pstart = pend - padded
    local = jnp.cumsum(n, axis=1) - n
    first = pstart[None, :] + jnp.cumsum(n, axis=0) - n
    n_blocks = (n_tiles * TD * TOP_K + MOE_BLOCK - 1) // MOE_BLOCK + N_EXPERTS
    block_row0 = jnp.arange(n_blocks, dtype=jnp.int32) * MOE_BLOCK
    block_expert = jnp.minimum(
        jnp.sum((pend[None, :] <= block_row0[:, None]).astype(jnp.int32), axis=1), N_EXPERTS - 1)
    pads = (pstart + counts, padded - counts, (pend[-1:] // MOE_BLOCK).astype(jnp.int32))
    return (n.reshape(-1), local.reshape(-1), first.reshape(-1)), pads, block_expert, n_blocks * MOE_BLOCK


def kernel(x, attn_norm_g, w_in, conv_w, conv_b, conv_ln_g, conv_ln_b, gk_w, gk_b, gla_norm_g, w_out, ffn_norm_g,
           router_group_w, router_group_b, router_expert_w, router_expert_b, expert_w_gate, expert_w_up,
           expert_w_down, final_norm_g):
    B, S, D = x.shape
    T = B * S
    depth = w_in.shape[0]
    for l in range(depth):
        win = jnp.pad(w_in[l], ((0, 0), (0, D_IN_PAD - w_in.shape[2]))).astype(BF16)
        cw = jnp.broadcast_to(conv_w[l][:, None, :], (CONV_WIDTH, CONV_ROWS, D_CONV))
        gkw = jnp.pad(gk_w[l], ((0, LANES - GATE_RANK), (0, 0))).astype(BF16)
        pad_g = EXPERT_ROW0 - N_GROUPS
        pad_e = LANES - EXPERT_ROW0 - N_EXPERTS
        rwt = jnp.concatenate([router_group_w[l].T, jnp.zeros((pad_g, D), F32), router_expert_w[l].T,
                               jnp.zeros((pad_e, D), F32)], axis=0).astype(BF16)
        rbt = jnp.concatenate([router_group_b[l], jnp.zeros((pad_g,), F32), router_expert_b[l].reshape(-1),
                               jnp.zeros((pad_e,), F32)])
        rbt = jnp.broadcast_to(rbt[:, None], (LANES, TM))
        xn, hf, meta, cnt = _mixer(
            x, attn_norm_g[l].reshape(1, D), win, cw, conv_b[l].reshape(1, -1), conv_ln_g[l].reshape(1, -1),
            conv_ln_b[l].reshape(1, -1), gkw, gk_b[l].reshape(1, -1), gla_norm_g[l].reshape(1, -1),
            w_out[l].astype(BF16), ffn_norm_g[l].reshape(1, D), rwt, rbt)
        tables, pads, block_expert, cap = _routing_tables(cnt)
        xs = _dispatch(tables, pads, hf.reshape(T, D), meta, cap)
        yb = _experts(block_expert, pads[2], xs, expert_w_gate, expert_w_up, expert_w_down, l)
        x = _combine(tables, xn.reshape(T, D), meta, final_norm_g.reshape(1, D), yb,
                     final_norm=(l == depth - 1)).reshape(B, S, D)
    return x
```

```python
import functools

import jax
import jax.numpy as jnp
from jax import lax
from jax.experimental import pallas as pl
from jax.experimental.pallas import tpu as pltpu

D_MODEL = 1024
D_CONV = 512
D_GLA_V = 512
GLA_HEADS = 4
D_GLA_K = 256
HEAD_K = 64
HEAD_V = 128
GATE_RANK = 16
GATE_NORMALIZER = 16.0
CHUNK = 64
CONV_WIDTH = 31
N_GROUPS = 4
EXPERTS_PER_GROUP = 8
N_EXPERTS = 32
TOP_K = 2
D_EXPERT = 512
MOE_BLOCK = 512
EPS = 1e-6

LANES = 128
SUBLANES = 8
D_IN_MAIN = 2 * D_CONV + 2 * D_GLA_K + 2 * D_GLA_V
D_IN_PAD = D_IN_MAIN + LANES
TM = 256
HALO = 32
CONV_ROWS = 32
SHIFT_ROWS = TM + HALO - SUBLANES
TD = 256
EXPERT_ROW0 = 8
R_ROWS = 48
VMEM_LIMIT = 48 * 1024 * 1024

F32 = jnp.float32
BF16 = jnp.bfloat16


def _dot(a, b):
    return jnp.dot(a, b, preferred_element_type=F32)


def _dot_nt(a, b):
    return lax.dot_general(a, b, (((1,), (1,)), ((), ())), preferred_element_type=F32)


def _dot_tn(a, b):
    return lax.dot_general(a, b, (((0,), (0,)), ((), ())), preferred_element_type=F32)


def _split_bf16(x):
    hi = x.astype(BF16)
    lo = (x - hi.astype(F32)).astype(BF16)
    return hi, lo


def _rms(x, g):
    return x * lax.rsqrt(jnp.mean(x * x, axis=-1, keepdims=True) + EPS) * g


def _mixer_kernel(x_ref, ang_ref, win_ref, cw_ref, cb_ref, lng_ref, lnb_ref, gkw_ref, gkb_ref, gng_ref,
                  wout_ref, fng_ref, rwt_ref, rbt_ref,
                  xo_ref, hf_ref, meta_ref, cnt_ref,
                  ubuf, sbuf, ybuf, st_ref, *, tiles_per_seq):
    @pl.when(pl.program_id(0) % tiles_per_seq == 0)
    def _():
        ubuf[0:HALO, :] = jnp.zeros((HALO, D_CONV), F32)
        st_ref[...] = jnp.zeros_like(st_ref)

    x = x_ref[...]
    proj = _dot(_rms(x, ang_ref[...]).astype(BF16), win_ref[...])

    def piece(o0, width):
        return proj[:, o0:o0 + width]
    o_q = 2 * D_CONV
    o_v = o_q + 2 * D_GLA_K
    conv_out = _conv_branch(piece(0, D_CONV), piece(D_CONV, D_CONV), cw_ref, cb_ref, lng_ref, lnb_ref,
                            ubuf, sbuf, ybuf)
    gla_out = _gla_branch(piece(o_q, D_GLA_K), piece(o_q + D_GLA_K, D_GLA_K), piece(o_v, D_GLA_V),
                          piece(o_v + D_GLA_V, D_GLA_V), piece(D_IN_MAIN, LANES), gkw_ref, gkb_ref, gng_ref, st_ref)
    xn = (x + _dot(conv_out.astype(BF16), wout_ref[0:D_CONV, :])
          + _dot(gla_out.astype(BF16), wout_ref[D_CONV:, :]))
    xo_ref[...] = xn
    hf = _rms(xn, fng_ref[...]).astype(BF16)
    hf_ref[...] = hf
    meta_ref[...], cnt_ref[...] = _router(hf, rwt_ref, rbt_ref)


def _conv_branch(ua, ug, cw_ref, cb_ref, lng_ref, lnb_ref, ubuf, sbuf, ybuf):
    ubuf[HALO:HALO + TM, :] = ua * jax.nn.sigmoid(ug)
    for r in range(1, SUBLANES):
        sbuf[r - 1] = ubuf[r:r + SHIFT_ROWS, :]
    for c in range(TM // CONV_ROWS):
        acc = jnp.broadcast_to(cb_ref[...], (CONV_ROWS, D_CONV))
        for j in range(CONV_WIDTH):
            off = HALO - (CONV_WIDTH - 1) + j + c * CONV_ROWS
            a0, r = off - off % SUBLANES, off % SUBLANES
            tap = ubuf[a0:a0 + CONV_ROWS, :] if r == 0 else sbuf[r - 1, a0:a0 + CONV_ROWS, :]
            acc = acc + cw_ref[j] * tap
        ybuf[c * CONV_ROWS:(c + 1) * CONV_ROWS, :] = acc
    ubuf[0:HALO, :] = ubuf[TM:TM + HALO, :]
    y = ybuf[...]
    mu = jnp.mean(y, axis=-1, keepdims=True)
    yc = y - mu
    var = jnp.mean(yc * yc, axis=-1, keepdims=True)
    yn = yc * lax.rsqrt(var + EPS) * lng_ref[...] + lnb_ref[...]
    return yn * jax.nn.sigmoid(yn)


def _gla_branch(q, k, v, go, gkl, gkw_ref, gkb_ref, gng_ref, st_ref):
    gk = jax.nn.log_sigmoid(_dot(gkl.astype(BF16), gkw_ref[...]) + gkb_ref[...]) / GATE_NORMALIZER
    row = lax.broadcasted_iota(jnp.int32, (TM, TM), 0)
    col = lax.broadcasted_iota(jnp.int32, (TM, TM), 1)
    same_chunk = (row // CHUNK) == (col // CHUNK)
    causal = same_chunk & (col <= row)
    gk_hi, gk_lo = _split_bf16(gk)
    l_cum = causal.astype(BF16)
    b = _dot(l_cum, gk_hi) + _dot(l_cum, gk_lo)
    b_last = [b[c * CHUNK + CHUNK - 1:(c + 1) * CHUNK, :] for c in range(TM // CHUNK)]
    bl = jnp.concatenate([jnp.broadcast_to(t, (CHUNK, D_GLA_K)) for t in b_last], axis=0)
    qt = (q * (HEAD_K ** -0.5) * jnp.exp(b)).astype(BF16)
    kt = (k * jnp.exp(-b)).astype(BF16)
    ks = (k * jnp.exp(bl - b)).astype(BF16)
    vb = v.astype(BF16)

    klane = lax.broadcasted_iota(jnp.int32, (1, D_GLA_K), 1) // HEAD_K
    o_parts = []
    for h in range(GLA_HEADS):
        qh = jnp.where(klane == h, qt, jnp.zeros_like(qt))
        a = jnp.where(causal, _dot_nt(qh, kt), 0.0).astype(BF16)
        o_parts.append(_dot(a, vb[:, h * HEAD_V:(h + 1) * HEAD_V]))
    o_intra = jnp.concatenate(o_parts, axis=-1)

    srow = lax.broadcasted_iota(jnp.int32, (D_GLA_V, D_GLA_K), 0) // HEAD_V
    scol = lax.broadcasted_iota(jnp.int32, (D_GLA_V, D_GLA_K), 1) // HEAD_K
    head_diag = srow == scol
    o_inter = []
    for c in range(TM // CHUNK):
        r0 = c * CHUNK
        st = st_ref[...]
        o_inter.append(_dot_nt(qt[r0:r0 + CHUNK], st.astype(BF16)))
        ut = _dot_tn(vb[r0:r0 + CHUNK], ks[r0:r0 + CHUNK])
        st_ref[...] = st * jnp.exp(b_last[c]) + jnp.where(head_diag, ut, 0.0)
    o = o_intra + jnp.concatenate(o_inter, axis=0)
    o_n = []
    for h in range(GLA_HEADS):
        oh = o[:, h * HEAD_V:(h + 1) * HEAD_V]
        o_n.append(_rms(oh, gng_ref[...]))
    return jnp.concatenate(o_n, axis=-1) * (go * jax.nn.sigmoid(go))


def _router(hf, rwt_ref, rbt_ref):
    lgt = _dot_nt(rwt_ref[...], hf)[0:R_ROWS] + rbt_ref[0:R_ROWS]
    rowf = lax.broadcasted_iota(jnp.int32, (R_ROWS, TM), 0).astype(F32)
    neg = jnp.float32(-1e30)
    big = jnp.float32(R_ROWS)
    is_g = rowf < N_GROUPS
    gl = jnp.where(is_g, lgt, neg)
    gm = jnp.max(gl, axis=0, keepdims=True)
    grp_p = 1.0 / jnp.sum(jnp.where(is_g, jnp.exp(gl - gm), 0.0), axis=0, keepdims=True)
    gidx = jnp.min(jnp.where(is_g & (gl == gm), rowf, big), axis=0, keepdims=True)
    lo = EXPERT_ROW0 + gidx * EXPERTS_PER_GROUP
    in_sel = (rowf >= lo) & (rowf < lo + EXPERTS_PER_GROUP)
    sl = jnp.where(in_sel, lgt, neg)
    sm = jnp.max(sl, axis=0, keepdims=True)
    sz = jnp.sum(jnp.where(in_sel, jnp.exp(sl - sm), 0.0), axis=0, keepdims=True)
    i1 = jnp.min(jnp.where(in_sel & (sl == sm), rowf, big), axis=0, keepdims=True)
    rest = in_sel & (rowf != i1)
    sl2 = jnp.where(rest, sl, neg)
    sm2 = jnp.max(sl2, axis=0, keepdims=True)
    i2 = jnp.min(jnp.where(rest & (sl2 == sm2), rowf, big), axis=0, keepdims=True)
    w1 = 1.0 / sz
    w2 = jnp.exp(sm2 - sm) / sz
    den = w1 + w2
    g1 = grp_p * (w1 / den)
    g2 = grp_p * (w2 / den)
    oh1 = rowf == i1
    oh2 = rowf == i2
    oh_any = jnp.where(oh1 | oh2, 1.0, 0.0).astype(BF16)
    trow = lax.broadcasted_iota(jnp.int32, (TM, TM), 0)
    tcol = lax.broadcasted_iota(jnp.int32, (TM, TM), 1)
    earlier = (trow < tcol).astype(BF16)
    ones_tt = jnp.ones((TM, TM), BF16)
    below = (jnp.where(i1 < rowf, 1.0, 0.0) + jnp.where(i2 < rowf, 1.0, 0.0)).astype(BF16)
    base = _dot(below, ones_tt) + _dot(oh_any, earlier)
    p1 = jnp.sum(jnp.where(oh1, base, 0.0), axis=0, keepdims=True)
    p2 = jnp.sum(jnp.where(oh2, base, 0.0), axis=0, keepdims=True)
    counts = _dot(oh_any, ones_tt)[:, 0:LANES]
    mrow = lax.broadcasted_iota(jnp.int32, (SUBLANES, TM), 0)
    meta = jnp.where(mrow == 0, i1 - EXPERT_ROW0,
           jnp.where(mrow == 1, i2 - EXPERT_ROW0,
           jnp.where(mrow == 2, g1,
           jnp.where(mrow == 3, g2,
           jnp.where(mrow == 4, p1,
           jnp.where(mrow == 5, p2, 0.0))))))
    return meta, counts


def _mixer(x, seq_len, ang, win, cw, cb, lng, lnb, gkw, gkb, gng, wout, fng, rw, rb):
    T, D = x.shape
    n = T // TM
    const = lambda shape: pl.BlockSpec(shape, lambda g: (0,) * len(shape))
    tile = lambda w: pl.BlockSpec((TM, w), lambda g: (g, 0))
    return pl.pallas_call(
        functools.partial(_mixer_kernel, tiles_per_seq=seq_len // TM),
        grid=(n,),
        in_specs=[
            tile(D),
            const((1, D)), const((D, D_IN_PAD)), const((CONV_WIDTH, CONV_ROWS, D_CONV)), const((1, D_CONV)),
            const((1, D_CONV)), const((1, D_CONV)), const((LANES, D_GLA_K)), const((1, D_GLA_K)), const((1, HEAD_V)),
            const((D, D)), const((1, D)), const((LANES, D)), const((LANES, TM)),
        ],
        out_specs=[tile(D), tile(D),
                   pl.BlockSpec((SUBLANES, TM), lambda g: (0, g)),
                   pl.BlockSpec((None, R_ROWS, LANES), lambda g: (g, 0, 0))],
        out_shape=[
            jax.ShapeDtypeStruct((T, D), F32),
            jax.ShapeDtypeStruct((T, D), BF16),
            jax.ShapeDtypeStruct((SUBLANES, T), F32),
            jax.ShapeDtypeStruct((n, R_ROWS, LANES), F32),
        ],
        scratch_shapes=[
            pltpu.VMEM((TM + HALO, D_CONV), F32),
            pltpu.VMEM((SUBLANES - 1, SHIFT_ROWS, D_CONV), F32),
            pltpu.VMEM((TM, D_CONV), F32),
            pltpu.VMEM((D_GLA_V, D_GLA_K), F32),
        ],
        compiler_params=pltpu.CompilerParams(
            dimension_semantics=("arbitrary",), vmem_limit_bytes=VMEM_LIMIT),
        name="mixer",
    )(x, ang, win, cw, cb, lng, lnb, gkw, gkb, gng, wout, fng, rw, rb)


PIECES = D_MODEL // LANES
RUN_BITS = 9
PAD_BITS = 9


def _to_row_tiles(ref, value):
    for c in range(PIECES):
        ref[pl.ds(c, value.shape[0], stride=PIECES), :] = value[:, c * LANES:(c + 1) * LANES]


def _from_row_tiles(ref, n_rows):
    return jnp.concatenate([ref[pl.ds(c, n_rows, stride=PIECES), :] for c in range(PIECES)], axis=1)


def _for_each_piece(n, bits, enabled, visit):
    for bit in range(bits - 1, -1, -1):
        done = (n >> (bit + 1)) << (bit + 1)

        @pl.when((((n >> bit) & 1) == 1) & enabled)
        def _():
            visit(done, 1 << bit)


def _for_each_run_piece(tile, enabled, n_ref, lo_ref, gs_ref, visit):
    for e in range(N_EXPERTS):
        lo = lo_ref[tile * N_EXPERTS + e]
        gs = gs_ref[tile * N_EXPERTS + e]
        _for_each_piece(n_ref[tile * N_EXPERTS + e], RUN_BITS, enabled,
                        lambda done, rows: visit(lo + done, gs + done, rows))


def _slab(ref, row, rows):
    return ref.at[pl.ds(pl.multiple_of(row * PIECES, PIECES), rows * PIECES)]


def _dispatch_kernel(n_ref, lo_ref, gs_ref, pad0_ref, padn_ref, used_ref, hf_ref, meta_ref, xs_ref,
                     srt, zeros, sems, zsem):
    i = pl.program_id(0)
    last = pl.num_programs(0) - 1
    slot = i % 2

    def send(tile, s, enabled):
        _for_each_run_piece(
            tile, enabled, n_ref, lo_ref, gs_ref,
            lambda lrow, grow, rows: pltpu.make_async_copy(
                _slab(srt.at[s], lrow, rows), _slab(xs_ref, grow, rows), sems.at[s]).start())

    def wait_tile(s):
        pltpu.make_async_copy(srt.at[s], xs_ref.at[pl.ds(0, TOP_K * TD * PIECES)], sems.at[s]).wait()

    @pl.when(i >= 2)
    def _():
        wait_tile(slot)

    send(jnp.maximum(i - 1, 0), 1 - slot, i >= 1)
    pos = meta_ref[4:4 + TOP_K, :]
    prow = lax.broadcasted_iota(jnp.int32, (TOP_K * TD, TD), 0).astype(F32)
    perm = jnp.where((prow == pos[0:1, :]) | (prow == pos[1:2, :]), 1.0, 0.0).astype(BF16)
    _to_row_tiles(srt.at[slot], _dot(perm, hf_ref[...]))

    @pl.when(i == last)
    def _():
        send(i, slot, True)

        @pl.when(i >= 1)
        def _():
            wait_tile(1 - slot)
        wait_tile(slot)
        zeros[...] = jnp.zeros_like(zeros)
        for e in range(N_EXPERTS):
            def pad_copy(done, rows, e=e):
                return pltpu.make_async_copy(_slab(zeros, 0, rows), _slab(xs_ref, pad0_ref[e] + done, rows), zsem)
            _for_each_piece(padn_ref[e], PAD_BITS, True, lambda done, rows: pad_copy(done, rows).start())
            _for_each_piece(padn_ref[e], PAD_BITS, True, lambda done, rows: pad_copy(done, rows).wait())

        def tail_block(j, carry):
            cp = pltpu.make_async_copy(zeros, _slab(xs_ref, j * MOE_BLOCK, MOE_BLOCK), zsem)
            cp.start()
            cp.wait()
            return carry

        lax.fori_loop(used_ref[0], xs_ref.shape[0] // (MOE_BLOCK * PIECES), tail_block, 0)


def _dispatch(tables, pads, hf, meta, cap):
    T, D = hf.shape
    n = T // TD
    grid_spec = pltpu.PrefetchScalarGridSpec(
        num_scalar_prefetch=6,
        grid=(n,),
        in_specs=[
            pl.BlockSpec((TD, D), lambda i, *_: (i, 0)),
            pl.BlockSpec((SUBLANES, TD), lambda i, *_: (0, i)),
        ],
        out_specs=pl.BlockSpec(memory_space=pl.ANY),
        scratch_shapes=[pltpu.VMEM((2, TOP_K * TD * PIECES, LANES), F32),
                        pltpu.VMEM((MOE_BLOCK * PIECES, LANES), F32),
                        pltpu.SemaphoreType.DMA((2,)), pltpu.SemaphoreType.DMA(())],
    )
    return pl.pallas_call(
        _dispatch_kernel,
        grid_spec=grid_spec,
        out_shape=jax.ShapeDtypeStruct((cap * PIECES, LANES), F32),
        compiler_params=pltpu.CompilerParams(dimension_semantics=("arbitrary",), vmem_limit_bytes=VMEM_LIMIT),
        name="dispatch",
    )(*tables, *pads, hf, meta)


def _expert_kernel(be_ref, used_ref, xs_ref, wg_ref, wu_ref, wd_ref, yb_ref, wg_b, wu_b, wd_b):
    i = pl.program_id(0)

    @pl.when((i == 0) | (be_ref[i] != be_ref[jnp.maximum(i - 1, 0)]))
    def _():
        wg_b[...] = wg_ref[...].astype(BF16)
        wu_b[...] = wu_ref[...].astype(BF16)
        wd_b[...] = wd_ref[...].astype(BF16)

    @pl.when(i < used_ref[0])
    def _():
        xb = _from_row_tiles(xs_ref, MOE_BLOCK).astype(BF16)
        g = _dot(xb, wg_b[...])
        u = _dot(xb, wu_b[...])
        h = (g * jax.nn.sigmoid(g) * u).astype(BF16)
        _to_row_tiles(yb_ref, _dot(h, wd_b[...]))

    @pl.when(i >= used_ref[0])
    def _():
        yb_ref[...] = jnp.zeros_like(yb_ref)


def _experts(block_expert, n_used, xs, wg, wu, wd, layer):
    D = D_MODEL
    n_blocks = xs.shape[0] // (MOE_BLOCK * PIECES)
    grid_spec = pltpu.PrefetchScalarGridSpec(
        num_scalar_prefetch=2,
        grid=(n_blocks,),
        in_specs=[
            pl.BlockSpec((MOE_BLOCK * PIECES, LANES), lambda i, be, used: (i, 0)),
            pl.BlockSpec((None, None, D, D_EXPERT), lambda i, be, used: (layer, be[i], 0, 0)),
            pl.BlockSpec((None, None, D, D_EXPERT), lambda i, be, used: (layer, be[i], 0, 0)),
            pl.BlockSpec((None, None, D_EXPERT, D), lambda i, be, used: (layer, be[i], 0, 0)),
        ],
        out_specs=pl.BlockSpec((MOE_BLOCK * PIECES, LANES), lambda i, be, used: (i, 0)),
        scratch_shapes=[pltpu.VMEM((D, D_EXPERT), BF16), pltpu.VMEM((D, D_EXPERT), BF16),
                        pltpu.VMEM((D_EXPERT, D), BF16)],
    )
    return pl.pallas_call(
        _expert_kernel,
        grid_spec=grid_spec,
        out_shape=jax.ShapeDtypeStruct(xs.shape, F32),
        compiler_params=pltpu.CompilerParams(
            dimension_semantics=("arbitrary",), vmem_limit_bytes=VMEM_LIMIT),
        name="experts",
    )(block_expert, n_used, xs, wg, wu, wd)


def _combine_kernel(n_ref, lo_ref, gs_ref, x_ref, meta_ref, fg_ref, yb_ref, o_ref, ys, sems, *, final_norm):
    i = pl.program_id(0)
    n_tiles = pl.num_programs(0)
    slot = i % 2

    def fetch(tile, s, enabled):
        _for_each_run_piece(
            tile, enabled, n_ref, lo_ref, gs_ref,
            lambda lrow, grow, rows: pltpu.make_async_copy(
                _slab(yb_ref, grow, rows), _slab(ys.at[s], lrow, rows), sems.at[s]).start())

    @pl.when(i == 0)
    def _():
        fetch(0, 0, True)

    nxt = jnp.minimum(i + 1, n_tiles - 1)
    fetch(nxt, 1 - slot, i + 1 < n_tiles)
    meta = jnp.transpose(jnp.concatenate([meta_ref[...], jnp.zeros((LANES - SUBLANES, TD), F32)], axis=0))
    pcol = lax.broadcasted_iota(jnp.int32, (TD, TOP_K * TD), 1).astype(F32)
    gather = (jnp.where(pcol == meta[:, 4:5], meta[:, 2:3], 0.0)
              + jnp.where(pcol == meta[:, 5:6], meta[:, 3:4], 0.0)).astype(BF16)
    pltpu.make_async_copy(yb_ref.at[pl.ds(0, TOP_K * TD * PIECES)], ys.at[slot], sems.at[slot]).wait()
    out = x_ref[...] + _dot(gather, _from_row_tiles(ys.at[slot], TOP_K * TD).astype(BF16))
    if final_norm:
        out = _rms(out, fg_ref[...])
    o_ref[...] = out


def _combine(tables, x, meta, fg, yb, final_norm):
    T, D = x.shape
    n = T // TD
    grid_spec = pltpu.PrefetchScalarGridSpec(
        num_scalar_prefetch=3,
        grid=(n,),
        in_specs=[
            pl.BlockSpec((TD, D), lambda i, *_: (i, 0)),
            pl.BlockSpec((SUBLANES, TD), lambda i, *_: (0, i)),
            pl.BlockSpec((1, D), lambda i, *_: (0, 0)),
            pl.BlockSpec(memory_space=pl.ANY),
        ],
        out_specs=pl.BlockSpec((TD, D), lambda i, *_: (i, 0)),
        scratch_shapes=[pltpu.VMEM((2, TOP_K * TD * PIECES, LANES), F32), pltpu.SemaphoreType.DMA((2,))],
    )
    return pl.pallas_call(
        functools.partial(_combine_kernel, final_norm=final_norm),
        grid_spec=grid_spec,
        out_shape=jax.ShapeDtypeStruct((T, D), F32),
        compiler_params=pltpu.CompilerParams(
            dimension_semantics=("arbitrary",), vmem_limit_bytes=VMEM_LIMIT),
        name="combine",
    )(*tables, x, meta, fg, yb)


def _routing_tables(cnt):
    n_tiles = cnt.shape[0]
    n = cnt[:, EXPERT_ROW0:EXPERT_ROW0 + N_EXPERTS, 0].astype(jnp.int32)
    counts = jnp.sum(n, axis=0)
    padded = (counts + MOE_BLOCK - 1) // MOE_BLOCK * MOE_BLOCK
    pend = jnp.cumsum(padded)
    pstart = pend - padded
    local = jnp.cumsum(n, axis=1) - n
    first = pstart[None, :] + jnp.cumsum(n, axis=0) - n
    n_blocks = (n_tiles * TD * TOP_K + MOE_BLOCK - 1) // MOE_BLOCK + N_EXPERTS
    block_row0 = jnp.arange(n_blocks, dtype=jnp.int32) * MOE_BLOCK
    block_expert = jnp.minimum(
        jnp.sum((pend[None, :] <= block_row0[:, None]).astype(jnp.int32), axis=1), N_EXPERTS - 1)
    pads = (pstart + counts, padded - counts, (pend[-1:] // MOE_BLOCK).astype(jnp.int32))
    return (n.reshape(-1), local.reshape(-1), first.reshape(-1)), pads, block_expert, n_blocks * MOE_BLOCK


def kernel(x, attn_norm_g, w_in, conv_w, conv_b, conv_ln_g, conv_ln_b, gk_w, gk_b, gla_norm_g, w_out, ffn_norm_g,
           router_group_w, router_group_b, router_expert_w, router_expert_b, expert_w_gate, expert_w_up,
           expert_w_down, final_norm_g):
    B, S, D = x.shape
    T = B * S
    depth = w_in.shape[0]
    x = x.reshape(T, D)
    for l in range(depth):
        win = jnp.pad(w_in[l], ((0, 0), (0, D_IN_PAD - w_in.shape[2]))).astype(BF16)
        cw = jnp.broadcast_to(conv_w[l][:, None, :], (CONV_WIDTH, CONV_ROWS, D_CONV))
        gkw = jnp.pad(gk_w[l], ((0, LANES - GATE_RANK), (0, 0))).astype(BF16)
        pad_g = EXPERT_ROW0 - N_GROUPS
        pad_e = LANES - EXPERT_ROW0 - N_EXPERTS
        rwt = jnp.concatenate([router_group_w[l].T, jnp.zeros((pad_g, D), F32), router_expert_w[l].T,
                               jnp.zeros((pad_e, D), F32)], axis=0).astype(BF16)
        rbt = jnp.concatenate([router_group_b[l], jnp.zeros((pad_g,), F32), router_expert_b[l].reshape(-1),
                               jnp.zeros((pad_e,), F32)])
        rbt = jnp.broadcast_to(rbt[:, None], (LANES, TM))
        xn, hf, meta, cnt = _mixer(
            x, S, attn_norm_g[l].reshape(1, D), win, cw, conv_b[l].reshape(1, -1), conv_ln_g[l].reshape(1, -1),
            conv_ln_b[l].reshape(1, -1), gkw, gk_b[l].reshape(1, -1), gla_norm_g[l].reshape(1, -1),
            w_out[l].astype(BF16), ffn_norm_g[l].reshape(1, D), rwt, rbt)
        tables, pads, block_expert, cap = _routing_tables(cnt)
        xs = _dispatch(tables, pads, hf, meta, cap)
        yb = _experts(block_expert, pads[2], xs, expert_w_gate, expert_w_up, expert_w_down, l)
        x = _combine(tables, xn, meta, final_norm_g.reshape(1, D), yb, final_norm=(l == depth - 1))
    return x.reshape(B, S, D)
```

```python
import functools

import jax
import jax.numpy as jnp
from jax import lax
from jax.experimental import pallas as pl
from jax.experimental.pallas import tpu as pltpu

D_MODEL = 1024
D_CONV = 512
D_GLA_V = 512
GLA_HEADS = 4
D_GLA_K = 256
HEAD_K = 64
HEAD_V = 128
GATE_RANK = 16
GATE_NORMALIZER = 16.0
CHUNK = 64
CONV_WIDTH = 31
N_GROUPS = 4
EXPERTS_PER_GROUP = 8
N_EXPERTS = 32
TOP_K = 2
D_EXPERT = 512
MOE_BLOCK = 512
EPS = 1e-6

LANES = 128
SUBLANES = 8
D_IN_MAIN = 2 * D_CONV + 2 * D_GLA_K + 2 * D_GLA_V
D_IN_PAD = D_IN_MAIN + LANES
TM = 256
HALO = 32
CONV_ROWS = 32
SHIFT_ROWS = TM + HALO - SUBLANES
TD = 256
EXPERT_ROW0 = 8
R_ROWS = 48
VMEM_LIMIT = 48 * 1024 * 1024

F32 = jnp.float32
BF16 = jnp.bfloat16


def _dot(a, b):
    return jnp.dot(a, b, preferred_element_type=F32)


def _dot_nt(a, b):
    return lax.dot_general(a, b, (((1,), (1,)), ((), ())), preferred_element_type=F32)


def _dot_tn(a, b):
    return lax.dot_general(a, b, (((0,), (0,)), ((), ())), preferred_element_type=F32)


def _split_bf16(x):
    hi = x.astype(BF16)
    lo = (x - hi.astype(F32)).astype(BF16)
    return hi, lo


def _rms(x, g):
    return x * lax.rsqrt(jnp.mean(x * x, axis=-1, keepdims=True) + EPS) * g


def _mixer_kernel(x_ref, ang_ref, win_ref, cw_ref, cb_ref, lng_ref, lnb_ref, gkw_ref, gkb_ref, gng_ref,
                  wout_ref, fng_ref, rwt_ref, rbt_ref,
                  xo_ref, hf_ref, meta_ref, cnt_ref,
                  ubuf, sbuf, ybuf, st_ref, *, tiles_per_seq):
    @pl.when(pl.program_id(0) % tiles_per_seq == 0)
    def _():
        ubuf[0:HALO, :] = jnp.zeros((HALO, D_CONV), F32)
        st_ref[...] = jnp.zeros_like(st_ref)

    x = x_ref[...]
    proj = _dot(_rms(x, ang_ref[...]).astype(BF16), win_ref[...])

    def piece(o0, width):
        return proj[:, o0:o0 + width]
    o_q = 2 * D_CONV
    o_v = o_q + 2 * D_GLA_K
    conv_out = _conv_branch(piece(0, D_CONV), piece(D_CONV, D_CONV), cw_ref, cb_ref, lng_ref, lnb_ref,
                            ubuf, sbuf, ybuf)
    gla_out = _gla_branch(piece(o_q, D_GLA_K), piece(o_q + D_GLA_K, D_GLA_K), piece(o_v, D_GLA_V),
                          piece(o_v + D_GLA_V, D_GLA_V), piece(D_IN_MAIN, LANES), gkw_ref, gkb_ref, gng_ref, st_ref)
    xn = (x + _dot(conv_out.astype(BF16), wout_ref[0:D_CONV, :])
          + _dot(gla_out.astype(BF16), wout_ref[D_CONV:, :]))
    xo_ref[...] = xn
    hf = _rms(xn, fng_ref[...]).astype(BF16)
    hf_ref[...] = hf
    meta_ref[...], cnt_ref[...] = _router(hf, rwt_ref, rbt_ref)


def _conv_branch(ua, ug, cw_ref, cb_ref, lng_ref, lnb_ref, ubuf, sbuf, ybuf):
    ubuf[HALO:HALO + TM, :] = ua * jax.nn.sigmoid(ug)
    for r in range(1, SUBLANES):
        sbuf[r - 1] = ubuf[r:r + SHIFT_ROWS, :]
    for c in range(TM // CONV_ROWS):
        acc = jnp.broadcast_to(cb_ref[...], (CONV_ROWS, D_CONV))
        for j in range(CONV_WIDTH):
            off = HALO - (CONV_WIDTH - 1) + j + c * CONV_ROWS
            a0, r = off - off % SUBLANES, off % SUBLANES
            tap = ubuf[a0:a0 + CONV_ROWS, :] if r == 0 else sbuf[r - 1, a0:a0 + CONV_ROWS, :]
            acc = acc + cw_ref[j] * tap
        ybuf[c * CONV_ROWS:(c + 1) * CONV_ROWS, :] = acc
    ubuf[0:HALO, :] = ubuf[TM:TM + HALO, :]
    y = ybuf[...]
    mu = jnp.mean(y, axis=-1, keepdims=True)
    yc = y - mu
    var = jnp.mean(yc * yc, axis=-1, keepdims=True)
    yn = yc * lax.rsqrt(var + EPS) * lng_ref[...] + lnb_ref[...]
    return yn * jax.nn.sigmoid(yn)


def _gla_branch(q, k, v, go, gkl, gkw_ref, gkb_ref, gng_ref, st_ref):
    gk = jax.nn.log_sigmoid(_dot(gkl.astype(BF16), gkw_ref[...]) + gkb_ref[...]) / GATE_NORMALIZER
    row = lax.broadcasted_iota(jnp.int32, (TM, TM), 0)
    col = lax.broadcasted_iota(jnp.int32, (TM, TM), 1)
    same_chunk = (row // CHUNK) == (col // CHUNK)
    causal = same_chunk & (col <= row)
    gk_hi, gk_lo = _split_bf16(gk)
    l_cum = causal.astype(BF16)
    b = _dot(l_cum, gk_hi) + _dot(l_cum, gk_lo)
    b_last = [b[c * CHUNK + CHUNK - 1:(c + 1) * CHUNK, :] for c in range(TM // CHUNK)]
    bl = jnp.concatenate([jnp.broadcast_to(t, (CHUNK, D_GLA_K)) for t in b_last], axis=0)
    qt = (q * (HEAD_K ** -0.5) * jnp.exp(b)).astype(BF16)
    kt = (k * jnp.exp(-b)).astype(BF16)
    ks = (k * jnp.exp(bl - b)).astype(BF16)
    vb = v.astype(BF16)

    klane = lax.broadcasted_iota(jnp.int32, (1, D_GLA_K), 1) // HEAD_K
    o_parts = []
    for h in range(GLA_HEADS):
        qh = jnp.where(klane == h, qt, jnp.zeros_like(qt))
        a = jnp.where(causal, _dot_nt(qh, kt), 0.0).astype(BF16)
        o_parts.append(_dot(a, vb[:, h * HEAD_V:(h + 1) * HEAD_V]))
    o_intra = jnp.concatenate(o_parts, axis=-1)

    srow = lax.broadcasted_iota(jnp.int32, (D_GLA_V, D_GLA_K), 0) // HEAD_V
    scol = lax.broadcasted_iota(jnp.int32, (D_GLA_V, D_GLA_K), 1) // HEAD_K
    head_diag = srow == scol
    o_inter = []
    for c in range(TM // CHUNK):
        r0 = c * CHUNK
        st = st_ref[...]
        o_inter.append(_dot_nt(qt[r0:r0 + CHUNK], st.astype(BF16)))
        ut = _dot_tn(vb[r0:r0 + CHUNK], ks[r0:r0 + CHUNK])
        st_ref[...] = st * jnp.exp(b_last[c]) + jnp.where(head_diag, ut, 0.0)
    o = o_intra + jnp.concatenate(o_inter, axis=0)
    o_n = []
    for h in range(GLA_HEADS):
        oh = o[:, h * HEAD_V:(h + 1) * HEAD_V]
        o_n.append(_rms(oh, gng_ref[...]))
    return jnp.concatenate(o_n, axis=-1) * (go * jax.nn.sigmoid(go))


def _router(hf, rwt_ref, rbt_ref):
    lgt = _dot_nt(rwt_ref[...], hf)[0:R_ROWS] + rbt_ref[0:R_ROWS]
    rowf = lax.broadcasted_iota(jnp.int32, (R_ROWS, TM), 0).astype(F32)
    neg = jnp.float32(-1e30)
    big = jnp.float32(R_ROWS)
    is_g = rowf < N_GROUPS
    gl = jnp.where(is_g, lgt, neg)
    gm = jnp.max(gl, axis=0, keepdims=True)
    grp_p = 1.0 / jnp.sum(jnp.where(is_g, jnp.exp(gl - gm), 0.0), axis=0, keepdims=True)
    gidx = jnp.min(jnp.where(is_g & (gl == gm), rowf, big), axis=0, keepdims=True)
    lo = EXPERT_ROW0 + gidx * EXPERTS_PER_GROUP
    in_sel = (rowf >= lo) & (rowf < lo + EXPERTS_PER_GROUP)
    sl = jnp.where(in_sel, lgt, neg)
    sm = jnp.max(sl, axis=0, keepdims=True)
    sz = jnp.sum(jnp.where(in_sel, jnp.exp(sl - sm), 0.0), axis=0, keepdims=True)
    i1 = jnp.min(jnp.where(in_sel & (sl == sm), rowf, big), axis=0, keepdims=True)
    rest = in_sel & (rowf != i1)
    sl2 = jnp.where(rest, sl, neg)
    sm2 = jnp.max(sl2, axis=0, keepdims=True)
    i2 = jnp.min(jnp.where(rest & (sl2 == sm2), rowf, big), axis=0, keepdims=True)
    w1 = 1.0 / sz
    w2 = jnp.exp(sm2 - sm) / sz
    den = w1 + w2
    g1 = grp_p * (w1 / den)
    g2 = grp_p * (w2 / den)
    oh1 = rowf == i1
    oh2 = rowf == i2
    oh_any = jnp.where(oh1 | oh2, 1.0, 0.0).astype(BF16)
    trow = lax.broadcasted_iota(jnp.int32, (TM, TM), 0)
    tcol = lax.broadcasted_iota(jnp.int32, (TM, TM), 1)
    earlier = (trow < tcol).astype(BF16)
    ones_tt = jnp.ones((TM, TM), BF16)
    below = (jnp.where(i1 < rowf, 1.0, 0.0) + jnp.where(i2 < rowf, 1.0, 0.0)).astype(BF16)
    base = _dot(below, ones_tt) + _dot(oh_any, earlier)
    p1 = jnp.sum(jnp.where(oh1, base, 0.0), axis=0, keepdims=True)
    p2 = jnp.sum(jnp.where(oh2, base, 0.0), axis=0, keepdims=True)
    counts = _dot(oh_any, ones_tt)[:, 0:LANES]
    mrow = lax.broadcasted_iota(jnp.int32, (SUBLANES, TM), 0)
    meta = jnp.where(mrow == 0, i1 - EXPERT_ROW0,
           jnp.where(mrow == 1, i2 - EXPERT_ROW0,
           jnp.where(mrow == 2, g1,
           jnp.where(mrow == 3, g2,
           jnp.where(mrow == 4, p1,
           jnp.where(mrow == 5, p2, 0.0))))))
    return meta, counts


def _mixer(x, seq_len, ang, win, cw, cb, lng, lnb, gkw, gkb, gng, wout, fng, rw, rb):
    T, D = x.shape
    n = T // TM
    const = lambda shape: pl.BlockSpec(shape, lambda g: (0,) * len(shape))
    tile = lambda w: pl.BlockSpec((TM, w), lambda g: (g, 0))
    return pl.pallas_call(
        functools.partial(_mixer_kernel, tiles_per_seq=seq_len // TM),
        grid=(n,),
        in_specs=[
            tile(D),
            const((1, D)), const((D, D_IN_PAD)), const((CONV_WIDTH, CONV_ROWS, D_CONV)), const((1, D_CONV)),
            const((1, D_CONV)), const((1, D_CONV)), const((LANES, D_GLA_K)), const((1, D_GLA_K)), const((1, HEAD_V)),
            const((D, D)), const((1, D)), const((LANES, D)), const((LANES, TM)),
        ],
        out_specs=[tile(D), tile(D),
                   pl.BlockSpec((SUBLANES, TM), lambda g: (0, g)),
                   pl.BlockSpec((None, R_ROWS, LANES), lambda g: (g, 0, 0))],
        out_shape=[
            jax.ShapeDtypeStruct((T, D), F32),
            jax.ShapeDtypeStruct((T, D), BF16),
            jax.ShapeDtypeStruct((SUBLANES, T), F32),
            jax.ShapeDtypeStruct((n, R_ROWS, LANES), F32),
        ],
        scratch_shapes=[
            pltpu.VMEM((TM + HALO, D_CONV), F32),
            pltpu.VMEM((SUBLANES - 1, SHIFT_ROWS, D_CONV), F32),
            pltpu.VMEM((TM, D_CONV), F32),
            pltpu.VMEM((D_GLA_V, D_GLA_K), F32),
        ],
        compiler_params=pltpu.CompilerParams(
            dimension_semantics=("arbitrary",), vmem_limit_bytes=VMEM_LIMIT),
        name="mixer",
    )(x, ang, win, cw, cb, lng, lnb, gkw, gkb, gng, wout, fng, rw, rb)


PIECES = D_MODEL // LANES


def _to_row_tiles(ref, value):
    for c in range(PIECES):
        ref[pl.ds(c, value.shape[0], stride=PIECES), :] = value[:, c * LANES:(c + 1) * LANES]


def _from_row_tiles(ref, n_rows):
    return jnp.concatenate([ref[pl.ds(c, n_rows, stride=PIECES), :] for c in range(PIECES)], axis=1)


def _for_each_run(tile, enabled, n_ref, lo_ref, gs_ref, visit):
    for e in range(N_EXPERTS):
        lo = lo_ref[tile * N_EXPERTS + e]
        gs = gs_ref[tile * N_EXPERTS + e]
        n = n_ref[tile * N_EXPERTS + e]

        @pl.when((n > 0) & enabled)
        def _():
            visit(lo, gs, n)


def _slab(ref, row, rows):
    return ref.at[pl.ds(pl.multiple_of(row * PIECES, PIECES), rows * PIECES)]


def _dispatch_kernel(n_ref, lo_ref, gs_ref, pad0_ref, padn_ref, used_ref, hf_ref, meta_ref, xs_ref,
                     srt, zeros, sems, zsem):
    i = pl.program_id(0)
    last = pl.num_programs(0) - 1
    slot = i % 2
    pos = meta_ref[4:4 + TOP_K, :]
    prow = lax.broadcasted_iota(jnp.int32, (TOP_K * TD, TD), 0).astype(F32)
    perm = jnp.where((prow == pos[0:1, :]) | (prow == pos[1:2, :]), 1.0, 0.0).astype(BF16)
    _to_row_tiles(srt.at[slot], _dot(perm, hf_ref[...]))
    _for_each_run(
        i, True, n_ref, lo_ref, gs_ref,
        lambda lrow, grow, rows: pltpu.make_async_copy(
            _slab(srt.at[slot], lrow, rows), _slab(xs_ref, grow, rows), sems.at[slot]).start())

    def wait_tile(s):
        pltpu.make_async_copy(srt.at[s], xs_ref.at[pl.ds(0, TOP_K * TD * PIECES)], sems.at[s]).wait()

    @pl.when(i > 0)
    def _():
        wait_tile(1 - slot)

    @pl.when(i == last)
    def _():
        wait_tile(slot)
        zeros[...] = jnp.zeros_like(zeros)
        for e in range(N_EXPERTS):
            @pl.when(padn_ref[e] > 0)
            def _():
                cp = pltpu.make_async_copy(_slab(zeros, 0, padn_ref[e]), _slab(xs_ref, pad0_ref[e], padn_ref[e]), zsem)
                cp.start()
                cp.wait()

        def tail_block(j, carry):
            cp = pltpu.make_async_copy(zeros, _slab(xs_ref, j * MOE_BLOCK, MOE_BLOCK), zsem)
            cp.start()
            cp.wait()
            return carry

        lax.fori_loop(used_ref[0], xs_ref.shape[0] // (MOE_BLOCK * PIECES), tail_block, 0)


def _dispatch(tables, pads, hf, meta, cap):
    T, D = hf.shape
    n = T // TD
    grid_spec = pltpu.PrefetchScalarGridSpec(
        num_scalar_prefetch=6,
        grid=(n,),
        in_specs=[
            pl.BlockSpec((TD, D), lambda i, *_: (i, 0)),
            pl.BlockSpec((SUBLANES, TD), lambda i, *_: (0, i)),
        ],
        out_specs=pl.BlockSpec(memory_space=pl.ANY),
        scratch_shapes=[pltpu.VMEM((2, TOP_K * TD * PIECES, LANES), F32),
                        pltpu.VMEM((MOE_BLOCK * PIECES, LANES), F32),
                        pltpu.SemaphoreType.DMA((2,)), pltpu.SemaphoreType.DMA(())],
    )
    return pl.pallas_call(
        _dispatch_kernel,
        grid_spec=grid_spec,
        out_shape=jax.ShapeDtypeStruct((cap * PIECES, LANES), F32),
        compiler_params=pltpu.CompilerParams(dimension_semantics=("arbitrary",), vmem_limit_bytes=VMEM_LIMIT),
        name="dispatch",
    )(*tables, *pads, hf, meta)


def _expert_kernel(be_ref, used_ref, xs_ref, wg_ref, wu_ref, wd_ref, yb_ref, wg_b, wu_b, wd_b):
    i = pl.program_id(0)

    @pl.when((i == 0) | (be_ref[i] != be_ref[jnp.maximum(i - 1, 0)]))
    def _():
        wg_b[...] = wg_ref[...].astype(BF16)
        wu_b[...] = wu_ref[...].astype(BF16)
        wd_b[...] = wd_ref[...].astype(BF16)

    @pl.when(i < used_ref[0])
    def _():
        xb = _from_row_tiles(xs_ref, MOE_BLOCK).astype(BF16)
        g = _dot(xb, wg_b[...])
        u = _dot(xb, wu_b[...])
        h = (g * jax.nn.sigmoid(g) * u).astype(BF16)
        _to_row_tiles(yb_ref, _dot(h, wd_b[...]))

    @pl.when(i >= used_ref[0])
    def _():
        yb_ref[...] = jnp.zeros_like(yb_ref)


def _experts(block_expert, n_used, xs, wg, wu, wd, layer):
    D = D_MODEL
    n_blocks = xs.shape[0] // (MOE_BLOCK * PIECES)
    grid_spec = pltpu.PrefetchScalarGridSpec(
        num_scalar_prefetch=2,
        grid=(n_blocks,),
        in_specs=[
            pl.BlockSpec((MOE_BLOCK * PIECES, LANES), lambda i, be, used: (i, 0)),
            pl.BlockSpec((None, None, D, D_EXPERT), lambda i, be, used: (layer, be[i], 0, 0)),
            pl.BlockSpec((None, None, D, D_EXPERT), lambda i, be, used: (layer, be[i], 0, 0)),
            pl.BlockSpec((None, None, D_EXPERT, D), lambda i, be, used: (layer, be[i], 0, 0)),
        ],
        out_specs=pl.BlockSpec((MOE_BLOCK * PIECES, LANES), lambda i, be, used: (i, 0)),
        scratch_shapes=[pltpu.VMEM((D, D_EXPERT), BF16), pltpu.VMEM((D, D_EXPERT), BF16),
                        pltpu.VMEM((D_EXPERT, D), BF16)],
    )
    return pl.pallas_call(
        _expert_kernel,
        grid_spec=grid_spec,
        out_shape=jax.ShapeDtypeStruct(xs.shape, F32),
        compiler_params=pltpu.CompilerParams(
            dimension_semantics=("arbitrary",), vmem_limit_bytes=VMEM_LIMIT),
        name="experts",
    )(block_expert, n_used, xs, wg, wu, wd)


def _combine_kernel(n_ref, lo_ref, gs_ref, x_ref, meta_ref, fg_ref, yb_ref, o_ref, ys, sems, *, final_norm):
    i = pl.program_id(0)
    n_tiles = pl.num_programs(0)
    slot = i % 2

    def fetch(tile, s, enabled):
        _for_each_run(
            tile, enabled, n_ref, lo_ref, gs_ref,
            lambda lrow, grow, rows: pltpu.make_async_copy(
                _slab(yb_ref, grow, rows), _slab(ys.at[s], lrow, rows), sems.at[s]).start())

    @pl.when(i == 0)
    def _():
        fetch(0, 0, True)

    nxt = jnp.minimum(i + 1, n_tiles - 1)
    fetch(nxt, 1 - slot, i + 1 < n_tiles)
    meta = jnp.transpose(jnp.concatenate([meta_ref[...], jnp.zeros((LANES - SUBLANES, TD), F32)], axis=0))
    pcol = lax.broadcasted_iota(jnp.int32, (TD, TOP_K * TD), 1).astype(F32)
    gather = (jnp.where(pcol == meta[:, 4:5], meta[:, 2:3], 0.0)
              + jnp.where(pcol == meta[:, 5:6], meta[:, 3:4], 0.0)).astype(BF16)
    pltpu.make_async_copy(yb_ref.at[pl.ds(0, TOP_K * TD * PIECES)], ys.at[slot], sems.at[slot]).wait()
    out = x_ref[...] + _dot(gather, _from_row_tiles(ys.at[slot], TOP_K * TD).astype(BF16))
    if final_norm:
        out = _rms(out, fg_ref[...])
    o_ref[...] = out


def _combine(tables, x, meta, fg, yb, final_norm):
    T, D = x.shape
    n = T // TD
    grid_spec = pltpu.PrefetchScalarGridSpec(
        num_scalar_prefetch=3,
        grid=(n,),
        in_specs=[
            pl.BlockSpec((TD, D), lambda i, *_: (i, 0)),
            pl.BlockSpec((SUBLANES, TD), lambda i, *_: (0, i)),
            pl.BlockSpec((1, D), lambda i, *_: (0, 0)),
            pl.BlockSpec(memory_space=pl.ANY),
        ],
        out_specs=pl.BlockSpec((TD, D), lambda i, *_: (i, 0)),
        scratch_shapes=[pltpu.VMEM((2, TOP_K * TD * PIECES, LANES), F32), pltpu.SemaphoreType.DMA((2,))],
    )
    return pl.pallas_call(
        functools.partial(_combine_kernel, final_norm=final_norm),
        grid_spec=grid_spec,
        out_shape=jax.ShapeDtypeStruct((T, D), F32),
        compiler_params=pltpu.CompilerParams(
            dimension_semantics=("arbitrary",), vmem_limit_bytes=VMEM_LIMIT),
        name="combine",
    )(*tables, x, meta, fg, yb)


def _routing_tables(cnt):
    n_tiles = cnt.shape[0]
    n = cnt[:, EXPERT_ROW0:EXPERT_ROW0 + N_EXPERTS, 0].astype(jnp.int32)
    counts = jnp.sum(n, axis=0)
    padded = (counts + MOE_BLOCK - 1) // MOE_BLOCK * MOE_BLOCK
    pend = jnp.cumsum(padded)
    pstart = pend - padded
    local = jnp.cumsum(n, axis=1) - n
    first = pstart[None, :] + jnp.cumsum(n, axis=0) - n
    n_blocks = (n_tiles * TD * TOP_K + MOE_BLOCK - 1) // MOE_BLOCK + N_EXPERTS
    block_row0 = jnp.arange(n_blocks, dtype=jnp.int32) * MOE_BLOCK
    block_expert = jnp.minimum(
        jnp.sum((pend[None, :] <= block_row0[:, None]).astype(jnp.int32), axis=1), N_EXPERTS - 1)
    pads = (pstart + counts, padded - counts, (pend[-1:] // MOE_BLOCK).astype(jnp.int32))
    return (n.reshape(-1), local.reshape(-1), first.reshape(-1)), pads, block_expert, n_blocks * MOE_BLOCK


def kernel(x, attn_norm_g, w_in, conv_w, conv_b, conv_ln_g, conv_ln_b, gk_w, gk_b, gla_norm_g, w_out, ffn_norm_g,
           router_group_w, router_group_b, router_expert_w, router_expert_b, expert_w_gate, expert_w_up,
           expert_w_down, final_norm_g):
    B, S, D = x.shape
    T = B * S
    depth = w_in.shape[0]
    x = x.reshape(T, D)
    for l in range(depth):
        win = jnp.pad(w_in[l], ((0, 0), (0, D_IN_PAD - w_in.shape[2]))).astype(BF16)
        cw = jnp.broadcast_to(conv_w[l][:, None, :], (CONV_WIDTH, CONV_ROWS, D_CONV))
        gkw = jnp.pad(gk_w[l], ((0, LANES - GATE_RANK), (0, 0))).astype(BF16)
        pad_g = EXPERT_ROW0 - N_GROUPS
        pad_e = LANES - EXPERT_ROW0 - N_EXPERTS
        rwt = jnp.concatenate([router_group_w[l].T, jnp.zeros((pad_g, D), F32), router_expert_w[l].T,
                               jnp.zeros((pad_e, D), F32)], axis=0).astype(BF16)
        rbt = jnp.concatenate([router_group_b[l], jnp.zeros((pad_g,), F32), router_expert_b[l].reshape(-1),
                               jnp.zeros((pad_e,), F32)])
        rbt = jnp.broadcast_to(rbt[:, None], (LANES, TM))
        xn, hf, meta, cnt = _mixer(
            x, S, attn_norm_g[l].reshape(1, D), win, cw, conv_b[l].reshape(1, -1), conv_ln_g[l].reshape(1, -1),
            conv_ln_b[l].reshape(1, -1), gkw, gk_b[l].reshape(1, -1), gla_norm_g[l].reshape(1, -1),
            w_out[l].astype(BF16), ffn_norm_g[l].reshape(1, D), rwt, rbt)
        tables, pads, block_expert, cap = _routing_tables(cnt)
        xs = _dispatch(tables, pads, hf, meta, cap)
        yb = _experts(block_expert, pads[2], xs, expert_w_gate, expert_w_up, expert_w_down, l)
        x = _combine(tables, xn, meta, final_norm_g.reshape(1, D), yb, final_norm=(l == depth - 1))
    return x.reshape(B, S, D)
```

```python
import functools

import jax
import jax.numpy as jnp
from jax import lax
from jax.experimental import pallas as pl
from jax.experimental.pallas import tpu as pltpu

D_MODEL = 1024
D_CONV = 512
D_GLA_V = 512
GLA_HEADS = 4
D_GLA_K = 256
HEAD_K = 64
HEAD_V = 128
GATE_RANK = 16
GATE_NORMALIZER = 16.0
CHUNK = 64
CONV_WIDTH = 31
N_GROUPS = 4
EXPERTS_PER_GROUP = 8
N_EXPERTS = 32
TOP_K = 2
D_EXPERT = 512
MOE_BLOCK = 512
EPS = 1e-6

LANES = 128
SUBLANES = 8
D_IN_MAIN = 2 * D_CONV + 2 * D_GLA_K + 2 * D_GLA_V
D_IN_PAD = D_IN_MAIN + LANES
TM = 256
HALO = 32
CONV_ROWS = 32
SHIFT_ROWS = TM + HALO - SUBLANES
TD = 256
EXPERT_ROW0 = 8
R_ROWS = 48
VMEM_LIMIT = 48 * 1024 * 1024

F32 = jnp.float32
BF16 = jnp.bfloat16


def _dot(a, b):
    return jnp.dot(a, b, preferred_element_type=F32)


def _dot_nt(a, b):
    return lax.dot_general(a, b, (((1,), (1,)), ((), ())), preferred_element_type=F32)


def _dot_tn(a, b):
    return lax.dot_general(a, b, (((0,), (0,)), ((), ())), preferred_element_type=F32)


def _split_bf16(x):
    hi = x.astype(BF16)
    lo = (x - hi.astype(F32)).astype(BF16)
    return hi, lo


def _rms(x, g):
    return x * lax.rsqrt(jnp.mean(x * x, axis=-1, keepdims=True) + EPS) * g


def _mixer_kernel(x_ref, ang_ref, win_ref, cw_ref, cb_ref, lng_ref, lnb_ref, gkw_ref, gkb_ref, gng_ref,
                  wout_ref, fng_ref, rwt_ref, rbt_ref,
                  xo_ref, hf_ref, meta_ref, cnt_ref,
                  ubuf, sbuf, ybuf, st_ref, *, tiles_per_seq):
    @pl.when(pl.program_id(0) % tiles_per_seq == 0)
    def _():
        ubuf[0:HALO, :] = jnp.zeros((HALO, D_CONV), F32)
        st_ref[...] = jnp.zeros_like(st_ref)

    x = x_ref[...]
    proj = _dot(_rms(x, ang_ref[...]).astype(BF16), win_ref[...])

    def piece(o0, width):
        return proj[:, o0:o0 + width]
    o_q = 2 * D_CONV
    o_v = o_q + 2 * D_GLA_K
    conv_out = _conv_branch(piece(0, D_CONV), piece(D_CONV, D_CONV), cw_ref, cb_ref, lng_ref, lnb_ref,
                            ubuf, sbuf, ybuf)
    gla_out = _gla_branch(piece(o_q, D_GLA_K), piece(o_q + D_GLA_K, D_GLA_K), piece(o_v, D_GLA_V),
                          piece(o_v + D_GLA_V, D_GLA_V), piece(D_IN_MAIN, LANES), gkw_ref, gkb_ref, gng_ref, st_ref)
    xn = (x + _dot(conv_out.astype(BF16), wout_ref[0:D_CONV, :])
          + _dot(gla_out.astype(BF16), wout_ref[D_CONV:, :]))
    xo_ref[...] = xn
    hf = _rms(xn, fng_ref[...]).astype(BF16)
    hf_ref[...] = hf
    meta_ref[...], cnt_ref[...] = _router(hf, rwt_ref, rbt_ref)


def _conv_branch(ua, ug, cw_ref, cb_ref, lng_ref, lnb_ref, ubuf, sbuf, ybuf):
    ubuf[HALO:HALO + TM, :] = ua * jax.nn.sigmoid(ug)
    for r in range(1, SUBLANES):
        sbuf[r - 1] = ubuf[r:r + SHIFT_ROWS, :]
    for c in range(TM // CONV_ROWS):
        acc = jnp.broadcast_to(cb_ref[...], (CONV_ROWS, D_CONV))
        for j in range(CONV_WIDTH):
            off = HALO - (CONV_WIDTH - 1) + j + c * CONV_ROWS
            a0, r = off - off % SUBLANES, off % SUBLANES
            tap = ubuf[a0:a0 + CONV_ROWS, :] if r == 0 else sbuf[r - 1, a0:a0 + CONV_ROWS, :]
            acc = acc + cw_ref[j] * tap
        ybuf[c * CONV_ROWS:(c + 1) * CONV_ROWS, :] = acc
    ubuf[0:HALO, :] = ubuf[TM:TM + HALO, :]
    y = ybuf[...]
    mu = jnp.mean(y, axis=-1, keepdims=True)
    yc = y - mu
    var = jnp.mean(yc * yc, axis=-1, keepdims=True)
    yn = yc * lax.rsqrt(var + EPS) * lng_ref[...] + lnb_ref[...]
    return yn * jax.nn.sigmoid(yn)


def _gla_branch(q, k, v, go, gkl, gkw_ref, gkb_ref, gng_ref, st_ref):
    gk = jax.nn.log_sigmoid(_dot(gkl.astype(BF16), gkw_ref[...]) + gkb_ref[...]) / GATE_NORMALIZER
    row = lax.broadcasted_iota(jnp.int32, (TM, TM), 0)
    col = lax.broadcasted_iota(jnp.int32, (TM, TM), 1)
    same_chunk = (row // CHUNK) == (col // CHUNK)
    causal = same_chunk & (col <= row)
    gk_hi, gk_lo = _split_bf16(gk)
    l_cum = causal.astype(BF16)
    b = _dot(l_cum, gk_hi) + _dot(l_cum, gk_lo)
    b_last = [b[c * CHUNK + CHUNK - 1:(c + 1) * CHUNK, :] for c in range(TM // CHUNK)]
    bl = jnp.concatenate([jnp.broadcast_to(t, (CHUNK, D_GLA_K)) for t in b_last], axis=0)
    qt = (q * (HEAD_K ** -0.5) * jnp.exp(b)).astype(BF16)
    kt = (k * jnp.exp(-b)).astype(BF16)
    ks = (k * jnp.exp(bl - b)).astype(BF16)
    vb = v.astype(BF16)

    klane = lax.broadcasted_iota(jnp.int32, (1, D_GLA_K), 1) // HEAD_K
    o_parts = []
    for h in range(GLA_HEADS):
        qh = jnp.where(klane == h, qt, jnp.zeros_like(qt))
        a = jnp.where(causal, _dot_nt(qh, kt), 0.0).astype(BF16)
        o_parts.append(_dot(a, vb[:, h * HEAD_V:(h + 1) * HEAD_V]))
    o_intra = jnp.concatenate(o_parts, axis=-1)

    srow = lax.broadcasted_iota(jnp.int32, (D_GLA_V, D_GLA_K), 0) // HEAD_V
    scol = lax.broadcasted_iota(jnp.int32, (D_GLA_V, D_GLA_K), 1) // HEAD_K
    head_diag = srow == scol
    o_inter = []
    for c in range(TM // CHUNK):
        r0 = c * CHUNK
        st = st_ref[...]
        o_inter.append(_dot_nt(qt[r0:r0 + CHUNK], st.astype(BF16)))
        ut = _dot_tn(vb[r0:r0 + CHUNK], ks[r0:r0 + CHUNK])
        st_ref[...] = st * jnp.exp(b_last[c]) + jnp.where(head_diag, ut, 0.0)
    o = o_intra + jnp.concatenate(o_inter, axis=0)
    o_n = []
    for h in range(GLA_HEADS):
        oh = o[:, h * HEAD_V:(h + 1) * HEAD_V]
        o_n.append(_rms(oh, gng_ref[...]))
    return jnp.concatenate(o_n, axis=-1) * (go * jax.nn.sigmoid(go))


def _router(hf, rwt_ref, rbt_ref):
    lgt = _dot_nt(rwt_ref[...], hf)[0:R_ROWS] + rbt_ref[0:R_ROWS]
    rowf = lax.broadcasted_iota(jnp.int32, (R_ROWS, TM), 0).astype(F32)
    neg = jnp.float32(-1e30)
    big = jnp.float32(R_ROWS)
    is_g = rowf < N_GROUPS
    gl = jnp.where(is_g, lgt, neg)
    gm = jnp.max(gl, axis=0, keepdims=True)
    grp_p = 1.0 / jnp.sum(jnp.where(is_g, jnp.exp(gl - gm), 0.0), axis=0, keepdims=True)
    gidx = jnp.min(jnp.where(is_g & (gl == gm), rowf, big), axis=0, keepdims=True)
    lo = EXPERT_ROW0 + gidx * EXPERTS_PER_GROUP
    in_sel = (rowf >= lo) & (rowf < lo + EXPERTS_PER_GROUP)
    sl = jnp.where(in_sel, lgt, neg)
    sm = jnp.max(sl, axis=0, keepdims=True)
    sz = jnp.sum(jnp.where(in_sel, jnp.exp(sl - sm), 0.0), axis=0, keepdims=True)
    i1 = jnp.min(jnp.where(in_sel & (sl == sm), rowf, big), axis=0, keepdims=True)
    rest = in_sel & (rowf != i1)
    sl2 = jnp.where(rest, sl, neg)
    sm2 = jnp.max(sl2, axis=0, keepdims=True)
    i2 = jnp.min(jnp.where(rest & (sl2 == sm2), rowf, big), axis=0, keepdims=True)
    w1 = 1.0 / sz
    w2 = jnp.exp(sm2 - sm) / sz
    den = w1 + w2
    g1 = grp_p * (w1 / den)
    g2 = grp_p * (w2 / den)
    oh1 = rowf == i1
    oh2 = rowf == i2
    oh_any = jnp.where(oh1 | oh2, 1.0, 0.0).astype(BF16)
    trow = lax.broadcasted_iota(jnp.int32, (TM, TM), 0)
    tcol = lax.broadcasted_iota(jnp.int32, (TM, TM), 1)
    earlier = (trow < tcol).astype(BF16)
    ones_tt = jnp.ones((TM, TM), BF16)
    below = (jnp.where(i1 < rowf, 1.0, 0.0) + jnp.where(i2 < rowf, 1.0, 0.0)).astype(BF16)
    base = _dot(below, ones_tt) + _dot(oh_any, earlier)
    p1 = jnp.sum(jnp.where(oh1, base, 0.0), axis=0, keepdims=True)
    p2 = jnp.sum(jnp.where(oh2, base, 0.0), axis=0, keepdims=True)
    counts = _dot(oh_any, ones_tt)[:, 0:LANES]
    mrow = lax.broadcasted_iota(jnp.int32, (SUBLANES, TM), 0)
    meta = jnp.where(mrow == 0, i1 - EXPERT_ROW0,
           jnp.where(mrow == 1, i2 - EXPERT_ROW0,
           jnp.where(mrow == 2, g1,
           jnp.where(mrow == 3, g2,
           jnp.where(mrow == 4, p1,
           jnp.where(mrow == 5, p2, 0.0))))))
    return meta, counts


def _mixer(x, seq_len, ang, win, cw, cb, lng, lnb, gkw, gkb, gng, wout, fng, rw, rb):
    T, D = x.shape
    n = T // TM
    const = lambda shape: pl.BlockSpec(shape, lambda g: (0,) * len(shape))
    tile = lambda w: pl.BlockSpec((TM, w), lambda g: (g, 0))
    return pl.pallas_call(
        functools.partial(_mixer_kernel, tiles_per_seq=seq_len // TM),
        grid=(n,),
        in_specs=[
            tile(D),
            const((1, D)), const((D, D_IN_PAD)), const((CONV_WIDTH, CONV_ROWS, D_CONV)), const((1, D_CONV)),
            const((1, D_CONV)), const((1, D_CONV)), const((LANES, D_GLA_K)), const((1, D_GLA_K)), const((1, HEAD_V)),
            const((D, D)), const((1, D)), const((LANES, D)), const((LANES, TM)),
        ],
        out_specs=[tile(D), tile(D),
                   pl.BlockSpec((SUBLANES, TM), lambda g: (0, g)),
                   pl.BlockSpec((None, R_ROWS, LANES), lambda g: (g, 0, 0))],
        out_shape=[
            jax.ShapeDtypeStruct((T, D), F32),
            jax.ShapeDtypeStruct((T, D), BF16),
            jax.ShapeDtypeStruct((SUBLANES, T), F32),
            jax.ShapeDtypeStruct((n, R_ROWS, LANES), F32),
        ],
        scratch_shapes=[
            pltpu.VMEM((TM + HALO, D_CONV), F32),
            pltpu.VMEM((SUBLANES - 1, SHIFT_ROWS, D_CONV), F32),
            pltpu.VMEM((TM, D_CONV), F32),
            pltpu.VMEM((D_GLA_V, D_GLA_K), F32),
        ],
        compiler_params=pltpu.CompilerParams(
            dimension_semantics=("arbitrary",), vmem_limit_bytes=VMEM_LIMIT),
        name="mixer",
    )(x, ang, win, cw, cb, lng, lnb, gkw, gkb, gng, wout, fng, rw, rb)


PIECES = D_MODEL // LANES
RING = 4


def _to_row_tiles(ref, value):
    for c in range(PIECES):
        ref[pl.ds(c, value.shape[0], stride=PIECES), :] = value[:, c * LANES:(c + 1) * LANES]


def _from_row_tiles(ref, n_rows):
    return jnp.concatenate([ref[pl.ds(c, n_rows, stride=PIECES), :] for c in range(PIECES)], axis=1)


def _for_each_run(tile, enabled, n_ref, lo_ref, gs_ref, visit):
    for e in range(N_EXPERTS):
        lo = lo_ref[tile * N_EXPERTS + e]
        gs = gs_ref[tile * N_EXPERTS + e]
        n = n_ref[tile * N_EXPERTS + e]

        @pl.when((n > 0) & enabled)
        def _():
            visit(lo, gs, n)


def _slab(ref, row, rows):
    return ref.at[pl.ds(pl.multiple_of(row * PIECES, PIECES), rows * PIECES)]


def _dispatch_kernel(n_ref, lo_ref, gs_ref, pad0_ref, padn_ref, used_ref, hf_ref, meta_ref, xs_ref,
                     srt, zeros, sems, zsem):
    i = pl.program_id(0)
    last = pl.num_programs(0) - 1
    slot = i % RING
    pos = meta_ref[4:4 + TOP_K, :]
    prow = lax.broadcasted_iota(jnp.int32, (TOP_K * TD, TD), 0).astype(F32)
    perm = jnp.where((prow == pos[0:1, :]) | (prow == pos[1:2, :]), 1.0, 0.0).astype(BF16)
    _to_row_tiles(srt.at[slot], _dot(perm, hf_ref[...]))
    _for_each_run(
        i, True, n_ref, lo_ref, gs_ref,
        lambda lrow, grow, rows: pltpu.make_async_copy(
            _slab(srt.at[slot], lrow, rows), _slab(xs_ref, grow, rows), sems.at[slot]).start())

    def wait_tile(s):
        pltpu.make_async_copy(srt.at[s], xs_ref.at[pl.ds(0, TOP_K * TD * PIECES)], sems.at[s]).wait()

    @pl.when(i >= RING - 1)
    def _():
        wait_tile((i + 1) % RING)

    @pl.when(i == last)
    def _():
        for back in range(RING - 2, -1, -1):
            @pl.when(i >= back)
            def _():
                wait_tile((i - back) % RING)
        zeros[...] = jnp.zeros_like(zeros)
        for e in range(N_EXPERTS):
            @pl.when(padn_ref[e] > 0)
            def _():
                cp = pltpu.make_async_copy(_slab(zeros, 0, padn_ref[e]), _slab(xs_ref, pad0_ref[e], padn_ref[e]), zsem)
                cp.start()
                cp.wait()

        def tail_block(j, carry):
            cp = pltpu.make_async_copy(zeros, _slab(xs_ref, j * MOE_BLOCK, MOE_BLOCK), zsem)
            cp.start()
            cp.wait()
            return carry

        lax.fori_loop(used_ref[0], xs_ref.shape[0] // (MOE_BLOCK * PIECES), tail_block, 0)


def _dispatch(tables, pads, hf, meta, cap):
    T, D = hf.shape
    n = T // TD
    grid_spec = pltpu.PrefetchScalarGridSpec(
        num_scalar_prefetch=6,
        grid=(n,),
        in_specs=[
            pl.BlockSpec((TD, D), lambda i, *_: (i, 0)),
            pl.BlockSpec((SUBLANES, TD), lambda i, *_: (0, i)),
        ],
        out_specs=pl.BlockSpec(memory_space=pl.ANY),
        scratch_shapes=[pltpu.VMEM((RING, TOP_K * TD * PIECES, LANES), F32),
                        pltpu.VMEM((MOE_BLOCK * PIECES, LANES), F32),
                        pltpu.SemaphoreType.DMA((RING,)), pltpu.SemaphoreType.DMA(())],
    )
    return pl.pallas_call(
        _dispatch_kernel,
        grid_spec=grid_spec,
        out_shape=jax.ShapeDtypeStruct((cap * PIECES, LANES), F32),
        compiler_params=pltpu.CompilerParams(dimension_semantics=("arbitrary",), vmem_limit_bytes=VMEM_LIMIT),
        name="dispatch",
    )(*tables, *pads, hf, meta)


def _expert_kernel(be_ref, used_ref, xs_ref, wg_ref, wu_ref, wd_ref, yb_ref, wg_b, wu_b, wd_b):
    i = pl.program_id(0)

    @pl.when((i == 0) | (be_ref[i] != be_ref[jnp.maximum(i - 1, 0)]))
    def _():
        wg_b[...] = wg_ref[...].astype(BF16)
        wu_b[...] = wu_ref[...].astype(BF16)
        wd_b[...] = wd_ref[...].astype(BF16)

    @pl.when(i < used_ref[0])
    def _():
        xb = _from_row_tiles(xs_ref, MOE_BLOCK).astype(BF16)
        g = _dot(xb, wg_b[...])
        u = _dot(xb, wu_b[...])
        h = (g * jax.nn.sigmoid(g) * u).astype(BF16)
        _to_row_tiles(yb_ref, _dot(h, wd_b[...]))

    @pl.when(i >= used_ref[0])
    def _():
        yb_ref[...] = jnp.zeros_like(yb_ref)


def _experts(block_expert, n_used, xs, wg, wu, wd, layer):
    D = D_MODEL
    n_blocks = xs.shape[0] // (MOE_BLOCK * PIECES)
    grid_spec = pltpu.PrefetchScalarGridSpec(
        num_scalar_prefetch=2,
        grid=(n_blocks,),
        in_specs=[
            pl.BlockSpec((MOE_BLOCK * PIECES, LANES), lambda i, be, used: (i, 0)),
            pl.BlockSpec((None, None, D, D_EXPERT), lambda i, be, used: (layer, be[i], 0, 0)),
            pl.BlockSpec((None, None, D, D_EXPERT), lambda i, be, used: (layer, be[i], 0, 0)),
            pl.BlockSpec((None, None, D_EXPERT, D), lambda i, be, used: (layer, be[i], 0, 0)),
        ],
        out_specs=pl.BlockSpec((MOE_BLOCK * PIECES, LANES), lambda i, be, used: (i, 0)),
        scratch_shapes=[pltpu.VMEM((D, D_EXPERT), BF16), pltpu.VMEM((D, D_EXPERT), BF16),
                        pltpu.VMEM((D_EXPERT, D), BF16)],
    )
    return pl.pallas_call(
        _expert_kernel,
        grid_spec=grid_spec,
        out_shape=jax.ShapeDtypeStruct(xs.shape, F32),
        compiler_params=pltpu.CompilerParams(
            dimension_semantics=("arbitrary",), vmem_limit_bytes=VMEM_LIMIT),
        name="experts",
    )(block_expert, n_used, xs, wg, wu, wd)


def _combine_kernel(n_ref, lo_ref, gs_ref, x_ref, meta_ref, fg_ref, yb_ref, o_ref, ys, sems, *, final_norm):
    i = pl.program_id(0)
    n_tiles = pl.num_programs(0)
    slot = i % RING

    def fetch(tile, enabled):
        s = tile % RING
        _for_each_run(
            jnp.minimum(tile, n_tiles - 1), enabled & (tile < n_tiles), n_ref, lo_ref, gs_ref,
            lambda lrow, grow, rows: pltpu.make_async_copy(
                _slab(yb_ref, grow, rows), _slab(ys.at[s], lrow, rows), sems.at[s]).start())

    @pl.when(i == 0)
    def _():
        for ahead in range(RING - 1):
            fetch(i + ahead, True)

    fetch(i + RING - 1, True)
    meta = jnp.transpose(jnp.concatenate([meta_ref[...], jnp.zeros((LANES - SUBLANES, TD), F32)], axis=0))
    pcol = lax.broadcasted_iota(jnp.int32, (TD, TOP_K * TD), 1).astype(F32)
    gather = (jnp.where(pcol == meta[:, 4:5], meta[:, 2:3], 0.0)
              + jnp.where(pcol == meta[:, 5:6], meta[:, 3:4], 0.0)).astype(BF16)
    pltpu.make_async_copy(yb_ref.at[pl.ds(0, TOP_K * TD * PIECES)], ys.at[slot], sems.at[slot]).wait()
    out = x_ref[...] + _dot(gather, _from_row_tiles(ys.at[slot], TOP_K * TD).astype(BF16))
    if final_norm:
        out = _rms(out, fg_ref[...])
    o_ref[...] = out


def _combine(tables, x, meta, fg, yb, final_norm):
    T, D = x.shape
    n = T // TD
    grid_spec = pltpu.PrefetchScalarGridSpec(
        num_scalar_prefetch=3,
        grid=(n,),
        in_specs=[
            pl.BlockSpec((TD, D), lambda i, *_: (i, 0)),
            pl.BlockSpec((SUBLANES, TD), lambda i, *_: (0, i)),
            pl.BlockSpec((1, D), lambda i, *_: (0, 0)),
            pl.BlockSpec(memory_space=pl.ANY),
        ],
        out_specs=pl.BlockSpec((TD, D), lambda i, *_: (i, 0)),
        scratch_shapes=[pltpu.VMEM((RING, TOP_K * TD * PIECES, LANES), F32), pltpu.SemaphoreType.DMA((RING,))],
    )
    return pl.pallas_call(
        functools.partial(_combine_kernel, final_norm=final_norm),
        grid_spec=grid_spec,
        out_shape=jax.ShapeDtypeStruct((T, D), F32),
        compiler_params=pltpu.CompilerParams(
            dimension_semantics=("arbitrary",), vmem_limit_bytes=VMEM_LIMIT),
        name="combine",
    )(*tables, x, meta, fg, yb)


def _routing_tables(cnt):
    n_tiles = cnt.shape[0]
    n = cnt[:, EXPERT_ROW0:EXPERT_ROW0 + N_EXPERTS, 0].astype(jnp.int32)
    counts = jnp.sum(n, axis=0)
    padded = (counts + MOE_BLOCK - 1) // MOE_BLOCK * MOE_BLOCK
    pend = jnp.cumsum(padded)
    pstart = pend - padded
    local = jnp.cumsum(n, axis=1) - n
    first = pstart[None, :] + jnp.cumsum(n, axis=0) - n
    n_blocks = (n_tiles * TD * TOP_K + MOE_BLOCK - 1) // MOE_BLOCK + N_EXPERTS
    block_row0 = jnp.arange(n_blocks, dtype=jnp.int32) * MOE_BLOCK
    block_expert = jnp.minimum(
        jnp.sum((pend[None, :] <= block_row0[:, None]).astype(jnp.int32), axis=1), N_EXPERTS - 1)
    pads = (pstart + counts, padded - counts, (pend[-1:] // MOE_BLOCK).astype(jnp.int32))
    return (n.reshape(-1), local.reshape(-1), first.reshape(-1)), pads, block_expert, n_blocks * MOE_BLOCK


def kernel(x, attn_norm_g, w_in, conv_w, conv_b, conv_ln_g, conv_ln_b, gk_w, gk_b, gla_norm_g, w_out, ffn_norm_g,
           router_group_w, router_group_b, router_expert_w, router_expert_b, expert_w_gate, expert_w_up,
           expert_w_down, final_norm_g):
    B, S, D = x.shape
    T = B * S
    depth = w_in.shape[0]
    x = x.reshape(T, D)
    for l in range(depth):
        win = jnp.pad(w_in[l], ((0, 0), (0, D_IN_PAD - w_in.shape[2]))).astype(BF16)
        cw = jnp.broadcast_to(conv_w[l][:, None, :], (CONV_WIDTH, CONV_ROWS, D_CONV))
        gkw = jnp.pad(gk_w[l], ((0, LANES - GATE_RANK), (0, 0))).astype(BF16)
        pad_g = EXPERT_ROW0 - N_GROUPS
        pad_e = LANES - EXPERT_ROW0 - N_EXPERTS
        rwt = jnp.concatenate([router_group_w[l].T, jnp.zeros((pad_g, D), F32), router_expert_w[l].T,
                               jnp.zeros((pad_e, D), F32)], axis=0).astype(BF16)
        rbt = jnp.concatenate([router_group_b[l], jnp.zeros((pad_g,), F32), router_expert_b[l].reshape(-1),
                               jnp.zeros((pad_e,), F32)])
        rbt = jnp.broadcast_to(rbt[:, None], (LANES, TM))
        xn, hf, meta, cnt = _mixer(
            x, S, attn_norm_g[l].reshape(1, D), win, cw, conv_b[l].reshape(1, -1), conv_ln_g[l].reshape(1, -1),
            conv_ln_b[l].reshape(1, -1), gkw, gk_b[l].reshape(1, -1), gla_norm_g[l].reshape(1, -1),
            w_out[l].astype(BF16), ffn_norm_g[l].reshape(1, D), rwt, rbt)
        tables, pads, block_expert, cap = _routing_tables(cnt)
        xs = _dispatch(tables, pads, hf, meta, cap)
        yb = _experts(block_expert, pads[2], xs, expert_w_gate, expert_w_up, expert_w_down, l)
        x = _combine(tables, xn, meta, final_norm_g.reshape(1, D), yb, final_norm=(l == depth - 1))
    return x.reshape(B, S, D)
```

```python
import functools

import jax
import jax.numpy as jnp
from jax import lax
from jax.experimental import pallas as pl
from jax.experimental.pallas import tpu as pltpu

D_MODEL = 1024
D_CONV = 512
D_GLA_V = 512
GLA_HEADS = 4
D_GLA_K = 256
HEAD_K = 64
HEAD_V = 128
GATE_RANK = 16
GATE_NORMALIZER = 16.0
CHUNK = 64
CONV_WIDTH = 31
N_GROUPS = 4
EXPERTS_PER_GROUP = 8
N_EXPERTS = 32
TOP_K = 2
D_EXPERT = 512
MOE_BLOCK = 512
EPS = 1e-6

LANES = 128
SUBLANES = 8
D_IN_MAIN = 2 * D_CONV + 2 * D_GLA_K + 2 * D_GLA_V
D_IN_PAD = D_IN_MAIN + LANES
TM = 256
HALO = 32
CONV_ROWS = 32
SHIFT_ROWS = TM + HALO - SUBLANES
TD = 256
EXPERT_ROW0 = 8
R_ROWS = 48
VMEM_LIMIT = 48 * 1024 * 1024

F32 = jnp.float32
BF16 = jnp.bfloat16


def _dot(a, b):
    return jnp.dot(a, b, preferred_element_type=F32)


def _dot_nt(a, b):
    return lax.dot_general(a, b, (((1,), (1,)), ((), ())), preferred_element_type=F32)


def _dot_tn(a, b):
    return lax.dot_general(a, b, (((0,), (0,)), ((), ())), preferred_element_type=F32)


def _split_bf16(x):
    hi = x.astype(BF16)
    lo = (x - hi.astype(F32)).astype(BF16)
    return hi, lo


def _rms(x, g):
    return x * lax.rsqrt(jnp.mean(x * x, axis=-1, keepdims=True) + EPS) * g


def _mixer_kernel(x_ref, ang_ref, win_ref, cw_ref, cb_ref, lng_ref, lnb_ref, gkw_ref, gkb_ref, gng_ref,
                  wout_ref, fng_ref, rwt_ref, rbt_ref,
                  xo_ref, hf_ref, meta_ref, cnt_ref,
                  ubuf, sbuf, ybuf, st_ref, *, tiles_per_seq):
    @pl.when(pl.program_id(0) % tiles_per_seq == 0)
    def _():
        ubuf[0:HALO, :] = jnp.zeros((HALO, D_CONV), F32)
        st_ref[...] = jnp.zeros_like(st_ref)

    x = x_ref[...]
    proj = _dot(_rms(x, ang_ref[...]).astype(BF16), win_ref[...])

    def piece(o0, width):
        return proj[:, o0:o0 + width]
    o_q = 2 * D_CONV
    o_v = o_q + 2 * D_GLA_K
    conv_out = _conv_branch(piece(0, D_CONV), piece(D_CONV, D_CONV), cw_ref, cb_ref, lng_ref, lnb_ref,
                            ubuf, sbuf, ybuf)
    gla_out = _gla_branch(piece(o_q, D_GLA_K), piece(o_q + D_GLA_K, D_GLA_K), piece(o_v, D_GLA_V),
                          piece(o_v + D_GLA_V, D_GLA_V), piece(D_IN_MAIN, LANES), gkw_ref, gkb_ref, gng_ref, st_ref)
    xn = (x + _dot(conv_out.astype(BF16), wout_ref[0:D_CONV, :])
          + _dot(gla_out.astype(BF16), wout_ref[D_CONV:, :]))
    xo_ref[...] = xn
    hf = _rms(xn, fng_ref[...]).astype(BF16)
    hf_ref[...] = hf
    meta_ref[...], cnt_ref[...] = _router(hf, rwt_ref, rbt_ref)


def _conv_branch(ua, ug, cw_ref, cb_ref, lng_ref, lnb_ref, ubuf, sbuf, ybuf):
    ubuf[HALO:HALO + TM, :] = ua * jax.nn.sigmoid(ug)
    for r in range(1, SUBLANES):
        sbuf[r - 1] = ubuf[r:r + SHIFT_ROWS, :]
    for c in range(TM // CONV_ROWS):
        acc = jnp.broadcast_to(cb_ref[...], (CONV_ROWS, D_CONV))
        for j in range(CONV_WIDTH):
            off = HALO - (CONV_WIDTH - 1) + j + c * CONV_ROWS
            a0, r = off - off % SUBLANES, off % SUBLANES
            tap = ubuf[a0:a0 + CONV_ROWS, :] if r == 0 else sbuf[r - 1, a0:a0 + CONV_ROWS, :]
            acc = acc + cw_ref[j] * tap
        ybuf[c * CONV_ROWS:(c + 1) * CONV_ROWS, :] = acc
    ubuf[0:HALO, :] = ubuf[TM:TM + HALO, :]
    y = ybuf[...]
    mu = jnp.mean(y, axis=-1, keepdims=True)
    yc = y - mu
    var = jnp.mean(yc * yc, axis=-1, keepdims=True)
    yn = yc * lax.rsqrt(var + EPS) * lng_ref[...] + lnb_ref[...]
    return yn * jax.nn.sigmoid(yn)


def _gla_branch(q, k, v, go, gkl, gkw_ref, gkb_ref, gng_ref, st_ref):
    gk = jax.nn.log_sigmoid(_dot(gkl.astype(BF16), gkw_ref[...]) + gkb_ref[...]) / GATE_NORMALIZER
    row = lax.broadcasted_iota(jnp.int32, (TM, TM), 0)
    col = lax.broadcasted_iota(jnp.int32, (TM, TM), 1)
    same_chunk = (row // CHUNK) == (col // CHUNK)
    causal = same_chunk & (col <= row)
    gk_hi, gk_lo = _split_bf16(gk)
    l_cum = causal.astype(BF16)
    b = _dot(l_cum, gk_hi) + _dot(l_cum, gk_lo)
    b_last = [b[c * CHUNK + CHUNK - 1:(c + 1) * CHUNK, :] for c in range(TM // CHUNK)]
    bl = jnp.concatenate([jnp.broadcast_to(t, (CHUNK, D_GLA_K)) for t in b_last], axis=0)
    qt = (q * (HEAD_K ** -0.5) * jnp.exp(b)).astype(BF16)
    kt = (k * jnp.exp(-b)).astype(BF16)
    ks = (k * jnp.exp(bl - b)).astype(BF16)
    vb = v.astype(BF16)

    klane = lax.broadcasted_iota(jnp.int32, (1, D_GLA_K), 1) // HEAD_K
    o_parts = []
    for h in range(GLA_HEADS):
        qh = jnp.where(klane == h, qt, jnp.zeros_like(qt))
        a = jnp.where(causal, _dot_nt(qh, kt), 0.0).astype(BF16)
        o_parts.append(_dot(a, vb[:, h * HEAD_V:(h + 1) * HEAD_V]))
    o_intra = jnp.concatenate(o_parts, axis=-1)

    srow = lax.broadcasted_iota(jnp.int32, (D_GLA_V, D_GLA_K), 0) // HEAD_V
    scol = lax.broadcasted_iota(jnp.int32, (D_GLA_V, D_GLA_K), 1) // HEAD_K
    head_diag = srow == scol
    o_inter = []
    for c in range(TM // CHUNK):
        r0 = c * CHUNK
        st = st_ref[...]
        o_inter.append(_dot_nt(qt[r0:r0 + CHUNK], st.astype(BF16)))
        ut = _dot_tn(vb[r0:r0 + CHUNK], ks[r0:r0 + CHUNK])
        st_ref[...] = st * jnp.exp(b_last[c]) + jnp.where(head_diag, ut, 0.0)
    o = o_intra + jnp.concatenate(o_inter, axis=0)
    o_n = []
    for h in range(GLA_HEADS):
        oh = o[:, h * HEAD_V:(h + 1) * HEAD_V]
        o_n.append(_rms(oh, gng_ref[...]))
    return jnp.concatenate(o_n, axis=-1) * (go * jax.nn.sigmoid(go))


def _router(hf, rwt_ref, rbt_ref):
    lgt = _dot_nt(rwt_ref[...], hf)[0:R_ROWS] + rbt_ref[0:R_ROWS]
    rowf = lax.broadcasted_iota(jnp.int32, (R_ROWS, TM), 0).astype(F32)
    neg = jnp.float32(-1e30)
    big = jnp.float32(R_ROWS)
    is_g = rowf < N_GROUPS
    gl = jnp.where(is_g, lgt, neg)
    gm = jnp.max(gl, axis=0, keepdims=True)
    grp_p = 1.0 / jnp.sum(jnp.where(is_g, jnp.exp(gl - gm), 0.0), axis=0, keepdims=True)
    gidx = jnp.min(jnp.where(is_g & (gl == gm), rowf, big), axis=0, keepdims=True)
    lo = EXPERT_ROW0 + gidx * EXPERTS_PER_GROUP
    in_sel = (rowf >= lo) & (rowf < lo + EXPERTS_PER_GROUP)
    sl = jnp.where(in_sel, lgt, neg)
    sm = jnp.max(sl, axis=0, keepdims=True)
    sz = jnp.sum(jnp.where(in_sel, jnp.exp(sl - sm), 0.0), axis=0, keepdims=True)
    i1 = jnp.min(jnp.where(in_sel & (sl == sm), rowf, big), axis=0, keepdims=True)
    rest = in_sel & (rowf != i1)
    sl2 = jnp.where(rest, sl, neg)
    sm2 = jnp.max(sl2, axis=0, keepdims=True)
    i2 = jnp.min(jnp.where(rest & (sl2 == sm2), rowf, big), axis=0, keepdims=True)
    w1 = 1.0 / sz
    w2 = jnp.exp(sm2 - sm) / sz
    den = w1 + w2
    g1 = grp_p * (w1 / den)
    g2 = grp_p * (w2 / den)
    oh1 = rowf == i1
    oh2 = rowf == i2
    oh_any = jnp.where(oh1 | oh2, 1.0, 0.0).astype(BF16)
    trow = lax.broadcasted_iota(jnp.int32, (TM, TM), 0)
    tcol = lax.broadcasted_iota(jnp.int32, (TM, TM), 1)
    earlier = (trow < tcol).astype(BF16)
    ones_tt = jnp.ones((TM, TM), BF16)
    below = (jnp.where(i1 < rowf, 1.0, 0.0) + jnp.where(i2 < rowf, 1.0, 0.0)).astype(BF16)
    base = _dot(below, ones_tt) + _dot(oh_any, earlier)
    p1 = jnp.sum(jnp.where(oh1, base, 0.0), axis=0, keepdims=True)
    p2 = jnp.sum(jnp.where(oh2, base, 0.0), axis=0, keepdims=True)
    counts = _dot(oh_any, ones_tt)[:, 0:LANES]
    mrow = lax.broadcasted_iota(jnp.int32, (SUBLANES, TM), 0)
    meta = jnp.where(mrow == 0, i1 - EXPERT_ROW0,
           jnp.where(mrow == 1, i2 - EXPERT_ROW0,
           jnp.where(mrow == 2, g1,
           jnp.where(mrow == 3, g2,
           jnp.where(mrow == 4, p1,
           jnp.where(mrow == 5, p2, 0.0))))))
    return meta, counts


def _mixer(x, seq_len, ang, win, cw, cb, lng, lnb, gkw, gkb, gng, wout, fng, rw, rb):
    T, D = x.shape
    n = T // TM
    const = lambda shape: pl.BlockSpec(shape, lambda g: (0,) * len(shape))
    tile = lambda w: pl.BlockSpec((TM, w), lambda g: (g, 0))
    return pl.pallas_call(
        functools.partial(_mixer_kernel, tiles_per_seq=seq_len // TM),
        grid=(n,),
        in_specs=[
            tile(D),
            const((1, D)), const((D, D_IN_PAD)), const((CONV_WIDTH, CONV_ROWS, D_CONV)), const((1, D_CONV)),
            const((1, D_CONV)), const((1, D_CONV)), const((LANES, D_GLA_K)), const((1, D_GLA_K)), const((1, HEAD_V)),
            const((D, D)), const((1, D)), const((LANES, D)), const((LANES, TM)),
        ],
        out_specs=[tile(D), tile(D),
                   pl.BlockSpec((SUBLANES, TM), lambda g: (0, g)),
                   pl.BlockSpec((None, R_ROWS, LANES), lambda g: (g, 0, 0))],
        out_shape=[
            jax.ShapeDtypeStruct((T, D), F32),
            jax.ShapeDtypeStruct((T, D), BF16),
            jax.ShapeDtypeStruct((SUBLANES, T), F32),
            jax.ShapeDtypeStruct((n, R_ROWS, LANES), F32),
        ],
        scratch_shapes=[
            pltpu.VMEM((TM + HALO, D_CONV), F32),
            pltpu.VMEM((SUBLANES - 1, SHIFT_ROWS, D_CONV), F32),
            pltpu.VMEM((TM, D_CONV), F32),
            pltpu.VMEM((D_GLA_V, D_GLA_K), F32),
        ],
        compiler_params=pltpu.CompilerParams(
            dimension_semantics=("arbitrary",), vmem_limit_bytes=VMEM_LIMIT),
        name="mixer",
    )(x, ang, win, cw, cb, lng, lnb, gkw, gkb, gng, wout, fng, rw, rb)


PIECES = D_MODEL // LANES
RING = 4


def _to_row_tiles(ref, value):
    for c in range(PIECES):
        ref[pl.ds(c, value.shape[0], stride=PIECES), :] = value[:, c * LANES:(c + 1) * LANES]


def _from_row_tiles(ref, n_rows):
    return jnp.concatenate([ref[pl.ds(c, n_rows, stride=PIECES), :] for c in range(PIECES)], axis=1)


def _for_each_run(tile, enabled, n_ref, lo_ref, gs_ref, visit):
    for e in range(N_EXPERTS):
        lo = lo_ref[tile * N_EXPERTS + e]
        gs = gs_ref[tile * N_EXPERTS + e]
        n = n_ref[tile * N_EXPERTS + e]

        @pl.when((n > 0) & enabled)
        def _():
            visit(lo, gs, n)


def _slab(ref, row, rows):
    return ref.at[pl.ds(pl.multiple_of(row * PIECES, PIECES), rows * PIECES)]


def _dispatch_kernel(n_ref, lo_ref, gs_ref, pad0_ref, padn_ref, used_ref, hf_ref, meta_ref, xs_ref,
                     srt, zeros, sems, zsem):
    i = pl.program_id(0)
    last = pl.num_programs(0) - 1
    slot = i % RING
    pos = meta_ref[4:4 + TOP_K, :]
    prow = lax.broadcasted_iota(jnp.int32, (TOP_K * TD, TD), 0).astype(F32)
    perm = jnp.where((prow == pos[0:1, :]) | (prow == pos[1:2, :]), 1.0, 0.0).astype(BF16)
    _to_row_tiles(srt.at[slot], _dot(perm, hf_ref[...]))
    _for_each_run(
        i, True, n_ref, lo_ref, gs_ref,
        lambda lrow, grow, rows: pltpu.make_async_copy(
            _slab(srt.at[slot], lrow, rows), _slab(xs_ref, grow, rows), sems.at[slot]).start())

    def wait_tile(s):
        pltpu.make_async_copy(srt.at[s], xs_ref.at[pl.ds(0, TOP_K * TD * PIECES)], sems.at[s]).wait()

    @pl.when(i >= RING - 1)
    def _():
        wait_tile((i + 1) % RING)

    @pl.when(i == last)
    def _():
        for back in range(RING - 2, -1, -1):
            @pl.when(i >= back)
            def _():
                wait_tile((i - back) % RING)
        zeros[...] = jnp.zeros_like(zeros)
        n_blocks = xs_ref.shape[0] // (MOE_BLOCK * PIECES)

        def pad_copy(e):
            return pltpu.make_async_copy(_slab(zeros, 0, padn_ref[e]), _slab(xs_ref, pad0_ref[e], padn_ref[e]), zsem)

        def tail_copy(j):
            return pltpu.make_async_copy(zeros, _slab(xs_ref, j * MOE_BLOCK, MOE_BLOCK), zsem)

        def for_each_fill(act):
            for e in range(N_EXPERTS):
                @pl.when(padn_ref[e] > 0)
                def _():
                    act(pad_copy(e))

            def tail_block(j, carry):
                act(tail_copy(j))
                return carry
            lax.fori_loop(used_ref[0], n_blocks, tail_block, 0)

        for_each_fill(lambda cp: cp.start())
        for_each_fill(lambda cp: cp.wait())


def _dispatch(tables, pads, hf, meta, cap):
    T, D = hf.shape
    n = T // TD
    grid_spec = pltpu.PrefetchScalarGridSpec(
        num_scalar_prefetch=6,
        grid=(n,),
        in_specs=[
            pl.BlockSpec((TD, D), lambda i, *_: (i, 0)),
            pl.BlockSpec((SUBLANES, TD), lambda i, *_: (0, i)),
        ],
        out_specs=pl.BlockSpec(memory_space=pl.ANY),
        scratch_shapes=[pltpu.VMEM((RING, TOP_K * TD * PIECES, LANES), F32),
                        pltpu.VMEM((MOE_BLOCK * PIECES, LANES), F32),
                        pltpu.SemaphoreType.DMA((RING,)), pltpu.SemaphoreType.DMA(())],
    )
    return pl.pallas_call(
        _dispatch_kernel,
        grid_spec=grid_spec,
        out_shape=jax.ShapeDtypeStruct((cap * PIECES, LANES), F32),
        compiler_params=pltpu.CompilerParams(dimension_semantics=("arbitrary",), vmem_limit_bytes=VMEM_LIMIT),
        name="dispatch",
    )(*tables, *pads, hf, meta)


def _expert_kernel(be_ref, used_ref, xs_ref, wg_ref, wu_ref, wd_ref, yb_ref, wg_b, wu_b, wd_b):
    i = pl.program_id(0)

    @pl.when((i == 0) | (be_ref[i] != be_ref[jnp.maximum(i - 1, 0)]))
    def _():
        wg_b[...] = wg_ref[...].astype(BF16)
        wu_b[...] = wu_ref[...].astype(BF16)
        wd_b[...] = wd_ref[...].astype(BF16)

    @pl.when(i < used_ref[0])
    def _():
        xb = _from_row_tiles(xs_ref, MOE_BLOCK).astype(BF16)
        g = _dot(xb, wg_b[...])
        u = _dot(xb, wu_b[...])
        h = (g * jax.nn.sigmoid(g) * u).astype(BF16)
        _to_row_tiles(yb_ref, _dot(h, wd_b[...]))

    @pl.when(i >= used_ref[0])
    def _():
        yb_ref[...] = jnp.zeros_like(yb_ref)


def _experts(block_expert, n_used, xs, wg, wu, wd, layer):
    D = D_MODEL
    n_blocks = xs.shape[0] // (MOE_BLOCK * PIECES)
    grid_spec = pltpu.PrefetchScalarGridSpec(
        num_scalar_prefetch=2,
        grid=(n_blocks,),
        in_specs=[
            pl.BlockSpec((MOE_BLOCK * PIECES, LANES), lambda i, be, used: (i, 0)),
            pl.BlockSpec((None, None, D, D_EXPERT), lambda i, be, used: (layer, be[i], 0, 0)),
            pl.BlockSpec((None, None, D, D_EXPERT), lambda i, be, used: (layer, be[i], 0, 0)),
            pl.BlockSpec((None, None, D_EXPERT, D), lambda i, be, used: (layer, be[i], 0, 0)),
        ],
        out_specs=pl.BlockSpec((MOE_BLOCK * PIECES, LANES), lambda i, be, used: (i, 0)),
        scratch_shapes=[pltpu.VMEM((D, D_EXPERT), BF16), pltpu.VMEM((D, D_EXPERT), BF16),
                        pltpu.VMEM((D_EXPERT, D), BF16)],
    )
    return pl.pallas_call(
        _expert_kernel,
        grid_spec=grid_spec,
        out_shape=jax.ShapeDtypeStruct(xs.shape, F32),
        compiler_params=pltpu.CompilerParams(
            dimension_semantics=("arbitrary",), vmem_limit_bytes=VMEM_LIMIT),
        name="experts",
    )(block_expert, n_used, xs, wg, wu, wd)


def _combine_kernel(n_ref, lo_ref, gs_ref, x_ref, meta_ref, fg_ref, yb_ref, o_ref, ys, sems, *, final_norm):
    i = pl.program_id(0)
    n_tiles = pl.num_programs(0)
    slot = i % RING

    def fetch(tile, enabled):
        s = tile % RING
        _for_each_run(
            jnp.minimum(tile, n_tiles - 1), enabled & (tile < n_tiles), n_ref, lo_ref, gs_ref,
            lambda lrow, grow, rows: pltpu.make_async_copy(
                _slab(yb_ref, grow, rows), _slab(ys.at[s], lrow, rows), sems.at[s]).start())

    @pl.when(i == 0)
    def _():
        for ahead in range(RING - 1):
            fetch(i + ahead, True)

    fetch(i + RING - 1, True)
    meta = jnp.transpose(jnp.concatenate([meta_ref[...], jnp.zeros((LANES - SUBLANES, TD), F32)], axis=0))
    pcol = lax.broadcasted_iota(jnp.int32, (TD, TOP_K * TD), 1).astype(F32)
    gather = (jnp.where(pcol == meta[:, 4:5], meta[:, 2:3], 0.0)
              + jnp.where(pcol == meta[:, 5:6], meta[:, 3:4], 0.0)).astype(BF16)
    pltpu.make_async_copy(yb_ref.at[pl.ds(0, TOP_K * TD * PIECES)], ys.at[slot], sems.at[slot]).wait()
    out = x_ref[...] + _dot(gather, _from_row_tiles(ys.at[slot], TOP_K * TD).astype(BF16))
    if final_norm:
        out = _rms(out, fg_ref[...])
    o_ref[...] = out


def _combine(tables, x, meta, fg, yb, final_norm):
    T, D = x.shape
    n = T // TD
    grid_spec = pltpu.PrefetchScalarGridSpec(
        num_scalar_prefetch=3,
        grid=(n,),
        in_specs=[
            pl.BlockSpec((TD, D), lambda i, *_: (i, 0)),
            pl.BlockSpec((SUBLANES, TD), lambda i, *_: (0, i)),
            pl.BlockSpec((1, D), lambda i, *_: (0, 0)),
            pl.BlockSpec(memory_space=pl.ANY),
        ],
        out_specs=pl.BlockSpec((TD, D), lambda i, *_: (i, 0)),
        scratch_shapes=[pltpu.VMEM((RING, TOP_K * TD * PIECES, LANES), F32), pltpu.SemaphoreType.DMA((RING,))],
    )
    return pl.pallas_call(
        functools.partial(_combine_kernel, final_norm=final_norm),
        grid_spec=grid_spec,
        out_shape=jax.ShapeDtypeStruct((T, D), F32),
        compiler_params=pltpu.CompilerParams(
            dimension_semantics=("arbitrary",), vmem_limit_bytes=VMEM_LIMIT),
        name="combine",
    )(*tables, x, meta, fg, yb)


def _routing_tables(cnt):
    n_tiles = cnt.shape[0]
    n = cnt[:, EXPERT_ROW0:EXPERT_ROW0 + N_EXPERTS, 0].astype(jnp.int32)
    counts = jnp.sum(n, axis=0)
    padded = (counts + MOE_BLOCK - 1) // MOE_BLOCK * MOE_BLOCK
    pend = jnp.cumsum(padded)
    pstart = pend - padded
    local = jnp.cumsum(n, axis=1) - n
    first = pstart[None, :] + jnp.cumsum(n, axis=0) - n
    n_blocks = (n_tiles * TD * TOP_K + MOE_BLOCK - 1) // MOE_BLOCK + N_EXPERTS
    block_row0 = jnp.arange(n_blocks, dtype=jnp.int32) * MOE_BLOCK
    block_expert = jnp.minimum(
        jnp.sum((pend[None, :] <= block_row0[:, None]).astype(jnp.int32), axis=1), N_EXPERTS - 1)
    pads = (pstart + counts, padded - counts, (pend[-1:] // MOE_BLOCK).astype(jnp.int32))
    return (n.reshape(-1), local.reshape(-1), first.reshape(-1)), pads, block_expert, n_blocks * MOE_BLOCK


def kernel(x, attn_norm_g, w_in, conv_w, conv_b, conv_ln_g, conv_ln_b, gk_w, gk_b, gla_norm_g, w_out, ffn_norm_g,
           router_group_w, router_group_b, router_expert_w, router_expert_b, expert_w_gate, expert_w_up,
           expert_w_down, final_norm_g):
    B, S, D = x.shape
    T = B * S
    depth = w_in.shape[0]
    x = x.reshape(T, D)
    for l in range(depth):
        win = jnp.pad(w_in[l], ((0, 0), (0, D_IN_PAD - w_in.shape[2]))).astype(BF16)
        cw = jnp.broadcast_to(conv_w[l][:, None, :], (CONV_WIDTH, CONV_ROWS, D_CONV))
        gkw = jnp.pad(gk_w[l], ((0, LANES - GATE_RANK), (0, 0))).astype(BF16)
        pad_g = EXPERT_ROW0 - N_GROUPS
        pad_e = LANES - EXPERT_ROW0 - N_EXPERTS
        rwt = jnp.concatenate([router_group_w[l].T, jnp.zeros((pad_g, D), F32), router_expert_w[l].T,
                               jnp.zeros((pad_e, D), F32)], axis=0).astype(BF16)
        rbt = jnp.concatenate([router_group_b[l], jnp.zeros((pad_g,), F32), router_expert_b[l].reshape(-1),
                               jnp.zeros((pad_e,), F32)])
        rbt = jnp.broadcast_to(rbt[:, None], (LANES, TM))
        xn, hf, meta, cnt = _mixer(
            x, S, attn_norm_g[l].reshape(1, D), win, cw, conv_b[l].reshape(1, -1), conv_ln_g[l].reshape(1, -1),
            conv_ln_b[l].reshape(1, -1), gkw, gk_b[l].reshape(1, -1), gla_norm_g[l].reshape(1, -1),
            w_out[l].astype(BF16), ffn_norm_g[l].reshape(1, D), rwt, rbt)
        tables, pads, block_expert, cap = _routing_tables(cnt)
        xs = _dispatch(tables, pads, hf, meta, cap)
        yb = _experts(block_expert, pads[2], xs, expert_w_gate, expert_w_up, expert_w_down, l)
        x = _combine(tables, xn, meta, final_norm_g.reshape(1, D), yb, final_norm=(l == depth - 1))
    return x.reshape(B, S, D)
```

```python
import functools

import jax
import jax.numpy as jnp
from jax import lax
from jax.experimental import pallas as pl
from jax.experimental.pallas import tpu as pltpu

D_MODEL = 1024
D_CONV = 512
D_GLA_V = 512
GLA_HEADS = 4
D_GLA_K = 256
HEAD_K = 64
HEAD_V = 128
GATE_RANK = 16
GATE_NORMALIZER = 16.0
CHUNK = 64
CONV_WIDTH = 31
N_GROUPS = 4
EXPERTS_PER_GROUP = 8
N_EXPERTS = 32
TOP_K = 2
D_EXPERT = 512
MOE_BLOCK = 512
EPS = 1e-6

LANES = 128
SUBLANES = 8
D_IN_MAIN = 2 * D_CONV + 2 * D_GLA_K + 2 * D_GLA_V
D_IN_PAD = D_IN_MAIN + LANES
TM = 256
HALO = 32
CONV_ROWS = 32
SHIFT_ROWS = TM + HALO - SUBLANES
TD = 256
EXPERT_ROW0 = 8
R_ROWS = 48
ROUTE_TILES = 8
ROUTE_ROWS = ROUTE_TILES * TM
VMEM_LIMIT = 48 * 1024 * 1024

F32 = jnp.float32
BF16 = jnp.bfloat16


def _dot(a, b):
    return jnp.dot(a, b, preferred_element_type=F32)


def _dot_nt(a, b):
    return lax.dot_general(a, b, (((1,), (1,)), ((), ())), preferred_element_type=F32)


def _dot_tn(a, b):
    return lax.dot_general(a, b, (((0,), (0,)), ((), ())), preferred_element_type=F32)


def _split_bf16(x):
    hi = x.astype(BF16)
    lo = (x - hi.astype(F32)).astype(BF16)
    return hi, lo


def _rms(x, g):
    return x * lax.rsqrt(jnp.mean(x * x, axis=-1, keepdims=True) + EPS) * g


def _mixer_kernel(x_ref, ang_ref, win_ref, cw_ref, cb_ref, lng_ref, lnb_ref, gkw_ref, gkb_ref, gng_ref,
                  wout_ref, fng_ref, rwt_ref, rbt_ref,
                  xo_ref, hf_ref, meta_ref, cnt_ref,
                  ubuf, sbuf, ybuf, st_ref, hf_scr, *, tiles_per_seq):
    @pl.when(pl.program_id(0) % tiles_per_seq == 0)
    def _():
        ubuf[0:HALO, :] = jnp.zeros((HALO, D_CONV), F32)
        st_ref[...] = jnp.zeros_like(st_ref)

    x = x_ref[...]
    proj = _dot(_rms(x, ang_ref[...]).astype(BF16), win_ref[...])

    def piece(o0, width):
        return proj[:, o0:o0 + width]
    o_q = 2 * D_CONV
    o_v = o_q + 2 * D_GLA_K
    conv_out = _conv_branch(piece(0, D_CONV), piece(D_CONV, D_CONV), cw_ref, cb_ref, lng_ref, lnb_ref,
                            ubuf, sbuf, ybuf)
    gla_out = _gla_branch(piece(o_q, D_GLA_K), piece(o_q + D_GLA_K, D_GLA_K), piece(o_v, D_GLA_V),
                          piece(o_v + D_GLA_V, D_GLA_V), piece(D_IN_MAIN, LANES), gkw_ref, gkb_ref, gng_ref, st_ref)
    xn = (x + _dot(conv_out.astype(BF16), wout_ref[0:D_CONV, :])
          + _dot(gla_out.astype(BF16), wout_ref[D_CONV:, :]))
    xo_ref[...] = xn
    hf = _rms(xn, fng_ref[...]).astype(BF16)
    hf_ref[...] = hf
    sub = pl.program_id(0) % ROUTE_TILES
    hf_scr[pl.ds(pl.multiple_of(sub * TM, TM), TM), :] = hf

    @pl.when(sub == ROUTE_TILES - 1)
    def _():
        meta_ref[...], cnt_ref[...] = _router(hf_scr[...], rwt_ref, rbt_ref)


def _conv_branch(ua, ug, cw_ref, cb_ref, lng_ref, lnb_ref, ubuf, sbuf, ybuf):
    ubuf[HALO:HALO + TM, :] = ua * jax.nn.sigmoid(ug)
    for r in range(1, SUBLANES):
        sbuf[r - 1] = ubuf[r:r + SHIFT_ROWS, :]
    for c in range(TM // CONV_ROWS):
        acc = jnp.broadcast_to(cb_ref[...], (CONV_ROWS, D_CONV))
        for j in range(CONV_WIDTH):
            off = HALO - (CONV_WIDTH - 1) + j + c * CONV_ROWS
            a0, r = off - off % SUBLANES, off % SUBLANES
            tap = ubuf[a0:a0 + CONV_ROWS, :] if r == 0 else sbuf[r - 1, a0:a0 + CONV_ROWS, :]
            acc = acc + cw_ref[j] * tap
        ybuf[c * CONV_ROWS:(c + 1) * CONV_ROWS, :] = acc
    ubuf[0:HALO, :] = ubuf[TM:TM + HALO, :]
    y = ybuf[...]
    mu = jnp.mean(y, axis=-1, keepdims=True)
    yc = y - mu
    var = jnp.mean(yc * yc, axis=-1, keepdims=True)
    yn = yc * lax.rsqrt(var + EPS) * lng_ref[...] + lnb_ref[...]
    return yn * jax.nn.sigmoid(yn)


def _gla_branch(q, k, v, go, gkl, gkw_ref, gkb_ref, gng_ref, st_ref):
    gk = jax.nn.log_sigmoid(_dot(gkl.astype(BF16), gkw_ref[...]) + gkb_ref[...]) / GATE_NORMALIZER
    row = lax.broadcasted_iota(jnp.int32, (TM, TM), 0)
    col = lax.broadcasted_iota(jnp.int32, (TM, TM), 1)
    same_chunk = (row // CHUNK) == (col // CHUNK)
    causal = same_chunk & (col <= row)
    gk_hi, gk_lo = _split_bf16(gk)
    l_cum = causal.astype(BF16)
    b = _dot(l_cum, gk_hi) + _dot(l_cum, gk_lo)
    b_last = [b[c * CHUNK + CHUNK - 1:(c + 1) * CHUNK, :] for c in range(TM // CHUNK)]
    bl = jnp.concatenate([jnp.broadcast_to(t, (CHUNK, D_GLA_K)) for t in b_last], axis=0)
    qt = (q * (HEAD_K ** -0.5) * jnp.exp(b)).astype(BF16)
    kt = (k * jnp.exp(-b)).astype(BF16)
    ks = (k * jnp.exp(bl - b)).astype(BF16)
    vb = v.astype(BF16)

    klane = lax.broadcasted_iota(jnp.int32, (1, D_GLA_K), 1) // HEAD_K
    o_parts = []
    for h in range(GLA_HEADS):
        qh = jnp.where(klane == h, qt, jnp.zeros_like(qt))
        a = jnp.where(causal, _dot_nt(qh, kt), 0.0).astype(BF16)
        o_parts.append(_dot(a, vb[:, h * HEAD_V:(h + 1) * HEAD_V]))
    o_intra = jnp.concatenate(o_parts, axis=-1)

    srow = lax.broadcasted_iota(jnp.int32, (D_GLA_V, D_GLA_K), 0) // HEAD_V
    scol = lax.broadcasted_iota(jnp.int32, (D_GLA_V, D_GLA_K), 1) // HEAD_K
    head_diag = srow == scol
    o_inter = []
    for c in range(TM // CHUNK):
        r0 = c * CHUNK
        st = st_ref[...]
        o_inter.append(_dot_nt(qt[r0:r0 + CHUNK], st.astype(BF16)))
        ut = _dot_tn(vb[r0:r0 + CHUNK], ks[r0:r0 + CHUNK])
        st_ref[...] = st * jnp.exp(b_last[c]) + jnp.where(head_diag, ut, 0.0)
    o = o_intra + jnp.concatenate(o_inter, axis=0)
    o_n = []
    for h in range(GLA_HEADS):
        oh = o[:, h * HEAD_V:(h + 1) * HEAD_V]
        o_n.append(_rms(oh, gng_ref[...]))
    return jnp.concatenate(o_n, axis=-1) * (go * jax.nn.sigmoid(go))


def _router(hf, rwt_ref, rbt_ref):
    lgt = _dot_nt(rwt_ref[...], hf)[0:R_ROWS] + rbt_ref[0:R_ROWS]
    rowf = lax.broadcasted_iota(jnp.int32, (R_ROWS, ROUTE_ROWS), 0).astype(F32)
    neg = jnp.float32(-1e30)
    big = jnp.float32(R_ROWS)
    is_g = rowf < N_GROUPS
    gl = jnp.where(is_g, lgt, neg)
    gm = jnp.max(gl, axis=0, keepdims=True)
    grp_p = 1.0 / jnp.sum(jnp.where(is_g, jnp.exp(gl - gm), 0.0), axis=0, keepdims=True)
    gidx = jnp.min(jnp.where(is_g & (gl == gm), rowf, big), axis=0, keepdims=True)
    lo = EXPERT_ROW0 + gidx * EXPERTS_PER_GROUP
    in_sel = (rowf >= lo) & (rowf < lo + EXPERTS_PER_GROUP)
    sl = jnp.where(in_sel, lgt, neg)
    sm = jnp.max(sl, axis=0, keepdims=True)
    sz = jnp.sum(jnp.where(in_sel, jnp.exp(sl - sm), 0.0), axis=0, keepdims=True)
    i1 = jnp.min(jnp.where(in_sel & (sl == sm), rowf, big), axis=0, keepdims=True)
    rest = in_sel & (rowf != i1)
    sl2 = jnp.where(rest, sl, neg)
    sm2 = jnp.max(sl2, axis=0, keepdims=True)
    i2 = jnp.min(jnp.where(rest & (sl2 == sm2), rowf, big), axis=0, keepdims=True)
    w1 = 1.0 / sz
    w2 = jnp.exp(sm2 - sm) / sz
    den = w1 + w2
    g1 = grp_p * (w1 / den)
    g2 = grp_p * (w2 / den)
    oh1 = rowf == i1
    oh2 = rowf == i2
    oh_any = jnp.where(oh1 | oh2, 1.0, 0.0).astype(BF16)
    trow = lax.broadcasted_iota(jnp.int32, (TM, TM), 0)
    tcol = lax.broadcasted_iota(jnp.int32, (TM, TM), 1)
    earlier = (trow < tcol).astype(BF16)
    ones_tt = jnp.ones((TM, TM), BF16)
    below = (jnp.where(i1 < rowf, 1.0, 0.0) + jnp.where(i2 < rowf, 1.0, 0.0)).astype(BF16)
    tiles = [slice(t * TM, (t + 1) * TM) for t in range(ROUTE_TILES)]
    base = jnp.concatenate([_dot(below[:, c], ones_tt) + _dot(oh_any[:, c], earlier) for c in tiles],
                           axis=1)
    p1 = jnp.sum(jnp.where(oh1, base, 0.0), axis=0, keepdims=True)
    p2 = jnp.sum(jnp.where(oh2, base, 0.0), axis=0, keepdims=True)
    counts = jnp.stack([_dot(oh_any[:, c], ones_tt)[:, 0:LANES] for c in tiles])
    mrow = lax.broadcasted_iota(jnp.int32, (SUBLANES, ROUTE_ROWS), 0)
    meta = jnp.where(mrow == 0, i1 - EXPERT_ROW0,
           jnp.where(mrow == 1, i2 - EXPERT_ROW0,
           jnp.where(mrow == 2, g1,
           jnp.where(mrow == 3, g2,
           jnp.where(mrow == 4, p1,
           jnp.where(mrow == 5, p2, 0.0))))))
    return meta, counts


def _mixer(x, seq_len, ang, win, cw, cb, lng, lnb, gkw, gkb, gng, wout, fng, rw, rb):
    T, D = x.shape
    n = T // TM
    const = lambda shape: pl.BlockSpec(shape, lambda g: (0,) * len(shape))
    tile = lambda w: pl.BlockSpec((TM, w), lambda g: (g, 0))
    return pl.pallas_call(
        functools.partial(_mixer_kernel, tiles_per_seq=seq_len // TM),
        grid=(n,),
        in_specs=[
            tile(D),
            const((1, D)), const((D, D_IN_PAD)), const((CONV_WIDTH, CONV_ROWS, D_CONV)), const((1, D_CONV)),
            const((1, D_CONV)), const((1, D_CONV)), const((LANES, D_GLA_K)), const((1, D_GLA_K)), const((1, HEAD_V)),
            const((D, D)), const((1, D)), const((LANES, D)), const((LANES, ROUTE_ROWS)),
        ],
        out_specs=[tile(D), tile(D),
                   pl.BlockSpec((SUBLANES, ROUTE_ROWS), lambda g: (0, g // ROUTE_TILES)),
                   pl.BlockSpec((ROUTE_TILES, R_ROWS, LANES), lambda g: (g // ROUTE_TILES, 0, 0))],
        out_shape=[
            jax.ShapeDtypeStruct((T, D), F32),
            jax.ShapeDtypeStruct((T, D), BF16),
            jax.ShapeDtypeStruct((SUBLANES, T), F32),
            jax.ShapeDtypeStruct((n, R_ROWS, LANES), F32),
        ],
        scratch_shapes=[
            pltpu.VMEM((TM + HALO, D_CONV), F32),
            pltpu.VMEM((SUBLANES - 1, SHIFT_ROWS, D_CONV), F32),
            pltpu.VMEM((TM, D_CONV), F32),
            pltpu.VMEM((D_GLA_V, D_GLA_K), F32),
            pltpu.VMEM((ROUTE_ROWS, D), BF16),
        ],
        compiler_params=pltpu.CompilerParams(
            dimension_semantics=("arbitrary",), vmem_limit_bytes=VMEM_LIMIT),
        name="mixer",
    )(x, ang, win, cw, cb, lng, lnb, gkw, gkb, gng, wout, fng, rw, rb)


PIECES = D_MODEL // LANES
RING = 4


def _to_row_tiles(ref, value):
    for c in range(PIECES):
        ref[pl.ds(c, value.shape[0], stride=PIECES), :] = value[:, c * LANES:(c + 1) * LANES]


def _from_row_tiles(ref, n_rows):
    return jnp.concatenate([ref[pl.ds(c, n_rows, stride=PIECES), :] for c in range(PIECES)], axis=1)


def _for_each_run(tile, enabled, n_ref, lo_ref, gs_ref, visit):
    for e in range(N_EXPERTS):
        lo = lo_ref[tile * N_EXPERTS + e]
        gs = gs_ref[tile * N_EXPERTS + e]
        n = n_ref[tile * N_EXPERTS + e]

        @pl.when((n > 0) & enabled)
        def _():
            visit(lo, gs, n)


def _slab(ref, row, rows):
    return ref.at[pl.ds(pl.multiple_of(row * PIECES, PIECES), rows * PIECES)]


def _dispatch_kernel(n_ref, lo_ref, gs_ref, pad0_ref, padn_ref, used_ref, hf_ref, meta_ref, xs_ref,
                     srt, zeros, sems, zsem):
    i = pl.program_id(0)
    last = pl.num_programs(0) - 1
    slot = i % RING
    pos = meta_ref[4:4 + TOP_K, :]
    prow = lax.broadcasted_iota(jnp.int32, (TOP_K * TD, TD), 0).astype(F32)
    perm = jnp.where((prow == pos[0:1, :]) | (prow == pos[1:2, :]), 1.0, 0.0).astype(BF16)
    _to_row_tiles(srt.at[slot], _dot(perm, hf_ref[...]))
    _for_each_run(
        i, True, n_ref, lo_ref, gs_ref,
        lambda lrow, grow, rows: pltpu.make_async_copy(
            _slab(srt.at[slot], lrow, rows), _slab(xs_ref, grow, rows), sems.at[slot]).start())

    def wait_tile(s):
        pltpu.make_async_copy(srt.at[s], xs_ref.at[pl.ds(0, TOP_K * TD * PIECES)], sems.at[s]).wait()

    @pl.when(i >= RING - 1)
    def _():
        wait_tile((i + 1) % RING)

    @pl.when(i == last)
    def _():
        for back in range(RING - 2, -1, -1):
            @pl.when(i >= back)
            def _():
                wait_tile((i - back) % RING)
        zeros[...] = jnp.zeros_like(zeros)
        n_blocks = xs_ref.shape[0] // (MOE_BLOCK * PIECES)

        def pad_copy(e):
            return pltpu.make_async_copy(_slab(zeros, 0, padn_ref[e]), _slab(xs_ref, pad0_ref[e], padn_ref[e]), zsem)

        def tail_copy(j):
            return pltpu.make_async_copy(zeros, _slab(xs_ref, j * MOE_BLOCK, MOE_BLOCK), zsem)

        def for_each_fill(act):
            for e in range(N_EXPERTS):
                @pl.when(padn_ref[e] > 0)
                def _():
                    act(pad_copy(e))

            def tail_block(j, carry):
                act(tail_copy(j))
                return carry
            lax.fori_loop(used_ref[0], n_blocks, tail_block, 0)

        for_each_fill(lambda cp: cp.start())
        for_each_fill(lambda cp: cp.wait())


def _dispatch(tables, pads, hf, meta, cap):
    T, D = hf.shape
    n = T // TD
    grid_spec = pltpu.PrefetchScalarGridSpec(
        num_scalar_prefetch=6,
        grid=(n,),
        in_specs=[
            pl.BlockSpec((TD, D), lambda i, *_: (i, 0)),
            pl.BlockSpec((SUBLANES, TD), lambda i, *_: (0, i)),
        ],
        out_specs=pl.BlockSpec(memory_space=pl.ANY),
        scratch_shapes=[pltpu.VMEM((RING, TOP_K * TD * PIECES, LANES), F32),
                        pltpu.VMEM((MOE_BLOCK * PIECES, LANES), F32),
                        pltpu.SemaphoreType.DMA((RING,)), pltpu.SemaphoreType.DMA(())],
    )
    return pl.pallas_call(
        _dispatch_kernel,
        grid_spec=grid_spec,
        out_shape=jax.ShapeDtypeStruct((cap * PIECES, LANES), F32),
        compiler_params=pltpu.CompilerParams(dimension_semantics=("arbitrary",), vmem_limit_bytes=VMEM_LIMIT),
        name="dispatch",
    )(*tables, *pads, hf, meta)


def _expert_kernel(be_ref, used_ref, xs_ref, wg_ref, wu_ref, wd_ref, yb_ref, wg_b, wu_b, wd_b):
    i = pl.program_id(0)

    @pl.when((i == 0) | (be_ref[i] != be_ref[jnp.maximum(i - 1, 0)]))
    def _():
        wg_b[...] = wg_ref[...].astype(BF16)
        wu_b[...] = wu_ref[...].astype(BF16)
        wd_b[...] = wd_ref[...].astype(BF16)

    @pl.when(i < used_ref[0])
    def _():
        xb = _from_row_tiles(xs_ref, MOE_BLOCK).astype(BF16)
        g = _dot(xb, wg_b[...])
        u = _dot(xb, wu_b[...])
        h = (g * jax.nn.sigmoid(g) * u).astype(BF16)
        _to_row_tiles(yb_ref, _dot(h, wd_b[...]))

    @pl.when(i >= used_ref[0])
    def _():
        yb_ref[...] = jnp.zeros_like(yb_ref)


def _experts(block_expert, n_used, xs, wg, wu, wd, layer):
    D = D_MODEL
    n_blocks = xs.shape[0] // (MOE_BLOCK * PIECES)
    grid_spec = pltpu.PrefetchScalarGridSpec(
        num_scalar_prefetch=2,
        grid=(n_blocks,),
        in_specs=[
            pl.BlockSpec((MOE_BLOCK * PIECES, LANES), lambda i, be, used: (i, 0)),
            pl.BlockSpec((None, None, D, D_EXPERT), lambda i, be, used: (layer, be[i], 0, 0)),
            pl.BlockSpec((None, None, D, D_EXPERT), lambda i, be, used: (layer, be[i], 0, 0)),
            pl.BlockSpec((None, None, D_EXPERT, D), lambda i, be, used: (layer, be[i], 0, 0)),
        ],
        out_specs=pl.BlockSpec((MOE_BLOCK * PIECES, LANES), lambda i, be, used: (i, 0)),
        scratch_shapes=[pltpu.VMEM((D, D_EXPERT), BF16), pltpu.VMEM((D, D_EXPERT), BF16),
                        pltpu.VMEM((D_EXPERT, D), BF16)],
    )
    return pl.pallas_call(
        _expert_kernel,
        grid_spec=grid_spec,
        out_shape=jax.ShapeDtypeStruct(xs.shape, F32),
        compiler_params=pltpu.CompilerParams(
            dimension_semantics=("arbitrary",), vmem_limit_bytes=VMEM_LIMIT),
        name="experts",
    )(block_expert, n_used, xs, wg, wu, wd)


def _combine_kernel(n_ref, lo_ref, gs_ref, x_ref, meta_ref, fg_ref, yb_ref, o_ref, ys, sems, *, final_norm):
    i = pl.program_id(0)
    n_tiles = pl.num_programs(0)
    slot = i % RING

    def fetch(tile, enabled):
        s = tile % RING
        _for_each_run(
            jnp.minimum(tile, n_tiles - 1), enabled & (tile < n_tiles), n_ref, lo_ref, gs_ref,
            lambda lrow, grow, rows: pltpu.make_async_copy(
                _slab(yb_ref, grow, rows), _slab(ys.at[s], lrow, rows), sems.at[s]).start())

    @pl.when(i == 0)
    def _():
        for ahead in range(RING - 1):
            fetch(i + ahead, True)

    fetch(i + RING - 1, True)
    meta = jnp.transpose(jnp.concatenate([meta_ref[...], jnp.zeros((LANES - SUBLANES, TD), F32)], axis=0))
    pcol = lax.broadcasted_iota(jnp.int32, (TD, TOP_K * TD), 1).astype(F32)
    gather = (jnp.where(pcol == meta[:, 4:5], meta[:, 2:3], 0.0)
              + jnp.where(pcol == meta[:, 5:6], meta[:, 3:4], 0.0)).astype(BF16)
    pltpu.make_async_copy(yb_ref.at[pl.ds(0, TOP_K * TD * PIECES)], ys.at[slot], sems.at[slot]).wait()
    out = x_ref[...] + _dot(gather, _from_row_tiles(ys.at[slot], TOP_K * TD).astype(BF16))
    if final_norm:
        out = _rms(out, fg_ref[...])
    o_ref[...] = out


def _combine(tables, x, meta, fg, yb, final_norm):
    T, D = x.shape
    n = T // TD
    grid_spec = pltpu.PrefetchScalarGridSpec(
        num_scalar_prefetch=3,
        grid=(n,),
        in_specs=[
            pl.BlockSpec((TD, D), lambda i, *_: (i, 0)),
            pl.BlockSpec((SUBLANES, TD), lambda i, *_: (0, i)),
            pl.BlockSpec((1, D), lambda i, *_: (0, 0)),
            pl.BlockSpec(memory_space=pl.ANY),
        ],
        out_specs=pl.BlockSpec((TD, D), lambda i, *_: (i, 0)),
        scratch_shapes=[pltpu.VMEM((RING, TOP_K * TD * PIECES, LANES), F32), pltpu.SemaphoreType.DMA((RING,))],
    )
    return pl.pallas_call(
        functools.partial(_combine_kernel, final_norm=final_norm),
        grid_spec=grid_spec,
        out_shape=jax.ShapeDtypeStruct((T, D), F32),
        compiler_params=pltpu.CompilerParams(
            dimension_semantics=("arbitrary",), vmem_limit_bytes=VMEM_LIMIT),
        name="combine",
    )(*tables, x, meta, fg, yb)


def _routing_tables(cnt):
    n_tiles = cnt.shape[0]
    n = cnt[:, EXPERT_ROW0:EXPERT_ROW0 + N_EXPERTS, 0].astype(jnp.int32)
    counts = jnp.sum(n, axis=0)
    padded = (counts + MOE_BLOCK - 1) // MOE_BLOCK * MOE_BLOCK
    pend = jnp.cumsum(padded)
    pstart = pend - padded
    local = jnp.cumsum(n, axis=1) - n
    first = pstart[None, :] + jnp.cumsum(n, axis=0) - n
    n_blocks = (n_tiles * TD * TOP_K + MOE_BLOCK - 1) // MOE_BLOCK + N_EXPERTS
    block_row0 = jnp.arange(n_blocks, dtype=jnp.int32) * MOE_BLOCK
    block_expert = jnp.minimum(
        jnp.sum((pend[None, :] <= block_row0[:, None]).astype(jnp.int32), axis=1), N_EXPERTS - 1)
    pads = (pstart + counts, padded - counts, (pend[-1:] // MOE_BLOCK).astype(jnp.int32))
    return (n.reshape(-1), local.reshape(-1), first.reshape(-1)), pads, block_expert, n_blocks * MOE_BLOCK


def kernel(x, attn_norm_g, w_in, conv_w, conv_b, conv_ln_g, conv_ln_b, gk_w, gk_b, gla_norm_g, w_out, ffn_norm_g,
           router_group_w, router_group_b, router_expert_w, router_expert_b, expert_w_gate, expert_w_up,
           expert_w_down, final_norm_g):
    B, S, D = x.shape
    T = B * S
    depth = w_in.shape[0]
    x = x.reshape(T, D)
    for l in range(depth):
        win = jnp.pad(w_in[l], ((0, 0), (0, D_IN_PAD - w_in.shape[2]))).astype(BF16)
        cw = jnp.broadcast_to(conv_w[l][:, None, :], (CONV_WIDTH, CONV_ROWS, D_CONV))
        gkw = jnp.pad(gk_w[l], ((0, LANES - GATE_RANK), (0, 0))).astype(BF16)
        pad_g = EXPERT_ROW0 - N_GROUPS
        pad_e = LANES - EXPERT_ROW0 - N_EXPERTS
        rwt = jnp.concatenate([router_group_w[l].T, jnp.zeros((pad_g, D), F32), router_expert_w[l].T,
                               jnp.zeros((pad_e, D), F32)], axis=0).astype(BF16)
        rbt = jnp.concatenate([router_group_b[l], jnp.zeros((pad_g,), F32), router_expert_b[l].reshape(-1),
                               jnp.zeros((pad_e,), F32)])
        rbt = jnp.broadcast_to(rbt[:, None], (LANES, ROUTE_ROWS))
        xn, hf, meta, cnt = _mixer(
            x, S, attn_norm_g[l].reshape(1, D), win, cw, conv_b[l].reshape(1, -1), conv_ln_g[l].reshape(1, -1),
            conv_ln_b[l].reshape(1, -1), gkw, gk_b[l].reshape(1, -1), gla_norm_g[l].reshape(1, -1),
            w_out[l].astype(BF16), ffn_norm_g[l].reshape(1, D), rwt, rbt)
        tables, pads, block_expert, cap = _routing_tables(cnt)
        xs = _dispatch(tables, pads, hf, meta, cap)
        yb = _experts(block_expert, pads[2], xs, expert_w_gate, expert_w_up, expert_w_down, l)
        x = _combine(tables, xn, meta, final_norm_g.reshape(1, D), yb, final_norm=(l == depth - 1))
    return x.reshape(B, S, D)
```

```python
import functools

import jax
import jax.numpy as jnp
from jax import lax
from jax.experimental import pallas as pl
from jax.experimental.pallas import tpu as pltpu

D_MODEL = 1024
D_CONV = 512
D_GLA_V = 512
GLA_HEADS = 4
D_GLA_K = 256
HEAD_K = 64
HEAD_V = 128
GATE_RANK = 16
GATE_NORMALIZER = 16.0
CHUNK = 64
CONV_WIDTH = 31
N_GROUPS = 4
EXPERTS_PER_GROUP = 8
N_EXPERTS = 32
TOP_K = 2
D_EXPERT = 512
MOE_BLOCK = 512
EPS = 1e-6

LANES = 128
SUBLANES = 8
D_IN_MAIN = 2 * D_CONV + 2 * D_GLA_K + 2 * D_GLA_V
D_IN_PAD = D_IN_MAIN + LANES
TM = 256
HALO = 32
CONV_ROWS = 32
SHIFT_ROWS = TM + HALO - SUBLANES
TD = 512
EXPERT_ROW0 = 8
R_ROWS = 48
ROUTE_TILES = 8
ROUTE_ROWS = ROUTE_TILES * TM
VMEM_LIMIT = 48 * 1024 * 1024

F32 = jnp.float32
BF16 = jnp.bfloat16


def _dot(a, b):
    return jnp.dot(a, b, preferred_element_type=F32)


def _dot_nt(a, b):
    return lax.dot_general(a, b, (((1,), (1,)), ((), ())), preferred_element_type=F32)


def _dot_tn(a, b):
    return lax.dot_general(a, b, (((0,), (0,)), ((), ())), preferred_element_type=F32)


def _split_bf16(x):
    hi = x.astype(BF16)
    lo = (x - hi.astype(F32)).astype(BF16)
    return hi, lo


def _rms(x, g):
    return x * lax.rsqrt(jnp.mean(x * x, axis=-1, keepdims=True) + EPS) * g


def _mixer_kernel(x_ref, ang_ref, win_ref, cw_ref, cb_ref, lng_ref, lnb_ref, gkw_ref, gkb_ref, gng_ref,
                  wout_ref, fng_ref, rwt_ref, rbt_ref,
                  xo_ref, hf_ref, meta_ref, cnt_ref,
                  ubuf, sbuf, ybuf, st_ref, hf_scr, *, tiles_per_seq):
    @pl.when(pl.program_id(0) % tiles_per_seq == 0)
    def _():
        ubuf[0:HALO, :] = jnp.zeros((HALO, D_CONV), F32)
        st_ref[...] = jnp.zeros_like(st_ref)

    x = x_ref[...]
    proj = _dot(_rms(x, ang_ref[...]).astype(BF16), win_ref[...])

    def piece(o0, width):
        return proj[:, o0:o0 + width]
    o_q = 2 * D_CONV
    o_v = o_q + 2 * D_GLA_K
    conv_out = _conv_branch(piece(0, D_CONV), piece(D_CONV, D_CONV), cw_ref, cb_ref, lng_ref, lnb_ref,
                            ubuf, sbuf, ybuf)
    gla_out = _gla_branch(piece(o_q, D_GLA_K), piece(o_q + D_GLA_K, D_GLA_K), piece(o_v, D_GLA_V),
                          piece(o_v + D_GLA_V, D_GLA_V), piece(D_IN_MAIN, LANES), gkw_ref, gkb_ref, gng_ref, st_ref)
    xn = (x + _dot(conv_out.astype(BF16), wout_ref[0:D_CONV, :])
          + _dot(gla_out.astype(BF16), wout_ref[D_CONV:, :]))
    xo_ref[...] = xn
    hf = _rms(xn, fng_ref[...]).astype(BF16)
    hf_ref[...] = hf
    sub = pl.program_id(0) % ROUTE_TILES
    hf_scr[pl.ds(pl.multiple_of(sub * TM, TM), TM), :] = hf

    @pl.when(sub == ROUTE_TILES - 1)
    def _():
        meta_ref[...], cnt_ref[...] = _router(hf_scr[...], rwt_ref, rbt_ref)


def _conv_branch(ua, ug, cw_ref, cb_ref, lng_ref, lnb_ref, ubuf, sbuf, ybuf):
    ubuf[HALO:HALO + TM, :] = ua * jax.nn.sigmoid(ug)
    for r in range(1, SUBLANES):
        sbuf[r - 1] = ubuf[r:r + SHIFT_ROWS, :]
    for c in range(TM // CONV_ROWS):
        acc = jnp.broadcast_to(cb_ref[...], (CONV_ROWS, D_CONV))
        for j in range(CONV_WIDTH):
            off = HALO - (CONV_WIDTH - 1) + j + c * CONV_ROWS
            a0, r = off - off % SUBLANES, off % SUBLANES
            tap = ubuf[a0:a0 + CONV_ROWS, :] if r == 0 else sbuf[r - 1, a0:a0 + CONV_ROWS, :]
            acc = acc + cw_ref[j] * tap
        ybuf[c * CONV_ROWS:(c + 1) * CONV_ROWS, :] = acc
    ubuf[0:HALO, :] = ubuf[TM:TM + HALO, :]
    y = ybuf[...]
    mu = jnp.mean(y, axis=-1, keepdims=True)
    yc = y - mu
    var = jnp.mean(yc * yc, axis=-1, keepdims=True)
    yn = yc * lax.rsqrt(var + EPS) * lng_ref[...] + lnb_ref[...]
    return yn * jax.nn.sigmoid(yn)


def _gla_branch(q, k, v, go, gkl, gkw_ref, gkb_ref, gng_ref, st_ref):
    gk = jax.nn.log_sigmoid(_dot(gkl.astype(BF16), gkw_ref[...]) + gkb_ref[...]) / GATE_NORMALIZER
    row = lax.broadcasted_iota(jnp.int32, (TM, TM), 0)
    col = lax.broadcasted_iota(jnp.int32, (TM, TM), 1)
    same_chunk = (row // CHUNK) == (col // CHUNK)
    causal = same_chunk & (col <= row)
    gk_hi, gk_lo = _split_bf16(gk)
    l_cum = causal.astype(BF16)
    b = _dot(l_cum, gk_hi) + _dot(l_cum, gk_lo)
    b_last = [b[c * CHUNK + CHUNK - 1:(c + 1) * CHUNK, :] for c in range(TM // CHUNK)]
    bl = jnp.concatenate([jnp.broadcast_to(t, (CHUNK, D_GLA_K)) for t in b_last], axis=0)
    qt = (q * (HEAD_K ** -0.5) * jnp.exp(b)).astype(BF16)
    kt = (k * jnp.exp(-b)).astype(BF16)
    ks = (k * jnp.exp(bl - b)).astype(BF16)
    vb = v.astype(BF16)

    klane = lax.broadcasted_iota(jnp.int32, (1, D_GLA_K), 1) // HEAD_K
    o_parts = []
    for h in range(GLA_HEADS):
        qh = jnp.where(klane == h, qt, jnp.zeros_like(qt))
        a = jnp.where(causal, _dot_nt(qh, kt), 0.0).astype(BF16)
        o_parts.append(_dot(a, vb[:, h * HEAD_V:(h + 1) * HEAD_V]))
    o_intra = jnp.concatenate(o_parts, axis=-1)

    srow = lax.broadcasted_iota(jnp.int32, (D_GLA_V, D_GLA_K), 0) // HEAD_V
    scol = lax.broadcasted_iota(jnp.int32, (D_GLA_V, D_GLA_K), 1) // HEAD_K
    head_diag = srow == scol
    o_inter = []
    for c in range(TM // CHUNK):
        r0 = c * CHUNK
        st = st_ref[...]
        o_inter.append(_dot_nt(qt[r0:r0 + CHUNK], st.astype(BF16)))
        ut = _dot_tn(vb[r0:r0 + CHUNK], ks[r0:r0 + CHUNK])
        st_ref[...] = st * jnp.exp(b_last[c]) + jnp.where(head_diag, ut, 0.0)
    o = o_intra + jnp.concatenate(o_inter, axis=0)
    o_n = []
    for h in range(GLA_HEADS):
        oh = o[:, h * HEAD_V:(h + 1) * HEAD_V]
        o_n.append(_rms(oh, gng_ref[...]))
    return jnp.concatenate(o_n, axis=-1) * (go * jax.nn.sigmoid(go))


def _router(hf, rwt_ref, rbt_ref):
    lgt = _dot_nt(rwt_ref[...], hf)[0:R_ROWS] + rbt_ref[0:R_ROWS]
    rowf = lax.broadcasted_iota(jnp.int32, (R_ROWS, ROUTE_ROWS), 0).astype(F32)
    neg = jnp.float32(-1e30)
    big = jnp.float32(R_ROWS)
    is_g = rowf < N_GROUPS
    gl = jnp.where(is_g, lgt, neg)
    gm = jnp.max(gl, axis=0, keepdims=True)
    grp_p = 1.0 / jnp.sum(jnp.where(is_g, jnp.exp(gl - gm), 0.0), axis=0, keepdims=True)
    gidx = jnp.min(jnp.where(is_g & (gl == gm), rowf, big), axis=0, keepdims=True)
    lo = EXPERT_ROW0 + gidx * EXPERTS_PER_GROUP
    in_sel = (rowf >= lo) & (rowf < lo + EXPERTS_PER_GROUP)
    sl = jnp.where(in_sel, lgt, neg)
    sm = jnp.max(sl, axis=0, keepdims=True)
    sz = jnp.sum(jnp.where(in_sel, jnp.exp(sl - sm), 0.0), axis=0, keepdims=True)
    i1 = jnp.min(jnp.where(in_sel & (sl == sm), rowf, big), axis=0, keepdims=True)
    rest = in_sel & (rowf != i1)
    sl2 = jnp.where(rest, sl, neg)
    sm2 = jnp.max(sl2, axis=0, keepdims=True)
    i2 = jnp.min(jnp.where(rest & (sl2 == sm2), rowf, big), axis=0, keepdims=True)
    w1 = 1.0 / sz
    w2 = jnp.exp(sm2 - sm) / sz
    den = w1 + w2
    g1 = grp_p * (w1 / den)
    g2 = grp_p * (w2 / den)
    oh1 = rowf == i1
    oh2 = rowf == i2
    oh_any = jnp.where(oh1 | oh2, 1.0, 0.0).astype(BF16)
    trow = lax.broadcasted_iota(jnp.int32, (TD, TD), 0)
    tcol = lax.broadcasted_iota(jnp.int32, (TD, TD), 1)
    earlier = (trow < tcol).astype(BF16)
    ones_tt = jnp.ones((TD, TD), BF16)
    below = (jnp.where(i1 < rowf, 1.0, 0.0) + jnp.where(i2 < rowf, 1.0, 0.0)).astype(BF16)
    tiles = [slice(t * TD, (t + 1) * TD) for t in range(ROUTE_ROWS // TD)]
    base = jnp.concatenate([_dot(below[:, c], ones_tt) + _dot(oh_any[:, c], earlier) for c in tiles],
                           axis=1)
    p1 = jnp.sum(jnp.where(oh1, base, 0.0), axis=0, keepdims=True)
    p2 = jnp.sum(jnp.where(oh2, base, 0.0), axis=0, keepdims=True)
    counts = jnp.stack([_dot(oh_any[:, c], ones_tt)[:, 0:LANES] for c in tiles])
    mrow = lax.broadcasted_iota(jnp.int32, (SUBLANES, ROUTE_ROWS), 0)
    meta = jnp.where(mrow == 0, i1 - EXPERT_ROW0,
           jnp.where(mrow == 1, i2 - EXPERT_ROW0,
           jnp.where(mrow == 2, g1,
           jnp.where(mrow == 3, g2,
           jnp.where(mrow == 4, p1,
           jnp.where(mrow == 5, p2, 0.0))))))
    return meta, counts


def _mixer(x, seq_len, ang, win, cw, cb, lng, lnb, gkw, gkb, gng, wout, fng, rw, rb):
    T, D = x.shape
    n = T // TM
    const = lambda shape: pl.BlockSpec(shape, lambda g: (0,) * len(shape))
    tile = lambda w: pl.BlockSpec((TM, w), lambda g: (g, 0))
    return pl.pallas_call(
        functools.partial(_mixer_kernel, tiles_per_seq=seq_len // TM),
        grid=(n,),
        in_specs=[
            tile(D),
            const((1, D)), const((D, D_IN_PAD)), const((CONV_WIDTH, CONV_ROWS, D_CONV)), const((1, D_CONV)),
            const((1, D_CONV)), const((1, D_CONV)), const((LANES, D_GLA_K)), const((1, D_GLA_K)), const((1, HEAD_V)),
            const((D, D)), const((1, D)), const((LANES, D)), const((LANES, ROUTE_ROWS)),
        ],
        out_specs=[tile(D), tile(D),
                   pl.BlockSpec((SUBLANES, ROUTE_ROWS), lambda g: (0, g // ROUTE_TILES)),
                   pl.BlockSpec((ROUTE_ROWS // TD, R_ROWS, LANES), lambda g: (g // ROUTE_TILES, 0, 0))],
        out_shape=[
            jax.ShapeDtypeStruct((T, D), F32),
            jax.ShapeDtypeStruct((T, D), BF16),
            jax.ShapeDtypeStruct((SUBLANES, T), F32),
            jax.ShapeDtypeStruct((T // TD, R_ROWS, LANES), F32),
        ],
        scratch_shapes=[
            pltpu.VMEM((TM + HALO, D_CONV), F32),
            pltpu.VMEM((SUBLANES - 1, SHIFT_ROWS, D_CONV), F32),
            pltpu.VMEM((TM, D_CONV), F32),
            pltpu.VMEM((D_GLA_V, D_GLA_K), F32),
            pltpu.VMEM((ROUTE_ROWS, D), BF16),
        ],
        compiler_params=pltpu.CompilerParams(
            dimension_semantics=("arbitrary",), vmem_limit_bytes=VMEM_LIMIT),
        name="mixer",
    )(x, ang, win, cw, cb, lng, lnb, gkw, gkb, gng, wout, fng, rw, rb)


PIECES = D_MODEL // LANES
RING = 4


def _to_row_tiles(ref, value):
    for c in range(PIECES):
        ref[pl.ds(c, value.shape[0], stride=PIECES), :] = value[:, c * LANES:(c + 1) * LANES]


def _from_row_tiles(ref, n_rows):
    return jnp.concatenate([ref[pl.ds(c, n_rows, stride=PIECES), :] for c in range(PIECES)], axis=1)


def _for_each_run(tile, enabled, n_ref, lo_ref, gs_ref, visit):
    for e in range(N_EXPERTS):
        lo = lo_ref[tile * N_EXPERTS + e]
        gs = gs_ref[tile * N_EXPERTS + e]
        n = n_ref[tile * N_EXPERTS + e]

        @pl.when((n > 0) & enabled)
        def _():
            visit(lo, gs, n)


def _slab(ref, row, rows):
    return ref.at[pl.ds(pl.multiple_of(row * PIECES, PIECES), rows * PIECES)]


def _dispatch_kernel(n_ref, lo_ref, gs_ref, pad0_ref, padn_ref, used_ref, hf_ref, meta_ref, xs_ref,
                     srt, zeros, sems, zsem):
    i = pl.program_id(0)
    last = pl.num_programs(0) - 1
    slot = i % RING
    pos = meta_ref[4:4 + TOP_K, :]
    prow = lax.broadcasted_iota(jnp.int32, (TOP_K * TD, TD), 0).astype(F32)
    perm = jnp.where((prow == pos[0:1, :]) | (prow == pos[1:2, :]), 1.0, 0.0).astype(BF16)
    _to_row_tiles(srt.at[slot], _dot(perm, hf_ref[...]))
    _for_each_run(
        i, True, n_ref, lo_ref, gs_ref,
        lambda lrow, grow, rows: pltpu.make_async_copy(
            _slab(srt.at[slot], lrow, rows), _slab(xs_ref, grow, rows), sems.at[slot]).start())

    def wait_tile(s):
        pltpu.make_async_copy(srt.at[s], xs_ref.at[pl.ds(0, TOP_K * TD * PIECES)], sems.at[s]).wait()

    @pl.when(i >= RING - 1)
    def _():
        wait_tile((i + 1) % RING)

    @pl.when(i == last)
    def _():
        for back in range(RING - 2, -1, -1):
            @pl.when(i >= back)
            def _():
                wait_tile((i - back) % RING)
        zeros[...] = jnp.zeros_like(zeros)
        n_blocks = xs_ref.shape[0] // (MOE_BLOCK * PIECES)

        def pad_copy(e):
            return pltpu.make_async_copy(_slab(zeros, 0, padn_ref[e]), _slab(xs_ref, pad0_ref[e], padn_ref[e]), zsem)

        def tail_copy(j):
            return pltpu.make_async_copy(zeros, _slab(xs_ref, j * MOE_BLOCK, MOE_BLOCK), zsem)

        def for_each_fill(act):
            for e in range(N_EXPERTS):
                @pl.when(padn_ref[e] > 0)
                def _():
                    act(pad_copy(e))

            def tail_block(j, carry):
                act(tail_copy(j))
                return carry
            lax.fori_loop(used_ref[0], n_blocks, tail_block, 0)

        for_each_fill(lambda cp: cp.start())
        for_each_fill(lambda cp: cp.wait())


def _dispatch(tables, pads, hf, meta, cap):
    T, D = hf.shape
    n = T // TD
    grid_spec = pltpu.PrefetchScalarGridSpec(
        num_scalar_prefetch=6,
        grid=(n,),
        in_specs=[
            pl.BlockSpec((TD, D), lambda i, *_: (i, 0)),
            pl.BlockSpec((SUBLANES, TD), lambda i, *_: (0, i)),
        ],
        out_specs=pl.BlockSpec(memory_space=pl.ANY),
        scratch_shapes=[pltpu.VMEM((RING, TOP_K * TD * PIECES, LANES), F32),
                        pltpu.VMEM((MOE_BLOCK * PIECES, LANES), F32),
                        pltpu.SemaphoreType.DMA((RING,)), pltpu.SemaphoreType.DMA(())],
    )
    return pl.pallas_call(
        _dispatch_kernel,
        grid_spec=grid_spec,
        out_shape=jax.ShapeDtypeStruct((cap * PIECES, LANES), F32),
        compiler_params=pltpu.CompilerParams(dimension_semantics=("arbitrary",), vmem_limit_bytes=VMEM_LIMIT),
        name="dispatch",
    )(*tables, *pads, hf, meta)


def _expert_kernel(be_ref, used_ref, xs_ref, wg_ref, wu_ref, wd_ref, yb_ref, wg_b, wu_b, wd_b):
    i = pl.program_id(0)

    @pl.when((i == 0) | (be_ref[i] != be_ref[jnp.maximum(i - 1, 0)]))
    def _():
        wg_b[...] = wg_ref[...].astype(BF16)
        wu_b[...] = wu_ref[...].astype(BF16)
        wd_b[...] = wd_ref[...].astype(BF16)

    @pl.when(i < used_ref[0])
    def _():
        xb = _from_row_tiles(xs_ref, MOE_BLOCK).astype(BF16)
        g = _dot(xb, wg_b[...])
        u = _dot(xb, wu_b[...])
        h = (g * jax.nn.sigmoid(g) * u).astype(BF16)
        _to_row_tiles(yb_ref, _dot(h, wd_b[...]))

    @pl.when(i >= used_ref[0])
    def _():
        yb_ref[...] = jnp.zeros_like(yb_ref)


def _experts(block_expert, n_used, xs, wg, wu, wd, layer):
    D = D_MODEL
    n_blocks = xs.shape[0] // (MOE_BLOCK * PIECES)
    grid_spec = pltpu.PrefetchScalarGridSpec(
        num_scalar_prefetch=2,
        grid=(n_blocks,),
        in_specs=[
            pl.BlockSpec((MOE_BLOCK * PIECES, LANES), lambda i, be, used: (i, 0)),
            pl.BlockSpec((None, None, D, D_EXPERT), lambda i, be, used: (layer, be[i], 0, 0)),
            pl.BlockSpec((None, None, D, D_EXPERT), lambda i, be, used: (layer, be[i], 0, 0)),
            pl.BlockSpec((None, None, D_EXPERT, D), lambda i, be, used: (layer, be[i], 0, 0)),
        ],
        out_specs=pl.BlockSpec((MOE_BLOCK * PIECES, LANES), lambda i, be, used: (i, 0)),
        scratch_shapes=[pltpu.VMEM((D, D_EXPERT), BF16), pltpu.VMEM((D, D_EXPERT), BF16),
                        pltpu.VMEM((D_EXPERT, D), BF16)],
    )
    return pl.pallas_call(
        _expert_kernel,
        grid_spec=grid_spec,
        out_shape=jax.ShapeDtypeStruct(xs.shape, F32),
        compiler_params=pltpu.CompilerParams(
            dimension_semantics=("arbitrary",), vmem_limit_bytes=VMEM_LIMIT),
        name="experts",
    )(block_expert, n_used, xs, wg, wu, wd)


def _combine_kernel(n_ref, lo_ref, gs_ref, x_ref, meta_ref, fg_ref, yb_ref, o_ref, ys, sems, *, final_norm):
    i = pl.program_id(0)
    n_tiles = pl.num_programs(0)
    slot = i % RING

    def fetch(tile, enabled):
        s = tile % RING
        _for_each_run(
            jnp.minimum(tile, n_tiles - 1), enabled & (tile < n_tiles), n_ref, lo_ref, gs_ref,
            lambda lrow, grow, rows: pltpu.make_async_copy(
                _slab(yb_ref, grow, rows), _slab(ys.at[s], lrow, rows), sems.at[s]).start())

    @pl.when(i == 0)
    def _():
        for ahead in range(RING - 1):
            fetch(i + ahead, True)

    fetch(i + RING - 1, True)
    meta = jnp.transpose(jnp.concatenate([meta_ref[...], jnp.zeros((LANES - SUBLANES, TD), F32)], axis=0))
    pcol = lax.broadcasted_iota(jnp.int32, (TD, TOP_K * TD), 1).astype(F32)
    gather = (jnp.where(pcol == meta[:, 4:5], meta[:, 2:3], 0.0)
              + jnp.where(pcol == meta[:, 5:6], meta[:, 3:4], 0.0)).astype(BF16)
    pltpu.make_async_copy(yb_ref.at[pl.ds(0, TOP_K * TD * PIECES)], ys.at[slot], sems.at[slot]).wait()
    out = x_ref[...] + _dot(gather, _from_row_tiles(ys.at[slot], TOP_K * TD).astype(BF16))
    if final_norm:
        out = _rms(out, fg_ref[...])
    o_ref[...] = out


def _combine(tables, x, meta, fg, yb, final_norm):
    T, D = x.shape
    n = T // TD
    grid_spec = pltpu.PrefetchScalarGridSpec(
        num_scalar_prefetch=3,
        grid=(n,),
        in_specs=[
            pl.BlockSpec((TD, D), lambda i, *_: (i, 0)),
            pl.BlockSpec((SUBLANES, TD), lambda i, *_: (0, i)),
            pl.BlockSpec((1, D), lambda i, *_: (0, 0)),
            pl.BlockSpec(memory_space=pl.ANY),
        ],
        out_specs=pl.BlockSpec((TD, D), lambda i, *_: (i, 0)),
        scratch_shapes=[pltpu.VMEM((RING, TOP_K * TD * PIECES, LANES), F32), pltpu.SemaphoreType.DMA((RING,))],
    )
    return pl.pallas_call(
        functools.partial(_combine_kernel, final_norm=final_norm),
        grid_spec=grid_spec,
        out_shape=jax.ShapeDtypeStruct((T, D), F32),
        compiler_params=pltpu.CompilerParams(
            dimension_semantics=("arbitrary",), vmem_limit_bytes=VMEM_LIMIT),
        name="combine",
    )(*tables, x, meta, fg, yb)


def _routing_tables(cnt):
    n_tiles = cnt.shape[0]
    n = cnt[:, EXPERT_ROW0:EXPERT_ROW0 + N_EXPERTS, 0].astype(jnp.int32)
    counts = jnp.sum(n, axis=0)
    padded = (counts + MOE_BLOCK - 1) // MOE_BLOCK * MOE_BLOCK
    pend = jnp.cumsum(padded)
    pstart = pend - padded
    local = jnp.cumsum(n, axis=1) - n
    first = pstart[None, :] + jnp.cumsum(n, axis=0) - n
    n_blocks = (n_tiles * TD * TOP_K + MOE_BLOCK - 1) // MOE_BLOCK + N_EXPERTS
    block_row0 = jnp.arange(n_blocks, dtype=jnp.int32) * MOE_BLOCK
    block_expert = jnp.minimum(
        jnp.sum((pend[None, :] <= block_row0[:, None]).astype(jnp.int32), axis=1), N_EXPERTS - 1)
    pads = (pstart + counts, padded - counts, (pend[-1:] // MOE_BLOCK).astype(jnp.int32))
    return (n.reshape(-1), local.reshape(-1), first.reshape(-1)), pads, block_expert, n_blocks * MOE_BLOCK


def kernel(x, attn_norm_g, w_in, conv_w, conv_b, conv_ln_g, conv_ln_b, gk_w, gk_b, gla_norm_g, w_out, ffn_norm_g,
           router_group_w, router_group_b, router_expert_w, router_expert_b, expert_w_gate, expert_w_up,
           expert_w_down, final_norm_g):
    B, S, D = x.shape
    T = B * S
    depth = w_in.shape[0]
    x = x.reshape(T, D)
    for l in range(depth):
        win = jnp.pad(w_in[l], ((0, 0), (0, D_IN_PAD - w_in.shape[2]))).astype(BF16)
        cw = jnp.broadcast_to(conv_w[l][:, None, :], (CONV_WIDTH, CONV_ROWS, D_CONV))
        gkw = jnp.pad(gk_w[l], ((0, LANES - GATE_RANK), (0, 0))).astype(BF16)
        pad_g = EXPERT_ROW0 - N_GROUPS
        pad_e = LANES - EXPERT_ROW0 - N_EXPERTS
        rwt = jnp.concatenate([router_group_w[l].T, jnp.zeros((pad_g, D), F32), router_expert_w[l].T,
                               jnp.zeros((pad_e, D), F32)], axis=0).astype(BF16)
        rbt = jnp.concatenate([router_group_b[l], jnp.zeros((pad_g,), F32), router_expert_b[l].reshape(-1),
                               jnp.zeros((pad_e,), F32)])
        rbt = jnp.broadcast_to(rbt[:, None], (LANES, ROUTE_ROWS))
        xn, hf, meta, cnt = _mixer(
            x, S, attn_norm_g[l].reshape(1, D), win, cw, conv_b[l].reshape(1, -1), conv_ln_g[l].reshape(1, -1),
            conv_ln_b[l].reshape(1, -1), gkw, gk_b[l].reshape(1, -1), gla_norm_g[l].reshape(1, -1),
            w_out[l].astype(BF16), ffn_norm_g[l].reshape(1, D), rwt, rbt)
        tables, pads, block_expert, cap = _routing_tables(cnt)
        xs = _dispatch(tables, pads, hf, meta, cap)
        yb = _experts(block_expert, pads[2], xs, expert_w_gate, expert_w_up, expert_w_down, l)
        x = _combine(tables, xn, meta, final_norm_g.reshape(1, D), yb, final_norm=(l == depth - 1))
    return x.reshape(B, S, D)
```

```python
import functools

import jax
import jax.numpy as jnp
from jax import lax
from jax.experimental import pallas as pl
from jax.experimental.pallas import tpu as pltpu

D_MODEL = 1024
D_CONV = 512
D_GLA_V = 512
GLA_HEADS = 4
D_GLA_K = 256
HEAD_K = 64
HEAD_V = 128
GATE_RANK = 16
GATE_NORMALIZER = 16.0
CHUNK = 64
CONV_WIDTH = 31
N_GROUPS = 4
EXPERTS_PER_GROUP = 8
N_EXPERTS = 32
TOP_K = 2
D_EXPERT = 512
MOE_BLOCK = 512
EPS = 1e-6

LANES = 128
SUBLANES = 8
D_IN_MAIN = 2 * D_CONV + 2 * D_GLA_K + 2 * D_GLA_V
D_IN_PAD = D_IN_MAIN + LANES
TM = 256
HALO = 32
CONV_ROWS = 32
SHIFT_ROWS = TM + HALO - SUBLANES
TD = 512
EXPERT_ROW0 = 8
R_ROWS = 48
ROUTE_TILES = 8
ROUTE_ROWS = ROUTE_TILES * TM
VMEM_LIMIT = 48 * 1024 * 1024

F32 = jnp.float32
BF16 = jnp.bfloat16


def _dot(a, b):
    return jnp.dot(a, b, preferred_element_type=F32)


def _dot_nt(a, b):
    return lax.dot_general(a, b, (((1,), (1,)), ((), ())), preferred_element_type=F32)


def _dot_tn(a, b):
    return lax.dot_general(a, b, (((0,), (0,)), ((), ())), preferred_element_type=F32)


def _split_bf16(x):
    hi = x.astype(BF16)
    lo = (x - hi.astype(F32)).astype(BF16)
    return hi, lo


def _rms(x, g):
    return x * lax.rsqrt(jnp.mean(x * x, axis=-1, keepdims=True) + EPS) * g


def _mixer_kernel(x_ref, ang_ref, win_ref, cw_ref, cb_ref, lng_ref, lnb_ref, gkw_ref, gkb_ref, gng_ref,
                  wout_ref, fng_ref, rwt_ref, rbt_ref,
                  xo_ref, hf_ref, meta_ref, cnt_ref,
                  ubuf, sbuf, ybuf, st_ref, hf_scr, *, tiles_per_seq):
    @pl.when(pl.program_id(0) % tiles_per_seq == 0)
    def _():
        ubuf[0:HALO, :] = jnp.zeros((HALO, D_CONV), F32)
        st_ref[...] = jnp.zeros_like(st_ref)

    x = x_ref[...]
    proj = _dot(_rms(x, ang_ref[...]).astype(BF16), win_ref[...])

    def piece(o0, width):
        return proj[:, o0:o0 + width]
    o_q = 2 * D_CONV
    o_v = o_q + 2 * D_GLA_K
    conv_out = _conv_branch(piece(0, D_CONV), piece(D_CONV, D_CONV), cw_ref, cb_ref, lng_ref, lnb_ref,
                            ubuf, sbuf, ybuf)
    gla_out = _gla_branch(piece(o_q, D_GLA_K), piece(o_q + D_GLA_K, D_GLA_K), piece(o_v, D_GLA_V),
                          piece(o_v + D_GLA_V, D_GLA_V), piece(D_IN_MAIN, LANES), gkw_ref, gkb_ref, gng_ref, st_ref)
    xn = (x + _dot(conv_out.astype(BF16), wout_ref[0:D_CONV, :])
          + _dot(gla_out.astype(BF16), wout_ref[D_CONV:, :]))
    xo_ref[...] = xn
    hf = _rms(xn, fng_ref[...]).astype(BF16)
    hf_ref[...] = hf
    sub = pl.program_id(0) % ROUTE_TILES
    hf_scr[pl.ds(pl.multiple_of(sub * TM, TM), TM), :] = hf

    @pl.when(sub == ROUTE_TILES - 1)
    def _():
        meta_ref[...], cnt_ref[...] = _router(hf_scr[...], rwt_ref, rbt_ref)


def _conv_branch(ua, ug, cw_ref, cb_ref, lng_ref, lnb_ref, ubuf, sbuf, ybuf):
    ubuf[HALO:HALO + TM, :] = ua * jax.nn.sigmoid(ug)
    for r in range(1, SUBLANES):
        sbuf[r - 1] = ubuf[r:r + SHIFT_ROWS, :]
    for c in range(TM // CONV_ROWS):
        acc = jnp.broadcast_to(cb_ref[...], (CONV_ROWS, D_CONV))
        for j in range(CONV_WIDTH):
            off = HALO - (CONV_WIDTH - 1) + j + c * CONV_ROWS
            a0, r = off - off % SUBLANES, off % SUBLANES
            tap = ubuf[a0:a0 + CONV_ROWS, :] if r == 0 else sbuf[r - 1, a0:a0 + CONV_ROWS, :]
            acc = acc + cw_ref[j] * tap
        ybuf[c * CONV_ROWS:(c + 1) * CONV_ROWS, :] = acc
    ubuf[0:HALO, :] = ubuf[TM:TM + HALO, :]
    y = ybuf[...]
    mu = jnp.mean(y, axis=-1, keepdims=True)
    yc = y - mu
    var = jnp.mean(yc * yc, axis=-1, keepdims=True)
    yn = yc * lax.rsqrt(var + EPS) * lng_ref[...] + lnb_ref[...]
    return yn * jax.nn.sigmoid(yn)


def _gla_branch(q, k, v, go, gkl, gkw_ref, gkb_ref, gng_ref, st_ref):
    gk = jax.nn.log_sigmoid(_dot(gkl.astype(BF16), gkw_ref[...]) + gkb_ref[...]) / GATE_NORMALIZER
    row = lax.broadcasted_iota(jnp.int32, (TM, TM), 0)
    col = lax.broadcasted_iota(jnp.int32, (TM, TM), 1)
    same_chunk = (row // CHUNK) == (col // CHUNK)
    causal = same_chunk & (col <= row)
    gk_hi, gk_lo = _split_bf16(gk)
    l_cum = causal.astype(BF16)
    b = _dot(l_cum, gk_hi) + _dot(l_cum, gk_lo)
    b_last = [b[c * CHUNK + CHUNK - 1:(c + 1) * CHUNK, :] for c in range(TM // CHUNK)]
    bl = jnp.concatenate([jnp.broadcast_to(t, (CHUNK, D_GLA_K)) for t in b_last], axis=0)
    qt = (q * (HEAD_K ** -0.5) * jnp.exp(b)).astype(BF16)
    kt = (k * jnp.exp(-b)).astype(BF16)
    ks = (k * jnp.exp(bl - b)).astype(BF16)
    vb = v.astype(BF16)

    klane = lax.broadcasted_iota(jnp.int32, (1, D_GLA_K), 1) // HEAD_K
    o_parts = []
    for h in range(GLA_HEADS):
        qh = jnp.where(klane == h, qt, jnp.zeros_like(qt))
        a = jnp.where(causal, _dot_nt(qh, kt), 0.0).astype(BF16)
        o_parts.append(_dot(a, vb[:, h * HEAD_V:(h + 1) * HEAD_V]))
    o_intra = jnp.concatenate(o_parts, axis=-1)

    srow = lax.broadcasted_iota(jnp.int32, (D_GLA_V, D_GLA_K), 0) // HEAD_V
    scol = lax.broadcasted_iota(jnp.int32, (D_GLA_V, D_GLA_K), 1) // HEAD_K
    head_diag = srow == scol
    o_inter = []
    for c in range(TM // CHUNK):
        r0 = c * CHUNK
        st = st_ref[...]
        o_inter.append(_dot_nt(qt[r0:r0 + CHUNK], st.astype(BF16)))
        ut = _dot_tn(vb[r0:r0 + CHUNK], ks[r0:r0 + CHUNK])
        st_ref[...] = st * jnp.exp(b_last[c]) + jnp.where(head_diag, ut, 0.0)
    o = o_intra + jnp.concatenate(o_inter, axis=0)
    o_n = []
    for h in range(GLA_HEADS):
        oh = o[:, h * HEAD_V:(h + 1) * HEAD_V]
        o_n.append(_rms(oh, gng_ref[...]))
    return jnp.concatenate(o_n, axis=-1) * (go * jax.nn.sigmoid(go))


def _router(hf, rwt_ref, rbt_ref):
    lgt = _dot_nt(rwt_ref[...], hf)[0:R_ROWS] + rbt_ref[0:R_ROWS]
    rowf = lax.broadcasted_iota(jnp.int32, (R_ROWS, ROUTE_ROWS), 0).astype(F32)
    neg = jnp.float32(-1e30)
    big = jnp.float32(R_ROWS)
    is_g = rowf < N_GROUPS
    gl = jnp.where(is_g, lgt, neg)
    gm = jnp.max(gl, axis=0, keepdims=True)
    grp_p = 1.0 / jnp.sum(jnp.where(is_g, jnp.exp(gl - gm), 0.0), axis=0, keepdims=True)
    gidx = jnp.min(jnp.where(is_g & (gl == gm), rowf, big), axis=0, keepdims=True)
    lo = EXPERT_ROW0 + gidx * EXPERTS_PER_GROUP
    in_sel = (rowf >= lo) & (rowf < lo + EXPERTS_PER_GROUP)
    sl = jnp.where(in_sel, lgt, neg)
    sm = jnp.max(sl, axis=0, keepdims=True)
    sz = jnp.sum(jnp.where(in_sel, jnp.exp(sl - sm), 0.0), axis=0, keepdims=True)
    i1 = jnp.min(jnp.where(in_sel & (sl == sm), rowf, big), axis=0, keepdims=True)
    rest = in_sel & (rowf != i1)
    sl2 = jnp.where(rest, sl, neg)
    sm2 = jnp.max(sl2, axis=0, keepdims=True)
    i2 = jnp.min(jnp.where(rest & (sl2 == sm2), rowf, big), axis=0, keepdims=True)
    w1 = 1.0 / sz
    w2 = jnp.exp(sm2 - sm) / sz
    den = w1 + w2
    g1 = grp_p * (w1 / den)
    g2 = grp_p * (w2 / den)
    oh1 = rowf == i1
    oh2 = rowf == i2
    oh_any = jnp.where(oh1 | oh2, 1.0, 0.0).astype(BF16)
    trow = lax.broadcasted_iota(jnp.int32, (TD, TD), 0)
    tcol = lax.broadcasted_iota(jnp.int32, (TD, TD), 1)
    earlier = (trow < tcol).astype(BF16)
    ones_tt = jnp.ones((TD, TD), BF16)
    below = (jnp.where(i1 < rowf, 1.0, 0.0) + jnp.where(i2 < rowf, 1.0, 0.0)).astype(BF16)
    tiles = [slice(t * TD, (t + 1) * TD) for t in range(ROUTE_ROWS // TD)]
    base = jnp.concatenate([_dot(below[:, c], ones_tt) + _dot(oh_any[:, c], earlier) for c in tiles],
                           axis=1)
    p1 = jnp.sum(jnp.where(oh1, base, 0.0), axis=0, keepdims=True)
    p2 = jnp.sum(jnp.where(oh2, base, 0.0), axis=0, keepdims=True)
    counts = jnp.stack([_dot(oh_any[:, c], ones_tt)[:, 0:LANES] for c in tiles])
    mrow = lax.broadcasted_iota(jnp.int32, (SUBLANES, ROUTE_ROWS), 0)
    meta = jnp.where(mrow == 0, i1 - EXPERT_ROW0,
           jnp.where(mrow == 1, i2 - EXPERT_ROW0,
           jnp.where(mrow == 2, g1,
           jnp.where(mrow == 3, g2,
           jnp.where(mrow == 4, p1,
           jnp.where(mrow == 5, p2, 0.0))))))
    return meta, counts


def _mixer(x, seq_len, ang, win, cw, cb, lng, lnb, gkw, gkb, gng, wout, fng, rw, rb):
    T, D = x.shape
    n = T // TM
    const = lambda shape: pl.BlockSpec(shape, lambda g: (0,) * len(shape))
    tile = lambda w: pl.BlockSpec((TM, w), lambda g: (g, 0))
    return pl.pallas_call(
        functools.partial(_mixer_kernel, tiles_per_seq=seq_len // TM),
        grid=(n,),
        in_specs=[
            tile(D),
            const((1, D)), const((D, D_IN_PAD)), const((CONV_WIDTH, CONV_ROWS, D_CONV)), const((1, D_CONV)),
            const((1, D_CONV)), const((1, D_CONV)), const((LANES, D_GLA_K)), const((1, D_GLA_K)), const((1, HEAD_V)),
            const((D, D)), const((1, D)), const((LANES, D)), const((LANES, ROUTE_ROWS)),
        ],
        out_specs=[tile(D), tile(D),
                   pl.BlockSpec((SUBLANES, ROUTE_ROWS), lambda g: (0, g // ROUTE_TILES)),
                   pl.BlockSpec((ROUTE_ROWS // TD, R_ROWS, LANES), lambda g: (g // ROUTE_TILES, 0, 0))],
        out_shape=[
            jax.ShapeDtypeStruct((T, D), F32),
            jax.ShapeDtypeStruct((T, D), BF16),
            jax.ShapeDtypeStruct((SUBLANES, T), F32),
            jax.ShapeDtypeStruct((T // TD, R_ROWS, LANES), F32),
        ],
        scratch_shapes=[
            pltpu.VMEM((TM + HALO, D_CONV), F32),
            pltpu.VMEM((SUBLANES - 1, SHIFT_ROWS, D_CONV), F32),
            pltpu.VMEM((TM, D_CONV), F32),
            pltpu.VMEM((D_GLA_V, D_GLA_K), F32),
            pltpu.VMEM((ROUTE_ROWS, D), BF16),
        ],
        compiler_params=pltpu.CompilerParams(
            dimension_semantics=("arbitrary",), vmem_limit_bytes=VMEM_LIMIT),
        name="mixer",
    )(x, ang, win, cw, cb, lng, lnb, gkw, gkb, gng, wout, fng, rw, rb)


PIECES = D_MODEL // LANES
RING = 4


def _to_row_tiles(ref, value):
    for c in range(PIECES):
        ref[pl.ds(c, value.shape[0], stride=PIECES), :] = value[:, c * LANES:(c + 1) * LANES]


def _from_row_tiles(ref, n_rows):
    return jnp.concatenate([ref[pl.ds(c, n_rows, stride=PIECES), :] for c in range(PIECES)], axis=1)


def _for_each_run(tile, enabled, n_ref, lo_ref, gs_ref, visit):
    for e in range(N_EXPERTS):
        lo = lo_ref[tile * N_EXPERTS + e]
        gs = gs_ref[tile * N_EXPERTS + e]
        n = n_ref[tile * N_EXPERTS + e]

        @pl.when((n > 0) & enabled)
        def _():
            visit(lo, gs, n)


def _slab(ref, row, rows):
    return ref.at[pl.ds(pl.multiple_of(row * PIECES, PIECES), rows * PIECES)]


def _dispatch_kernel(n_ref, lo_ref, gs_ref, pad0_ref, padn_ref, used_ref, hf_ref, meta_ref, xs_ref,
                     srt, zeros, sems, zsem):
    i = pl.program_id(0)
    last = pl.num_programs(0) - 1
    slot = i % RING
    pos = meta_ref[4:4 + TOP_K, :]
    prow = lax.broadcasted_iota(jnp.int32, (TOP_K * TD, TD), 0).astype(F32)
    perm = jnp.where((prow == pos[0:1, :]) | (prow == pos[1:2, :]), 1.0, 0.0).astype(BF16)
    _to_row_tiles(srt.at[slot], _dot(perm, hf_ref[...]))
    _for_each_run(
        i, True, n_ref, lo_ref, gs_ref,
        lambda lrow, grow, rows: pltpu.make_async_copy(
            _slab(srt.at[slot], lrow, rows), _slab(xs_ref, grow, rows), sems.at[slot]).start())

    def wait_tile(s):
        pltpu.make_async_copy(srt.at[s], xs_ref.at[pl.ds(0, TOP_K * TD * PIECES)], sems.at[s]).wait()

    @pl.when(i >= RING - 1)
    def _():
        wait_tile((i + 1) % RING)

    @pl.when(i == last)
    def _():
        for back in range(RING - 2, -1, -1):
            @pl.when(i >= back)
            def _():
                wait_tile((i - back) % RING)
        zeros[...] = jnp.zeros_like(zeros)
        n_blocks = xs_ref.shape[0] // (MOE_BLOCK * PIECES)

        def pad_copy(e):
            return pltpu.make_async_copy(_slab(zeros, 0, padn_ref[e]), _slab(xs_ref, pad0_ref[e], padn_ref[e]), zsem)

        def tail_copy(j):
            return pltpu.make_async_copy(zeros, _slab(xs_ref, j * MOE_BLOCK, MOE_BLOCK), zsem)

        def for_each_fill(act):
            for e in range(N_EXPERTS):
                @pl.when(padn_ref[e] > 0)
                def _():
                    act(pad_copy(e))

            def tail_block(j, carry):
                act(tail_copy(j))
                return carry
            lax.fori_loop(used_ref[0], n_blocks, tail_block, 0)

        for_each_fill(lambda cp: cp.start())
        for_each_fill(lambda cp: cp.wait())


def _dispatch(tables, pads, hf, meta, cap):
    T, D = hf.shape
    n = T // TD
    grid_spec = pltpu.PrefetchScalarGridSpec(
        num_scalar_prefetch=6,
        grid=(n,),
        in_specs=[
            pl.BlockSpec((TD, D), lambda i, *_: (i, 0)),
            pl.BlockSpec((SUBLANES, TD), lambda i, *_: (0, i)),
        ],
        out_specs=pl.BlockSpec(memory_space=pl.ANY),
        scratch_shapes=[pltpu.VMEM((RING, TOP_K * TD * PIECES, LANES), F32),
                        pltpu.VMEM((MOE_BLOCK * PIECES, LANES), F32),
                        pltpu.SemaphoreType.DMA((RING,)), pltpu.SemaphoreType.DMA(())],
    )
    return pl.pallas_call(
        _dispatch_kernel,
        grid_spec=grid_spec,
        out_shape=jax.ShapeDtypeStruct((cap * PIECES, LANES), F32),
        compiler_params=pltpu.CompilerParams(dimension_semantics=("arbitrary",), vmem_limit_bytes=VMEM_LIMIT),
        name="dispatch",
    )(*tables, *pads, hf, meta)


def _expert_kernel(be_ref, used_ref, nxt_ref, seg_ref, xs_ref, wg_hbm, wu_hbm, wd_hbm, yb_ref,
                   wg_s, wu_s, wd_s, wg_b, wu_b, wd_b, sems, *, layer):
    i = pl.program_id(0)
    e = be_ref[i]
    slot = seg_ref[i] % 2

    def weight_copies(expert, s):
        return [pltpu.make_async_copy(src.at[layer, expert], dst.at[s], sems.at[s])
                for src, dst in ((wg_hbm, wg_s), (wu_hbm, wu_s), (wd_hbm, wd_s))]

    @pl.when(i == 0)
    def _():
        for cp in weight_copies(e, slot):
            cp.start()

    @pl.when((i == 0) | (e != be_ref[jnp.maximum(i - 1, 0)]))
    def _():
        for cp in weight_copies(e, slot):
            cp.wait()
        wg_b[...] = wg_s[slot].astype(BF16)
        wu_b[...] = wu_s[slot].astype(BF16)
        wd_b[...] = wd_s[slot].astype(BF16)

        @pl.when(nxt_ref[i] != e)
        def _():
            for cp in weight_copies(nxt_ref[i], 1 - slot):
                cp.start()

    @pl.when(i < used_ref[0])
    def _():
        xb = _from_row_tiles(xs_ref, MOE_BLOCK).astype(BF16)
        g = _dot(xb, wg_b[...])
        u = _dot(xb, wu_b[...])
        h = (g * jax.nn.sigmoid(g) * u).astype(BF16)
        _to_row_tiles(yb_ref, _dot(h, wd_b[...]))

    @pl.when(i >= used_ref[0])
    def _():
        yb_ref[...] = jnp.zeros_like(yb_ref)


def _experts(block_expert, n_used, xs, wg, wu, wd, layer):
    D = D_MODEL
    n_blocks = xs.shape[0] // (MOE_BLOCK * PIECES)
    changed = jnp.concatenate([jnp.zeros((1,), jnp.int32), (block_expert[1:] != block_expert[:-1]).astype(jnp.int32)])
    segment = jnp.cumsum(changed)
    after = jnp.sum((block_expert[None, :] <= block_expert[:, None]).astype(jnp.int32), axis=1)
    next_expert = jnp.where(after < n_blocks, block_expert[jnp.minimum(after, n_blocks - 1)], block_expert)
    grid_spec = pltpu.PrefetchScalarGridSpec(
        num_scalar_prefetch=4,
        grid=(n_blocks,),
        in_specs=[
            pl.BlockSpec((MOE_BLOCK * PIECES, LANES), lambda i, be, used, *_: (jnp.minimum(i, used[0] - 1), 0)),
            pl.BlockSpec(memory_space=pl.ANY), pl.BlockSpec(memory_space=pl.ANY), pl.BlockSpec(memory_space=pl.ANY),
        ],
        out_specs=pl.BlockSpec((MOE_BLOCK * PIECES, LANES), lambda i, *_: (i, 0)),
        scratch_shapes=[pltpu.VMEM((2, D, D_EXPERT), F32), pltpu.VMEM((2, D, D_EXPERT), F32),
                        pltpu.VMEM((2, D_EXPERT, D), F32),
                        pltpu.VMEM((D, D_EXPERT), BF16), pltpu.VMEM((D, D_EXPERT), BF16),
                        pltpu.VMEM((D_EXPERT, D), BF16), pltpu.SemaphoreType.DMA((2,))],
    )
    return pl.pallas_call(
        functools.partial(_expert_kernel, layer=layer),
        grid_spec=grid_spec,
        out_shape=jax.ShapeDtypeStruct(xs.shape, F32),
        compiler_params=pltpu.CompilerParams(
            dimension_semantics=("arbitrary",), vmem_limit_bytes=VMEM_LIMIT),
        name="experts",
    )(block_expert, n_used, next_expert, segment, xs, wg, wu, wd)


def _combine_kernel(n_ref, lo_ref, gs_ref, x_ref, meta_ref, fg_ref, yb_ref, o_ref, ys, sems, *, final_norm):
    i = pl.program_id(0)
    n_tiles = pl.num_programs(0)
    slot = i % RING

    def fetch(tile, enabled):
        s = tile % RING
        _for_each_run(
            jnp.minimum(tile, n_tiles - 1), enabled & (tile < n_tiles), n_ref, lo_ref, gs_ref,
            lambda lrow, grow, rows: pltpu.make_async_copy(
                _slab(yb_ref, grow, rows), _slab(ys.at[s], lrow, rows), sems.at[s]).start())

    @pl.when(i == 0)
    def _():
        for ahead in range(RING - 1):
            fetch(i + ahead, True)

    fetch(i + RING - 1, True)
    meta = jnp.transpose(jnp.concatenate([meta_ref[...], jnp.zeros((LANES - SUBLANES, TD), F32)], axis=0))
    pcol = lax.broadcasted_iota(jnp.int32, (TD, TOP_K * TD), 1).astype(F32)
    gather = (jnp.where(pcol == meta[:, 4:5], meta[:, 2:3], 0.0)
              + jnp.where(pcol == meta[:, 5:6], meta[:, 3:4], 0.0)).astype(BF16)
    pltpu.make_async_copy(yb_ref.at[pl.ds(0, TOP_K * TD * PIECES)], ys.at[slot], sems.at[slot]).wait()
    out = x_ref[...] + _dot(gather, _from_row_tiles(ys.at[slot], TOP_K * TD).astype(BF16))
    if final_norm:
        out = _rms(out, fg_ref[...])
    o_ref[...] = out


def _combine(tables, x, meta, fg, yb, final_norm):
    T, D = x.shape
    n = T // TD
    grid_spec = pltpu.PrefetchScalarGridSpec(
        num_scalar_prefetch=3,
        grid=(n,),
        in_specs=[
            pl.BlockSpec((TD, D), lambda i, *_: (i, 0)),
            pl.BlockSpec((SUBLANES, TD), lambda i, *_: (0, i)),
            pl.BlockSpec((1, D), lambda i, *_: (0, 0)),
            pl.BlockSpec(memory_space=pl.ANY),
        ],
        out_specs=pl.BlockSpec((TD, D), lambda i, *_: (i, 0)),
        scratch_shapes=[pltpu.VMEM((RING, TOP_K * TD * PIECES, LANES), F32), pltpu.SemaphoreType.DMA((RING,))],
    )
    return pl.pallas_call(
        functools.partial(_combine_kernel, final_norm=final_norm),
        grid_spec=grid_spec,
        out_shape=jax.ShapeDtypeStruct((T, D), F32),
        compiler_params=pltpu.CompilerParams(
            dimension_semantics=("arbitrary",), vmem_limit_bytes=VMEM_LIMIT),
        name="combine",
    )(*tables, x, meta, fg, yb)


def _routing_tables(cnt):
    n_tiles = cnt.shape[0]
    n = cnt[:, EXPERT_ROW0:EXPERT_ROW0 + N_EXPERTS, 0].astype(jnp.int32)
    counts = jnp.sum(n, axis=0)
    padded = (counts + MOE_BLOCK - 1) // MOE_BLOCK * MOE_BLOCK
    pend = jnp.cumsum(padded)
    pstart = pend - padded
    local = jnp.cumsum(n, axis=1) - n
    first = pstart[None, :] + jnp.cumsum(n, axis=0) - n
    n_blocks = (n_tiles * TD * TOP_K + MOE_BLOCK - 1) // MOE_BLOCK + N_EXPERTS
    block_row0 = jnp.arange(n_blocks, dtype=jnp.int32) * MOE_BLOCK
    block_expert = jnp.minimum(
        jnp.sum((pend[None, :] <= block_row0[:, None]).astype(jnp.int32), axis=1), N_EXPERTS - 1)
    pads = (pstart + counts, padded - counts, (pend[-1:] // MOE_BLOCK).astype(jnp.int32))
    return (n.reshape(-1), local.reshape(-1), first.reshape(-1)), pads, block_expert, n_blocks * MOE_BLOCK


def kernel(x, attn_norm_g, w_in, conv_w, conv_b, conv_ln_g, conv_ln_b, gk_w, gk_b, gla_norm_g, w_out, ffn_norm_g,
           router_group_w, router_group_b, router_expert_w, router_expert_b, expert_w_gate, expert_w_up,
           expert_w_down, final_norm_g):
    B, S, D = x.shape
    T = B * S
    depth = w_in.shape[0]
    x = x.reshape(T, D)
    for l in range(depth):
        win = jnp.pad(w_in[l], ((0, 0), (0, D_IN_PAD - w_in.shape[2]))).astype(BF16)
        cw = jnp.broadcast_to(conv_w[l][:, None, :], (CONV_WIDTH, CONV_ROWS, D_CONV))
        gkw = jnp.pad(gk_w[l], ((0, LANES - GATE_RANK), (0, 0))).astype(BF16)
        pad_g = EXPERT_ROW0 - N_GROUPS
        pad_e = LANES - EXPERT_ROW0 - N_EXPERTS
        rwt = jnp.concatenate([router_group_w[l].T, jnp.zeros((pad_g, D), F32), router_expert_w[l].T,
                               jnp.zeros((pad_e, D), F32)], axis=0).astype(BF16)
        rbt = jnp.concatenate([router_group_b[l], jnp.zeros((pad_g,), F32), router_expert_b[l].reshape(-1),
                               jnp.zeros((pad_e,), F32)])
        rbt = jnp.broadcast_to(rbt[:, None], (LANES, ROUTE_ROWS))
        xn, hf, meta, cnt = _mixer(
            x, S, attn_norm_g[l].reshape(1, D), win, cw, conv_b[l].reshape(1, -1), conv_ln_g[l].reshape(1, -1),
            conv_ln_b[l].reshape(1, -1), gkw, gk_b[l].reshape(1, -1), gla_norm_g[l].reshape(1, -1),
            w_out[l].astype(BF16), ffn_norm_g[l].reshape(1, D), rwt, rbt)
        tables, pads, block_expert, cap = _routing_tables(cnt)
        xs = _dispatch(tables, pads, hf, meta, cap)
        yb = _experts(block_expert, pads[2], xs, expert_w_gate, expert_w_up, expert_w_down, l)
        x = _combine(tables, xn, meta, final_norm_g.reshape(1, D), yb, final_norm=(l == depth - 1))
    return x.reshape(B, S, D)
```

```python
import functools

import jax
import jax.numpy as jnp
from jax import lax
from jax.experimental import pallas as pl
from jax.experimental.pallas import tpu as pltpu

D_MODEL = 1024
D_CONV = 512
D_GLA_V = 512
GLA_HEADS = 4
D_GLA_K = 256
HEAD_K = 64
HEAD_V = 128
GATE_RANK = 16
GATE_NORMALIZER = 16.0
CHUNK = 64
CONV_WIDTH = 31
N_GROUPS = 4
EXPERTS_PER_GROUP = 8
N_EXPERTS = 32
TOP_K = 2
D_EXPERT = 512
MOE_BLOCK = 512
EPS = 1e-6

LANES = 128
SUBLANES = 8
D_IN_MAIN = 2 * D_CONV + 2 * D_GLA_K + 2 * D_GLA_V
D_IN_PAD = D_IN_MAIN + LANES
TM = 512
GLA_ROWS = 256
HALO = 32
CONV_ROWS = 32
SHIFT_ROWS = TM + HALO - SUBLANES
TD = 512
EXPERT_ROW0 = 8
R_ROWS = 48
ROUTE_ROWS = 2048
ROUTE_TILES = ROUTE_ROWS // TM
VMEM_LIMIT = 48 * 1024 * 1024

F32 = jnp.float32
BF16 = jnp.bfloat16


def _dot(a, b):
    return jnp.dot(a, b, preferred_element_type=F32)


def _dot_nt(a, b):
    return lax.dot_general(a, b, (((1,), (1,)), ((), ())), preferred_element_type=F32)


def _dot_tn(a, b):
    return lax.dot_general(a, b, (((0,), (0,)), ((), ())), preferred_element_type=F32)


def _split_bf16(x):
    hi = x.astype(BF16)
    lo = (x - hi.astype(F32)).astype(BF16)
    return hi, lo


def _rms(x, g):
    return x * lax.rsqrt(jnp.mean(x * x, axis=-1, keepdims=True) + EPS) * g


def _mixer_kernel(x_ref, ang_ref, win_ref, cw_ref, cb_ref, lng_ref, lnb_ref, gkw_ref, gkb_ref, gng_ref,
                  wout_ref, fng_ref, rwt_ref, rbt_ref,
                  xo_ref, hf_ref, meta_ref, cnt_ref,
                  ubuf, sbuf, ybuf, st_ref, hf_scr, *, tiles_per_seq):
    @pl.when(pl.program_id(0) % tiles_per_seq == 0)
    def _():
        ubuf[0:HALO, :] = jnp.zeros((HALO, D_CONV), F32)
        st_ref[...] = jnp.zeros_like(st_ref)

    x = x_ref[...]
    proj = _dot(_rms(x, ang_ref[...]).astype(BF16), win_ref[...])

    def piece(o0, width, rows=slice(None)):
        return proj[rows, o0:o0 + width]
    o_q = 2 * D_CONV
    o_v = o_q + 2 * D_GLA_K
    conv_out = _conv_branch(piece(0, D_CONV), piece(D_CONV, D_CONV), cw_ref, cb_ref, lng_ref, lnb_ref,
                            ubuf, sbuf, ybuf)
    gla_out = jnp.concatenate([
        _gla_branch(piece(o_q, D_GLA_K, rows), piece(o_q + D_GLA_K, D_GLA_K, rows), piece(o_v, D_GLA_V, rows),
                    piece(o_v + D_GLA_V, D_GLA_V, rows), piece(D_IN_MAIN, LANES, rows),
                    gkw_ref, gkb_ref, gng_ref, st_ref)
        for rows in (slice(r, r + GLA_ROWS) for r in range(0, TM, GLA_ROWS))], axis=0)
    xn = (x + _dot(conv_out.astype(BF16), wout_ref[0:D_CONV, :])
          + _dot(gla_out.astype(BF16), wout_ref[D_CONV:, :]))
    xo_ref[...] = xn
    hf = _rms(xn, fng_ref[...]).astype(BF16)
    hf_ref[...] = hf
    sub = pl.program_id(0) % ROUTE_TILES
    hf_scr[pl.ds(pl.multiple_of(sub * TM, TM), TM), :] = hf

    @pl.when(sub == ROUTE_TILES - 1)
    def _():
        meta_ref[...], cnt_ref[...] = _router(hf_scr[...], rwt_ref, rbt_ref)


def _conv_branch(ua, ug, cw_ref, cb_ref, lng_ref, lnb_ref, ubuf, sbuf, ybuf):
    ubuf[HALO:HALO + TM, :] = ua * jax.nn.sigmoid(ug)
    for r in range(1, SUBLANES):
        sbuf[r - 1] = ubuf[r:r + SHIFT_ROWS, :]
    for c in range(TM // CONV_ROWS):
        acc = jnp.broadcast_to(cb_ref[...], (CONV_ROWS, D_CONV))
        for j in range(CONV_WIDTH):
            off = HALO - (CONV_WIDTH - 1) + j + c * CONV_ROWS
            a0, r = off - off % SUBLANES, off % SUBLANES
            tap = ubuf[a0:a0 + CONV_ROWS, :] if r == 0 else sbuf[r - 1, a0:a0 + CONV_ROWS, :]
            acc = acc + cw_ref[j] * tap
        ybuf[c * CONV_ROWS:(c + 1) * CONV_ROWS, :] = acc
    ubuf[0:HALO, :] = ubuf[TM:TM + HALO, :]
    y = ybuf[...]
    mu = jnp.mean(y, axis=-1, keepdims=True)
    yc = y - mu
    var = jnp.mean(yc * yc, axis=-1, keepdims=True)
    yn = yc * lax.rsqrt(var + EPS) * lng_ref[...] + lnb_ref[...]
    return yn * jax.nn.sigmoid(yn)


def _gla_branch(q, k, v, go, gkl, gkw_ref, gkb_ref, gng_ref, st_ref):
    gk = jax.nn.log_sigmoid(_dot(gkl.astype(BF16), gkw_ref[...]) + gkb_ref[...]) / GATE_NORMALIZER
    row = lax.broadcasted_iota(jnp.int32, (GLA_ROWS, GLA_ROWS), 0)
    col = lax.broadcasted_iota(jnp.int32, (GLA_ROWS, GLA_ROWS), 1)
    same_chunk = (row // CHUNK) == (col // CHUNK)
    causal = same_chunk & (col <= row)
    gk_hi, gk_lo = _split_bf16(gk)
    l_cum = causal.astype(BF16)
    b = _dot(l_cum, gk_hi) + _dot(l_cum, gk_lo)
    b_last = [b[c * CHUNK + CHUNK - 1:(c + 1) * CHUNK, :] for c in range(GLA_ROWS // CHUNK)]
    bl = jnp.concatenate([jnp.broadcast_to(t, (CHUNK, D_GLA_K)) for t in b_last], axis=0)
    qt = (q * (HEAD_K ** -0.5) * jnp.exp(b)).astype(BF16)
    kt = (k * jnp.exp(-b)).astype(BF16)
    ks = (k * jnp.exp(bl - b)).astype(BF16)
    vb = v.astype(BF16)

    klane = lax.broadcasted_iota(jnp.int32, (1, D_GLA_K), 1) // HEAD_K
    o_parts = []
    for h in range(GLA_HEADS):
        qh = jnp.where(klane == h, qt, jnp.zeros_like(qt))
        a = jnp.where(causal, _dot_nt(qh, kt), 0.0).astype(BF16)
        o_parts.append(_dot(a, vb[:, h * HEAD_V:(h + 1) * HEAD_V]))
    o_intra = jnp.concatenate(o_parts, axis=-1)

    srow = lax.broadcasted_iota(jnp.int32, (D_GLA_V, D_GLA_K), 0) // HEAD_V
    scol = lax.broadcasted_iota(jnp.int32, (D_GLA_V, D_GLA_K), 1) // HEAD_K
    head_diag = srow == scol
    o_inter = []
    for c in range(GLA_ROWS // CHUNK):
        r0 = c * CHUNK
        st = st_ref[...]
        o_inter.append(_dot_nt(qt[r0:r0 + CHUNK], st.astype(BF16)))
        ut = _dot_tn(vb[r0:r0 + CHUNK], ks[r0:r0 + CHUNK])
        st_ref[...] = st * jnp.exp(b_last[c]) + jnp.where(head_diag, ut, 0.0)
    o = o_intra + jnp.concatenate(o_inter, axis=0)
    o_n = []
    for h in range(GLA_HEADS):
        oh = o[:, h * HEAD_V:(h + 1) * HEAD_V]
        o_n.append(_rms(oh, gng_ref[...]))
    return jnp.concatenate(o_n, axis=-1) * (go * jax.nn.sigmoid(go))


def _router(hf, rwt_ref, rbt_ref):
    lgt = _dot_nt(rwt_ref[...], hf)[0:R_ROWS] + rbt_ref[0:R_ROWS]
    rowf = lax.broadcasted_iota(jnp.int32, (R_ROWS, ROUTE_ROWS), 0).astype(F32)
    neg = jnp.float32(-1e30)
    big = jnp.float32(R_ROWS)
    is_g = rowf < N_GROUPS
    gl = jnp.where(is_g, lgt, neg)
    gm = jnp.max(gl, axis=0, keepdims=True)
    grp_p = 1.0 / jnp.sum(jnp.where(is_g, jnp.exp(gl - gm), 0.0), axis=0, keepdims=True)
    gidx = jnp.min(jnp.where(is_g & (gl == gm), rowf, big), axis=0, keepdims=True)
    lo = EXPERT_ROW0 + gidx * EXPERTS_PER_GROUP
    in_sel = (rowf >= lo) & (rowf < lo + EXPERTS_PER_GROUP)
    sl = jnp.where(in_sel, lgt, neg)
    sm = jnp.max(sl, axis=0, keepdims=True)
    sz = jnp.sum(jnp.where(in_sel, jnp.exp(sl - sm), 0.0), axis=0, keepdims=True)
    i1 = jnp.min(jnp.where(in_sel & (sl == sm), rowf, big), axis=0, keepdims=True)
    rest = in_sel & (rowf != i1)
    sl2 = jnp.where(rest, sl, neg)
    sm2 = jnp.max(sl2, axis=0, keepdims=True)
    i2 = jnp.min(jnp.where(rest & (sl2 == sm2), rowf, big), axis=0, keepdims=True)
    w1 = 1.0 / sz
    w2 = jnp.exp(sm2 - sm) / sz
    den = w1 + w2
    g1 = grp_p * (w1 / den)
    g2 = grp_p * (w2 / den)
    oh1 = rowf == i1
    oh2 = rowf == i2
    oh_any = jnp.where(oh1 | oh2, 1.0, 0.0).astype(BF16)
    trow = lax.broadcasted_iota(jnp.int32, (TD, TD), 0)
    tcol = lax.broadcasted_iota(jnp.int32, (TD, TD), 1)
    earlier = (trow < tcol).astype(BF16)
    ones_tt = jnp.ones((TD, TD), BF16)
    below = (jnp.where(i1 < rowf, 1.0, 0.0) + jnp.where(i2 < rowf, 1.0, 0.0)).astype(BF16)
    tiles = [slice(t * TD, (t + 1) * TD) for t in range(ROUTE_ROWS // TD)]
    base = jnp.concatenate([_dot(below[:, c], ones_tt) + _dot(oh_any[:, c], earlier) for c in tiles],
                           axis=1)
    p1 = jnp.sum(jnp.where(oh1, base, 0.0), axis=0, keepdims=True)
    p2 = jnp.sum(jnp.where(oh2, base, 0.0), axis=0, keepdims=True)
    counts = jnp.stack([_dot(oh_any[:, c], ones_tt)[:, 0:LANES] for c in tiles])
    mrow = lax.broadcasted_iota(jnp.int32, (SUBLANES, ROUTE_ROWS), 0)
    meta = jnp.where(mrow == 0, i1 - EXPERT_ROW0,
           jnp.where(mrow == 1, i2 - EXPERT_ROW0,
           jnp.where(mrow == 2, g1,
           jnp.where(mrow == 3, g2,
           jnp.where(mrow == 4, p1,
           jnp.where(mrow == 5, p2, 0.0))))))
    return meta, counts


def _mixer(x, seq_len, ang, win, cw, cb, lng, lnb, gkw, gkb, gng, wout, fng, rw, rb):
    T, D = x.shape
    n = T // TM
    const = lambda shape: pl.BlockSpec(shape, lambda g: (0,) * len(shape))
    tile = lambda w: pl.BlockSpec((TM, w), lambda g: (g, 0))
    return pl.pallas_call(
        functools.partial(_mixer_kernel, tiles_per_seq=seq_len // TM),
        grid=(n,),
        in_specs=[
            tile(D),
            const((1, D)), const((D, D_IN_PAD)), const((CONV_WIDTH, CONV_ROWS, D_CONV)), const((1, D_CONV)),
            const((1, D_CONV)), const((1, D_CONV)), const((LANES, D_GLA_K)), const((1, D_GLA_K)), const((1, HEAD_V)),
            const((D, D)), const((1, D)), const((LANES, D)), const((LANES, ROUTE_ROWS)),
        ],
        out_specs=[tile(D), tile(D),
                   pl.BlockSpec((SUBLANES, ROUTE_ROWS), lambda g: (0, g // ROUTE_TILES)),
                   pl.BlockSpec((ROUTE_ROWS // TD, R_ROWS, LANES), lambda g: (g // ROUTE_TILES, 0, 0))],
        out_shape=[
            jax.ShapeDtypeStruct((T, D), F32),
            jax.ShapeDtypeStruct((T, D), BF16),
            jax.ShapeDtypeStruct((SUBLANES, T), F32),
            jax.ShapeDtypeStruct((T // TD, R_ROWS, LANES), F32),
        ],
        scratch_shapes=[
            pltpu.VMEM((TM + HALO, D_CONV), F32),
            pltpu.VMEM((SUBLANES - 1, SHIFT_ROWS, D_CONV), F32),
            pltpu.VMEM((TM, D_CONV), F32),
            pltpu.VMEM((D_GLA_V, D_GLA_K), F32),
            pltpu.VMEM((ROUTE_ROWS, D), BF16),
        ],
        compiler_params=pltpu.CompilerParams(
            dimension_semantics=("arbitrary",), vmem_limit_bytes=VMEM_LIMIT),
        name="mixer",
    )(x, ang, win, cw, cb, lng, lnb, gkw, gkb, gng, wout, fng, rw, rb)


PIECES = D_MODEL // LANES
RING = 4


def _to_row_tiles(ref, value):
    for c in range(PIECES):
        ref[pl.ds(c, value.shape[0], stride=PIECES), :] = value[:, c * LANES:(c + 1) * LANES]


def _from_row_tiles(ref, n_rows):
    return jnp.concatenate([ref[pl.ds(c, n_rows, stride=PIECES), :] for c in range(PIECES)], axis=1)


def _for_each_run(tile, enabled, n_ref, lo_ref, gs_ref, visit):
    for e in range(N_EXPERTS):
        lo = lo_ref[tile * N_EXPERTS + e]
        gs = gs_ref[tile * N_EXPERTS + e]
        n = n_ref[tile * N_EXPERTS + e]

        @pl.when((n > 0) & enabled)
        def _():
            visit(lo, gs, n)


def _slab(ref, row, rows):
    return ref.at[pl.ds(pl.multiple_of(row * PIECES, PIECES), rows * PIECES)]


def _dispatch_kernel(n_ref, lo_ref, gs_ref, pad0_ref, padn_ref, used_ref, hf_ref, meta_ref, xs_ref,
                     srt, zeros, sems, zsem):
    i = pl.program_id(0)
    last = pl.num_programs(0) - 1
    slot = i % RING
    pos = meta_ref[4:4 + TOP_K, :]
    prow = lax.broadcasted_iota(jnp.int32, (TOP_K * TD, TD), 0).astype(F32)
    perm = jnp.where((prow == pos[0:1, :]) | (prow == pos[1:2, :]), 1.0, 0.0).astype(BF16)
    _to_row_tiles(srt.at[slot], _dot(perm, hf_ref[...]))
    _for_each_run(
        i, True, n_ref, lo_ref, gs_ref,
        lambda lrow, grow, rows: pltpu.make_async_copy(
            _slab(srt.at[slot], lrow, rows), _slab(xs_ref, grow, rows), sems.at[slot]).start())

    def wait_tile(s):
        pltpu.make_async_copy(srt.at[s], xs_ref.at[pl.ds(0, TOP_K * TD * PIECES)], sems.at[s]).wait()

    @pl.when(i >= RING - 1)
    def _():
        wait_tile((i + 1) % RING)

    @pl.when(i == last)
    def _():
        for back in range(RING - 2, -1, -1):
            @pl.when(i >= back)
            def _():
                wait_tile((i - back) % RING)
        zeros[...] = jnp.zeros_like(zeros)
        n_blocks = xs_ref.shape[0] // (MOE_BLOCK * PIECES)

        def pad_copy(e):
            return pltpu.make_async_copy(_slab(zeros, 0, padn_ref[e]), _slab(xs_ref, pad0_ref[e], padn_ref[e]), zsem)

        def tail_copy(j):
            return pltpu.make_async_copy(zeros, _slab(xs_ref, j * MOE_BLOCK, MOE_BLOCK), zsem)

        def for_each_fill(act):
            for e in range(N_EXPERTS):
                @pl.when(padn_ref[e] > 0)
                def _():
                    act(pad_copy(e))

            def tail_block(j, carry):
                act(tail_copy(j))
                return carry
            lax.fori_loop(used_ref[0], n_blocks, tail_block, 0)

        for_each_fill(lambda cp: cp.start())
        for_each_fill(lambda cp: cp.wait())


def _dispatch(tables, pads, hf, meta, cap):
    T, D = hf.shape
    n = T // TD
    grid_spec = pltpu.PrefetchScalarGridSpec(
        num_scalar_prefetch=6,
        grid=(n,),
        in_specs=[
            pl.BlockSpec((TD, D), lambda i, *_: (i, 0)),
            pl.BlockSpec((SUBLANES, TD), lambda i, *_: (0, i)),
        ],
        out_specs=pl.BlockSpec(memory_space=pl.ANY),
        scratch_shapes=[pltpu.VMEM((RING, TOP_K * TD * PIECES, LANES), F32),
                        pltpu.VMEM((MOE_BLOCK * PIECES, LANES), F32),
                        pltpu.SemaphoreType.DMA((RING,)), pltpu.SemaphoreType.DMA(())],
    )
    return pl.pallas_call(
        _dispatch_kernel,
        grid_spec=grid_spec,
        out_shape=jax.ShapeDtypeStruct((cap * PIECES, LANES), F32),
        compiler_params=pltpu.CompilerParams(dimension_semantics=("arbitrary",), vmem_limit_bytes=VMEM_LIMIT),
        name="dispatch",
    )(*tables, *pads, hf, meta)


def _expert_kernel(be_ref, used_ref, nxt_ref, seg_ref, xs_ref, wg_hbm, wu_hbm, wd_hbm, yb_ref,
                   wg_s, wu_s, wd_s, wg_b, wu_b, wd_b, sems, *, layer):
    i = pl.program_id(0)
    e = be_ref[i]
    slot = seg_ref[i] % 2

    def weight_copies(expert, s):
        return [pltpu.make_async_copy(src.at[layer, expert], dst.at[s], sems.at[s])
                for src, dst in ((wg_hbm, wg_s), (wu_hbm, wu_s), (wd_hbm, wd_s))]

    @pl.when(i == 0)
    def _():
        for cp in weight_copies(e, slot):
            cp.start()

    @pl.when((i == 0) | (e != be_ref[jnp.maximum(i - 1, 0)]))
    def _():
        for cp in weight_copies(e, slot):
            cp.wait()
        wg_b[...] = wg_s[slot].astype(BF16)
        wu_b[...] = wu_s[slot].astype(BF16)
        wd_b[...] = wd_s[slot].astype(BF16)

        @pl.when(nxt_ref[i] != e)
        def _():
            for cp in weight_copies(nxt_ref[i], 1 - slot):
                cp.start()

    @pl.when(i < used_ref[0])
    def _():
        xb = _from_row_tiles(xs_ref, MOE_BLOCK).astype(BF16)
        g = _dot(xb, wg_b[...])
        u = _dot(xb, wu_b[...])
        h = (g * jax.nn.sigmoid(g) * u).astype(BF16)
        _to_row_tiles(yb_ref, _dot(h, wd_b[...]))

    @pl.when(i >= used_ref[0])
    def _():
        yb_ref[...] = jnp.zeros_like(yb_ref)


def _experts(block_expert, n_used, xs, wg, wu, wd, layer):
    D = D_MODEL
    n_blocks = xs.shape[0] // (MOE_BLOCK * PIECES)
    changed = jnp.concatenate([jnp.zeros((1,), jnp.int32), (block_expert[1:] != block_expert[:-1]).astype(jnp.int32)])
    segment = jnp.cumsum(changed)
    after = jnp.sum((block_expert[None, :] <= block_expert[:, None]).astype(jnp.int32), axis=1)
    next_expert = jnp.where(after < n_blocks, block_expert[jnp.minimum(after, n_blocks - 1)], block_expert)
    grid_spec = pltpu.PrefetchScalarGridSpec(
        num_scalar_prefetch=4,
        grid=(n_blocks,),
        in_specs=[
            pl.BlockSpec((MOE_BLOCK * PIECES, LANES), lambda i, be, used, *_: (jnp.minimum(i, used[0] - 1), 0)),
            pl.BlockSpec(memory_space=pl.ANY), pl.BlockSpec(memory_space=pl.ANY), pl.BlockSpec(memory_space=pl.ANY),
        ],
        out_specs=pl.BlockSpec((MOE_BLOCK * PIECES, LANES), lambda i, *_: (i, 0)),
        scratch_shapes=[pltpu.VMEM((2, D, D_EXPERT), F32), pltpu.VMEM((2, D, D_EXPERT), F32),
                        pltpu.VMEM((2, D_EXPERT, D), F32),
                        pltpu.VMEM((D, D_EXPERT), BF16), pltpu.VMEM((D, D_EXPERT), BF16),
                        pltpu.VMEM((D_EXPERT, D), BF16), pltpu.SemaphoreType.DMA((2,))],
    )
    return pl.pallas_call(
        functools.partial(_expert_kernel, layer=layer),
        grid_spec=grid_spec,
        out_shape=jax.ShapeDtypeStruct(xs.shape, F32),
        compiler_params=pltpu.CompilerParams(
            dimension_semantics=("arbitrary",), vmem_limit_bytes=VMEM_LIMIT),
        name="experts",
    )(block_expert, n_used, next_expert, segment, xs, wg, wu, wd)


def _combine_kernel(n_ref, lo_ref, gs_ref, x_ref, meta_ref, fg_ref, yb_ref, o_ref, ys, sems, *, final_norm):
    i = pl.program_id(0)
    n_tiles = pl.num_programs(0)
    slot = i % RING

    def fetch(tile, enabled):
        s = tile % RING
        _for_each_run(
            jnp.minimum(tile, n_tiles - 1), enabled & (tile < n_tiles), n_ref, lo_ref, gs_ref,
            lambda lrow, grow, rows: pltpu.make_async_copy(
                _slab(yb_ref, grow, rows), _slab(ys.at[s], lrow, rows), sems.at[s]).start())

    @pl.when(i == 0)
    def _():
        for ahead in range(RING - 1):
            fetch(i + ahead, True)

    fetch(i + RING - 1, True)
    meta = jnp.transpose(jnp.concatenate([meta_ref[...], jnp.zeros((LANES - SUBLANES, TD), F32)], axis=0))
    pcol = lax.broadcasted_iota(jnp.int32, (TD, TOP_K * TD), 1).astype(F32)
    gather = (jnp.where(pcol == meta[:, 4:5], meta[:, 2:3], 0.0)
              + jnp.where(pcol == meta[:, 5:6], meta[:, 3:4], 0.0)).astype(BF16)
    pltpu.make_async_copy(yb_ref.at[pl.ds(0, TOP_K * TD * PIECES)], ys.at[slot], sems.at[slot]).wait()
    out = x_ref[...] + _dot(gather, _from_row_tiles(ys.at[slot], TOP_K * TD).astype(BF16))
    if final_norm:
        out = _rms(out, fg_ref[...])
    o_ref[...] = out


def _combine(tables, x, meta, fg, yb, final_norm):
    T, D = x.shape
    n = T // TD
    grid_spec = pltpu.PrefetchScalarGridSpec(
        num_scalar_prefetch=3,
        grid=(n,),
        in_specs=[
            pl.BlockSpec((TD, D), lambda i, *_: (i, 0)),
            pl.BlockSpec((SUBLANES, TD), lambda i, *_: (0, i)),
            pl.BlockSpec((1, D), lambda i, *_: (0, 0)),
            pl.BlockSpec(memory_space=pl.ANY),
        ],
        out_specs=pl.BlockSpec((TD, D), lambda i, *_: (i, 0)),
        scratch_shapes=[pltpu.VMEM((RING, TOP_K * TD * PIECES, LANES), F32), pltpu.SemaphoreType.DMA((RING,))],
    )
    return pl.pallas_call(
        functools.partial(_combine_kernel, final_norm=final_norm),
        grid_spec=grid_spec,
        out_shape=jax.ShapeDtypeStruct((T, D), F32),
        compiler_params=pltpu.CompilerParams(
            dimension_semantics=("arbitrary",), vmem_limit_bytes=VMEM_LIMIT),
        name="combine",
    )(*tables, x, meta, fg, yb)


def _routing_tables(cnt):
    n_tiles = cnt.shape[0]
    n = cnt[:, EXPERT_ROW0:EXPERT_ROW0 + N_EXPERTS, 0].astype(jnp.int32)
    counts = jnp.sum(n, axis=0)
    padded = (counts + MOE_BLOCK - 1) // MOE_BLOCK * MOE_BLOCK
    pend = jnp.cumsum(padded)
    pstart = pend - padded
    local = jnp.cumsum(n, axis=1) - n
    first = pstart[None, :] + jnp.cumsum(n, axis=0) - n
    n_blocks = (n_tiles * TD * TOP_K + MOE_BLOCK - 1) // MOE_BLOCK + N_EXPERTS
    block_row0 = jnp.arange(n_blocks, dtype=jnp.int32) * MOE_BLOCK
    block_expert = jnp.minimum(
        jnp.sum((pend[None, :] <= block_row0[:, None]).astype(jnp.int32), axis=1), N_EXPERTS - 1)
    pads = (pstart + counts, padded - counts, (pend[-1:] // MOE_BLOCK).astype(jnp.int32))
    return (n.reshape(-1), local.reshape(-1), first.reshape(-1)), pads, block_expert, n_blocks * MOE_BLOCK


def kernel(x, attn_norm_g, w_in, conv_w, conv_b, conv_ln_g, conv_ln_b, gk_w, gk_b, gla_norm_g, w_out, ffn_norm_g,
           router_group_w, router_group_b, router_expert_w, router_expert_b, expert_w_gate, expert_w_up,
           expert_w_down, final_norm_g):
    B, S, D = x.shape
    T = B * S
    depth = w_in.shape[0]
    x = x.reshape(T, D)
    for l in range(depth):
        win = jnp.pad(w_in[l], ((0, 0), (0, D_IN_PAD - w_in.shape[2]))).astype(BF16)
        cw = jnp.broadcast_to(conv_w[l][:, None, :], (CONV_WIDTH, CONV_ROWS, D_CONV))
        gkw = jnp.pad(gk_w[l], ((0, LANES - GATE_RANK), (0, 0))).astype(BF16)
        pad_g = EXPERT_ROW0 - N_GROUPS
        pad_e = LANES - EXPERT_ROW0 - N_EXPERTS
        rwt = jnp.concatenate([router_group_w[l].T, jnp.zeros((pad_g, D), F32), router_expert_w[l].T,
                               jnp.zeros((pad_e, D), F32)], axis=0).astype(BF16)
        rbt = jnp.concatenate([router_group_b[l], jnp.zeros((pad_g,), F32), router_expert_b[l].reshape(-1),
                               jnp.zeros((pad_e,), F32)])
        rbt = jnp.broadcast_to(rbt[:, None], (LANES, ROUTE_ROWS))
        xn, hf, meta, cnt = _mixer(
            x, S, attn_norm_g[l].reshape(1, D), win, cw, conv_b[l].reshape(1, -1), conv_ln_g[l].reshape(1, -1),
            conv_ln_b[l].reshape(1, -1), gkw, gk_b[l].reshape(1, -1), gla_norm_g[l].reshape(1, -1),
            w_out[l].astype(BF16), ffn_norm_g[l].reshape(1, D), rwt, rbt)
        tables, pads, block_expert, cap = _routing_tables(cnt)
        xs = _dispatch(tables, pads, hf, meta, cap)
        yb = _experts(block_expert, pads[2], xs, expert_w_gate, expert_w_up, expert_w_down, l)
        x = _combine(tables, xn, meta, final_norm_g.reshape(1, D), yb, final_norm=(l == depth - 1))
    return x.reshape(B, S, D)
```

```python
import functools

import jax
import jax.numpy as jnp
from jax import lax
from jax.experimental import pallas as pl
from jax.experimental.pallas import tpu as pltpu

D_MODEL = 1024
D_CONV = 512
D_GLA_V = 512
GLA_HEADS = 4
D_GLA_K = 256
HEAD_K = 64
HEAD_V = 128
GATE_RANK = 16
GATE_NORMALIZER = 16.0
CHUNK = 64
CONV_WIDTH = 31
N_GROUPS = 4
EXPERTS_PER_GROUP = 8
N_EXPERTS = 32
TOP_K = 2
D_EXPERT = 512
MOE_BLOCK = 512
EPS = 1e-6

LANES = 128
SUBLANES = 8
D_IN_MAIN = 2 * D_CONV + 2 * D_GLA_K + 2 * D_GLA_V
D_IN_PAD = D_IN_MAIN + LANES
TM = 512
GLA_ROWS = 256
HALO = 32
CONV_ROWS = 32
SHIFT_ROWS = TM + HALO - SUBLANES
TD = 512
EXPERT_ROW0 = 8
R_ROWS = 48
ROUTE_ROWS = 2048
ROUTE_TILES = ROUTE_ROWS // TM
VMEM_LIMIT = 48 * 1024 * 1024

F32 = jnp.float32
BF16 = jnp.bfloat16


def _dot(a, b):
    return jnp.dot(a, b, preferred_element_type=F32)


def _dot_nt(a, b):
    return lax.dot_general(a, b, (((1,), (1,)), ((), ())), preferred_element_type=F32)


def _dot_tn(a, b):
    return lax.dot_general(a, b, (((0,), (0,)), ((), ())), preferred_element_type=F32)


def _split_bf16(x):
    hi = x.astype(BF16)
    lo = (x - hi.astype(F32)).astype(BF16)
    return hi, lo


def _rms(x, g):
    return x * lax.rsqrt(jnp.mean(x * x, axis=-1, keepdims=True) + EPS) * g


def _mixer_kernel(x_ref, ang_ref, win_ref, cw_ref, cb_ref, lng_ref, lnb_ref, gkw_ref, gkb_ref, gng_ref,
                  wout_ref, fng_ref, rwt_ref, rbt_ref,
                  xo_ref, hf_ref, meta_ref, cnt_ref,
                  ubuf, sbuf, ybuf, st_ref, hf_scr, *, tiles_per_seq):
    @pl.when(pl.program_id(0) % tiles_per_seq == 0)
    def _():
        ubuf[0:HALO, :] = jnp.zeros((HALO, D_CONV), F32)
        st_ref[...] = jnp.zeros_like(st_ref)

    x = x_ref[...]
    proj = _dot(_rms(x, ang_ref[...]).astype(BF16), win_ref[...])

    def piece(o0, width, rows=slice(None)):
        return proj[rows, o0:o0 + width]
    o_q = 2 * D_CONV
    o_v = o_q + 2 * D_GLA_K
    conv_out = _conv_branch(piece(0, D_CONV), piece(D_CONV, D_CONV), cw_ref, cb_ref, lng_ref, lnb_ref,
                            ubuf, sbuf, ybuf)
    gla_out = jnp.concatenate([
        _gla_branch(piece(o_q, D_GLA_K, rows), piece(o_q + D_GLA_K, D_GLA_K, rows), piece(o_v, D_GLA_V, rows),
                    piece(o_v + D_GLA_V, D_GLA_V, rows), piece(D_IN_MAIN, LANES, rows),
                    gkw_ref, gkb_ref, gng_ref, st_ref)
        for rows in (slice(r, r + GLA_ROWS) for r in range(0, TM, GLA_ROWS))], axis=0)
    xn = (x + _dot(conv_out.astype(BF16), wout_ref[0:D_CONV, :])
          + _dot(gla_out.astype(BF16), wout_ref[D_CONV:, :]))
    xo_ref[...] = xn
    hf = _rms(xn, fng_ref[...]).astype(BF16)
    hf_ref[...] = hf
    sub = pl.program_id(0) % ROUTE_TILES
    hf_scr[pl.ds(pl.multiple_of(sub * TM, TM), TM), :] = hf

    @pl.when(sub == ROUTE_TILES - 1)
    def _():
        meta_ref[...], cnt_ref[...] = _router(hf_scr[...], rwt_ref, rbt_ref)


def _conv_branch(ua, ug, cw_ref, cb_ref, lng_ref, lnb_ref, ubuf, sbuf, ybuf):
    ubuf[HALO:HALO + TM, :] = ua * jax.nn.sigmoid(ug)
    for r in range(1, SUBLANES):
        sbuf[r - 1] = ubuf[r:r + SHIFT_ROWS, :]
    for c in range(TM // CONV_ROWS):
        acc = jnp.broadcast_to(cb_ref[...], (CONV_ROWS, D_CONV))
        for j in range(CONV_WIDTH):
            off = HALO - (CONV_WIDTH - 1) + j + c * CONV_ROWS
            a0, r = off - off % SUBLANES, off % SUBLANES
            tap = ubuf[a0:a0 + CONV_ROWS, :] if r == 0 else sbuf[r - 1, a0:a0 + CONV_ROWS, :]
            acc = acc + cw_ref[j] * tap
        ybuf[c * CONV_ROWS:(c + 1) * CONV_ROWS, :] = acc
    ubuf[0:HALO, :] = ubuf[TM:TM + HALO, :]
    y = ybuf[...]
    mu = jnp.mean(y, axis=-1, keepdims=True)
    yc = y - mu
    var = jnp.mean(yc * yc, axis=-1, keepdims=True)
    yn = yc * lax.rsqrt(var + EPS) * lng_ref[...] + lnb_ref[...]
    return yn * jax.nn.sigmoid(yn)


def _gla_branch(q, k, v, go, gkl, gkw_ref, gkb_ref, gng_ref, st_ref):
    gk = jax.nn.log_sigmoid(_dot(gkl.astype(BF16), gkw_ref[...]) + gkb_ref[...]) / GATE_NORMALIZER
    row = lax.broadcasted_iota(jnp.int32, (GLA_ROWS, GLA_ROWS), 0)
    col = lax.broadcasted_iota(jnp.int32, (GLA_ROWS, GLA_ROWS), 1)
    same_chunk = (row // CHUNK) == (col // CHUNK)
    causal = same_chunk & (col <= row)
    gk_hi, gk_lo = _split_bf16(gk)
    l_cum = causal.astype(BF16)
    b = _dot(l_cum, gk_hi) + _dot(l_cum, gk_lo)
    b_last = [b[c * CHUNK + CHUNK - 1:(c + 1) * CHUNK, :] for c in range(GLA_ROWS // CHUNK)]
    bl = jnp.concatenate([jnp.broadcast_to(t, (CHUNK, D_GLA_K)) for t in b_last], axis=0)
    qt = (q * (HEAD_K ** -0.5) * jnp.exp(b)).astype(BF16)
    kt = (k * jnp.exp(-b)).astype(BF16)
    ks = (k * jnp.exp(bl - b)).astype(BF16)
    vb = v.astype(BF16)

    klane = lax.broadcasted_iota(jnp.int32, (1, D_GLA_K), 1) // HEAD_K
    o_parts = []
    for h in range(GLA_HEADS):
        qh = jnp.where(klane == h, qt, jnp.zeros_like(qt))
        a = jnp.where(causal, _dot_nt(qh, kt), 0.0).astype(BF16)
        o_parts.append(_dot(a, vb[:, h * HEAD_V:(h + 1) * HEAD_V]))
    o_intra = jnp.concatenate(o_parts, axis=-1)

    srow = lax.broadcasted_iota(jnp.int32, (D_GLA_V, D_GLA_K), 0) // HEAD_V
    scol = lax.broadcasted_iota(jnp.int32, (D_GLA_V, D_GLA_K), 1) // HEAD_K
    head_diag = srow == scol
    o_inter = []
    for c in range(GLA_ROWS // CHUNK):
        r0 = c * CHUNK
        st = st_ref[...]
        o_inter.append(_dot_nt(qt[r0:r0 + CHUNK], st.astype(BF16)))
        ut = _dot_tn(vb[r0:r0 + CHUNK], ks[r0:r0 + CHUNK])
        st_ref[...] = st * jnp.exp(b_last[c]) + jnp.where(head_diag, ut, 0.0)
    o = o_intra + jnp.concatenate(o_inter, axis=0)
    o_n = []
    for h in range(GLA_HEADS):
        oh = o[:, h * HEAD_V:(h + 1) * HEAD_V]
        o_n.append(_rms(oh, gng_ref[...]))
    return jnp.concatenate(o_n, axis=-1) * (go * jax.nn.sigmoid(go))


def _router(hf, rwt_ref, rbt_ref):
    lgt = _dot_nt(rwt_ref[...], hf)[0:R_ROWS] + rbt_ref[0:R_ROWS]
    rowf = lax.broadcasted_iota(jnp.int32, (R_ROWS, ROUTE_ROWS), 0).astype(F32)
    neg = jnp.float32(-1e30)
    big = jnp.float32(R_ROWS)
    is_g = rowf < N_GROUPS
    gl = jnp.where(is_g, lgt, neg)
    gm = jnp.max(gl, axis=0, keepdims=True)
    grp_p = 1.0 / jnp.sum(jnp.where(is_g, jnp.exp(gl - gm), 0.0), axis=0, keepdims=True)
    gidx = jnp.min(jnp.where(is_g & (gl == gm), rowf, big), axis=0, keepdims=True)
    lo = EXPERT_ROW0 + gidx * EXPERTS_PER_GROUP
    in_sel = (rowf >= lo) & (rowf < lo + EXPERTS_PER_GROUP)
    sl = jnp.where(in_sel, lgt, neg)
    sm = jnp.max(sl, axis=0, keepdims=True)
    sz = jnp.sum(jnp.where(in_sel, jnp.exp(sl - sm), 0.0), axis=0, keepdims=True)
    i1 = jnp.min(jnp.where(in_sel & (sl == sm), rowf, big), axis=0, keepdims=True)
    rest = in_sel & (rowf != i1)
    sl2 = jnp.where(rest, sl, neg)
    sm2 = jnp.max(sl2, axis=0, keepdims=True)
    i2 = jnp.min(jnp.where(rest & (sl2 == sm2), rowf, big), axis=0, keepdims=True)
    w1 = 1.0 / sz
    w2 = jnp.exp(sm2 - sm) / sz
    den = w1 + w2
    g1 = grp_p * (w1 / den)
    g2 = grp_p * (w2 / den)
    oh1 = rowf == i1
    oh2 = rowf == i2
    oh_any = jnp.where(oh1 | oh2, 1.0, 0.0).astype(BF16)
    trow = lax.broadcasted_iota(jnp.int32, (TD, TD), 0)
    tcol = lax.broadcasted_iota(jnp.int32, (TD, TD), 1)
    earlier = (trow < tcol).astype(BF16)
    ones_tt = jnp.ones((TD, TD), BF16)
    below = (jnp.where(i1 < rowf, 1.0, 0.0) + jnp.where(i2 < rowf, 1.0, 0.0)).astype(BF16)
    tiles = [slice(t * TD, (t + 1) * TD) for t in range(ROUTE_ROWS // TD)]
    base = jnp.concatenate([_dot(below[:, c], ones_tt) + _dot(oh_any[:, c], earlier) for c in tiles],
                           axis=1)
    p1 = jnp.sum(jnp.where(oh1, base, 0.0), axis=0, keepdims=True)
    p2 = jnp.sum(jnp.where(oh2, base, 0.0), axis=0, keepdims=True)
    counts = jnp.stack([_dot(oh_any[:, c], ones_tt)[:, 0:LANES] for c in tiles])
    mrow = lax.broadcasted_iota(jnp.int32, (SUBLANES, ROUTE_ROWS), 0)
    meta = jnp.where(mrow == 0, i1 - EXPERT_ROW0,
           jnp.where(mrow == 1, i2 - EXPERT_ROW0,
           jnp.where(mrow == 2, g1,
           jnp.where(mrow == 3, g2,
           jnp.where(mrow == 4, p1,
           jnp.where(mrow == 5, p2, 0.0))))))
    return meta, counts


def _mixer(x, seq_len, ang, win, cw, cb, lng, lnb, gkw, gkb, gng, wout, fng, rw, rb):
    T, D = x.shape
    n = T // TM
    const = lambda shape: pl.BlockSpec(shape, lambda g: (0,) * len(shape))
    tile = lambda w: pl.BlockSpec((TM, w), lambda g: (g, 0))
    return pl.pallas_call(
        functools.partial(_mixer_kernel, tiles_per_seq=seq_len // TM),
        grid=(n,),
        in_specs=[
            tile(D),
            const((1, D)), const((D, D_IN_PAD)), const((CONV_WIDTH, CONV_ROWS, D_CONV)), const((1, D_CONV)),
            const((1, D_CONV)), const((1, D_CONV)), const((LANES, D_GLA_K)), const((1, D_GLA_K)), const((1, HEAD_V)),
            const((D, D)), const((1, D)), const((LANES, D)), const((LANES, ROUTE_ROWS)),
        ],
        out_specs=[tile(D), tile(D),
                   pl.BlockSpec((SUBLANES, ROUTE_ROWS), lambda g: (0, g // ROUTE_TILES)),
                   pl.BlockSpec((ROUTE_ROWS // TD, R_ROWS, LANES), lambda g: (g // ROUTE_TILES, 0, 0))],
        out_shape=[
            jax.ShapeDtypeStruct((T, D), F32),
            jax.ShapeDtypeStruct((T, D), BF16),
            jax.ShapeDtypeStruct((SUBLANES, T), F32),
            jax.ShapeDtypeStruct((T // TD, R_ROWS, LANES), F32),
        ],
        scratch_shapes=[
            pltpu.VMEM((TM + HALO, D_CONV), F32),
            pltpu.VMEM((SUBLANES - 1, SHIFT_ROWS, D_CONV), F32),
            pltpu.VMEM((TM, D_CONV), F32),
            pltpu.VMEM((D_GLA_V, D_GLA_K), F32),
            pltpu.VMEM((ROUTE_ROWS, D), BF16),
        ],
        compiler_params=pltpu.CompilerParams(
            dimension_semantics=("arbitrary",), vmem_limit_bytes=VMEM_LIMIT),
        name="mixer",
    )(x, ang, win, cw, cb, lng, lnb, gkw, gkb, gng, wout, fng, rw, rb)


PIECES = D_MODEL // LANES
RING = 4
STEP_BLOCKS = 2
STEP_TILES = 2


def _to_row_tiles(ref, value):
    for c in range(PIECES):
        ref[pl.ds(c, value.shape[0], stride=PIECES), :] = value[:, c * LANES:(c + 1) * LANES]


def _from_row_tiles(ref, n_rows):
    return jnp.concatenate([ref[pl.ds(c, n_rows, stride=PIECES), :] for c in range(PIECES)], axis=1)


def _for_each_run(tile, enabled, n_ref, lo_ref, gs_ref, visit):
    for e in range(N_EXPERTS):
        lo = lo_ref[tile * N_EXPERTS + e]
        gs = gs_ref[tile * N_EXPERTS + e]
        n = n_ref[tile * N_EXPERTS + e]

        @pl.when((n > 0) & enabled)
        def _():
            visit(lo, gs, n)


def _slab(ref, row, rows):
    return ref.at[pl.ds(pl.multiple_of(row * PIECES, PIECES), rows * PIECES)]


def _dispatch_kernel(n_ref, lo_ref, gs_ref, pad0_ref, padn_ref, used_ref, hf_ref, meta_ref, xs_ref,
                     srt, zeros, sems, zsem):
    last = pl.num_programs(0) * STEP_TILES - 1

    def wait_tile(s):
        pltpu.make_async_copy(srt.at[s], xs_ref.at[pl.ds(0, TOP_K * TD * PIECES)], sems.at[s]).wait()

    for k in range(STEP_TILES):
        i = pl.program_id(0) * STEP_TILES + k
        slot = i % RING
        pos = meta_ref[4:4 + TOP_K, k * TD:(k + 1) * TD]
        prow = lax.broadcasted_iota(jnp.int32, (TOP_K * TD, TD), 0).astype(F32)
        perm = jnp.where((prow == pos[0:1, :]) | (prow == pos[1:2, :]), 1.0, 0.0).astype(BF16)
        _to_row_tiles(srt.at[slot], _dot(perm, hf_ref[k * TD:(k + 1) * TD, :]))
        _for_each_run(
            i, True, n_ref, lo_ref, gs_ref,
            lambda lrow, grow, rows, slot=slot: pltpu.make_async_copy(
                _slab(srt.at[slot], lrow, rows), _slab(xs_ref, grow, rows), sems.at[slot]).start())

        @pl.when(i >= RING - 1)
        def _():
            wait_tile((i + 1) % RING)

    @pl.when(i == last)
    def _():
        for back in range(RING - 2, -1, -1):
            @pl.when(i >= back)
            def _():
                wait_tile((i - back) % RING)
        zeros[...] = jnp.zeros_like(zeros)
        n_blocks = xs_ref.shape[0] // (MOE_BLOCK * PIECES)

        def pad_copy(e):
            return pltpu.make_async_copy(_slab(zeros, 0, padn_ref[e]), _slab(xs_ref, pad0_ref[e], padn_ref[e]), zsem)

        def tail_copy(j):
            return pltpu.make_async_copy(zeros, _slab(xs_ref, j * MOE_BLOCK, MOE_BLOCK), zsem)

        def for_each_fill(act):
            for e in range(N_EXPERTS):
                @pl.when(padn_ref[e] > 0)
                def _():
                    act(pad_copy(e))

            def tail_block(j, carry):
                act(tail_copy(j))
                return carry
            lax.fori_loop(used_ref[0], n_blocks, tail_block, 0)

        for_each_fill(lambda cp: cp.start())
        for_each_fill(lambda cp: cp.wait())


def _dispatch(tables, pads, hf, meta, cap):
    T, D = hf.shape
    grid_spec = pltpu.PrefetchScalarGridSpec(
        num_scalar_prefetch=6,
        grid=(T // (STEP_TILES * TD),),
        in_specs=[
            pl.BlockSpec((STEP_TILES * TD, D), lambda i, *_: (i, 0)),
            pl.BlockSpec((SUBLANES, STEP_TILES * TD), lambda i, *_: (0, i)),
        ],
        out_specs=pl.BlockSpec(memory_space=pl.ANY),
        scratch_shapes=[pltpu.VMEM((RING, TOP_K * TD * PIECES, LANES), F32),
                        pltpu.VMEM((MOE_BLOCK * PIECES, LANES), F32),
                        pltpu.SemaphoreType.DMA((RING,)), pltpu.SemaphoreType.DMA(())],
    )
    return pl.pallas_call(
        _dispatch_kernel,
        grid_spec=grid_spec,
        out_shape=jax.ShapeDtypeStruct((cap * PIECES, LANES), F32),
        compiler_params=pltpu.CompilerParams(dimension_semantics=("arbitrary",), vmem_limit_bytes=VMEM_LIMIT),
        name="dispatch",
    )(*tables, *pads, hf, meta)


def _expert_kernel(be_ref, used_ref, nxt_ref, seg_ref, xs_ref, wg_hbm, wu_hbm, wd_hbm, yb_ref,
                   wg_s, wu_s, wd_s, wg_b, wu_b, wd_b, sems, *, layer):
    def weight_copies(expert, s):
        return [pltpu.make_async_copy(src.at[layer, expert], dst.at[s], sems.at[s])
                for src, dst in ((wg_hbm, wg_s), (wu_hbm, wu_s), (wd_hbm, wd_s))]

    for k in range(STEP_BLOCKS):
        i = pl.program_id(0) * STEP_BLOCKS + k
        rows = pl.ds(k * MOE_BLOCK * PIECES, MOE_BLOCK * PIECES)
        e = be_ref[i]
        slot = seg_ref[i] % 2

        @pl.when(i == 0)
        def _():
            for cp in weight_copies(e, slot):
                cp.start()

        @pl.when((i == 0) | (e != be_ref[jnp.maximum(i - 1, 0)]))
        def _():
            for cp in weight_copies(e, slot):
                cp.wait()
            wg_b[...] = wg_s[slot].astype(BF16)
            wu_b[...] = wu_s[slot].astype(BF16)
            wd_b[...] = wd_s[slot].astype(BF16)

            @pl.when(nxt_ref[i] != e)
            def _():
                for cp in weight_copies(nxt_ref[i], 1 - slot):
                    cp.start()

        @pl.when(i < used_ref[0])
        def _():
            xb = _from_row_tiles(xs_ref.at[rows], MOE_BLOCK).astype(BF16)
            g = _dot(xb, wg_b[...])
            u = _dot(xb, wu_b[...])
            h = (g * jax.nn.sigmoid(g) * u).astype(BF16)
            _to_row_tiles(yb_ref.at[rows], _dot(h, wd_b[...]))

        @pl.when(i >= used_ref[0])
        def _():
            yb_ref[rows, :] = jnp.zeros((MOE_BLOCK * PIECES, LANES), F32)


def _experts(block_expert, n_used, xs, wg, wu, wd, layer):
    D = D_MODEL
    n_blocks = xs.shape[0] // (MOE_BLOCK * PIECES)
    changed = jnp.concatenate([jnp.zeros((1,), jnp.int32), (block_expert[1:] != block_expert[:-1]).astype(jnp.int32)])
    segment = jnp.cumsum(changed)
    after = jnp.sum((block_expert[None, :] <= block_expert[:, None]).astype(jnp.int32), axis=1)
    next_expert = jnp.where(after < n_blocks, block_expert[jnp.minimum(after, n_blocks - 1)], block_expert)
    grid_spec = pltpu.PrefetchScalarGridSpec(
        num_scalar_prefetch=4,
        grid=(n_blocks // STEP_BLOCKS,),
        in_specs=[
            pl.BlockSpec((STEP_BLOCKS * MOE_BLOCK * PIECES, LANES),
                         lambda i, be, used, *_: (jnp.minimum(i, (used[0] - 1) // STEP_BLOCKS), 0)),
            pl.BlockSpec(memory_space=pl.ANY), pl.BlockSpec(memory_space=pl.ANY), pl.BlockSpec(memory_space=pl.ANY),
        ],
        out_specs=pl.BlockSpec((STEP_BLOCKS * MOE_BLOCK * PIECES, LANES), lambda i, *_: (i, 0)),
        scratch_shapes=[pltpu.VMEM((2, D, D_EXPERT), F32), pltpu.VMEM((2, D, D_EXPERT), F32),
                        pltpu.VMEM((2, D_EXPERT, D), F32),
                        pltpu.VMEM((D, D_EXPERT), BF16), pltpu.VMEM((D, D_EXPERT), BF16),
                        pltpu.VMEM((D_EXPERT, D), BF16), pltpu.SemaphoreType.DMA((2,))],
    )
    return pl.pallas_call(
        functools.partial(_expert_kernel, layer=layer),
        grid_spec=grid_spec,
        out_shape=jax.ShapeDtypeStruct(xs.shape, F32),
        compiler_params=pltpu.CompilerParams(
            dimension_semantics=("arbitrary",), vmem_limit_bytes=VMEM_LIMIT),
        name="experts",
    )(block_expert, n_used, next_expert, segment, xs, wg, wu, wd)


def _combine_kernel(n_ref, lo_ref, gs_ref, x_ref, meta_ref, fg_ref, yb_ref, o_ref, ys, sems, *, final_norm):
    n_tiles = pl.num_programs(0) * STEP_TILES

    def fetch(tile, enabled):
        s = tile % RING
        _for_each_run(
            jnp.minimum(tile, n_tiles - 1), enabled & (tile < n_tiles), n_ref, lo_ref, gs_ref,
            lambda lrow, grow, rows: pltpu.make_async_copy(
                _slab(yb_ref, grow, rows), _slab(ys.at[s], lrow, rows), sems.at[s]).start())

    for k in range(STEP_TILES):
        i = pl.program_id(0) * STEP_TILES + k
        slot = i % RING
        tokens = slice(k * TD, (k + 1) * TD)

        @pl.when(i == 0)
        def _():
            for ahead in range(RING - 1):
                fetch(i + ahead, True)

        fetch(i + RING - 1, True)
        meta = jnp.transpose(jnp.concatenate([meta_ref[:, tokens], jnp.zeros((LANES - SUBLANES, TD), F32)], axis=0))
        pcol = lax.broadcasted_iota(jnp.int32, (TD, TOP_K * TD), 1).astype(F32)
        gather = (jnp.where(pcol == meta[:, 4:5], meta[:, 2:3], 0.0)
                  + jnp.where(pcol == meta[:, 5:6], meta[:, 3:4], 0.0)).astype(BF16)
        pltpu.make_async_copy(yb_ref.at[pl.ds(0, TOP_K * TD * PIECES)], ys.at[slot], sems.at[slot]).wait()
        out = x_ref[tokens, :] + _dot(gather, _from_row_tiles(ys.at[slot], TOP_K * TD).astype(BF16))
        if final_norm:
            out = _rms(out, fg_ref[...])
        o_ref[tokens, :] = out


def _combine(tables, x, meta, fg, yb, final_norm):
    T, D = x.shape
    grid_spec = pltpu.PrefetchScalarGridSpec(
        num_scalar_prefetch=3,
        grid=(T // (STEP_TILES * TD),),
        in_specs=[
            pl.BlockSpec((STEP_TILES * TD, D), lambda i, *_: (i, 0)),
            pl.BlockSpec((SUBLANES, STEP_TILES * TD), lambda i, *_: (0, i)),
            pl.BlockSpec((1, D), lambda i, *_: (0, 0)),
            pl.BlockSpec(memory_space=pl.ANY),
        ],
        out_specs=pl.BlockSpec((STEP_TILES * TD, D), lambda i, *_: (i, 0)),
        scratch_shapes=[pltpu.VMEM((RING, TOP_K * TD * PIECES, LANES), F32), pltpu.SemaphoreType.DMA((RING,))],
    )
    return pl.pallas_call(
        functools.partial(_combine_kernel, final_norm=final_norm),
        grid_spec=grid_spec,
        out_shape=jax.ShapeDtypeStruct((T, D), F32),
        compiler_params=pltpu.CompilerParams(
            dimension_semantics=("arbitrary",), vmem_limit_bytes=VMEM_LIMIT),
        name="combine",
    )(*tables, x, meta, fg, yb)


def _routing_tables(cnt):
    n_tiles = cnt.shape[0]
    n = cnt[:, EXPERT_ROW0:EXPERT_ROW0 + N_EXPERTS, 0].astype(jnp.int32)
    counts = jnp.sum(n, axis=0)
    padded = (counts + MOE_BLOCK - 1) // MOE_BLOCK * MOE_BLOCK
    pend = jnp.cumsum(padded)
    pstart = pend - padded
    local = jnp.cumsum(n, axis=1) - n
    first = pstart[None, :] + jnp.cumsum(n, axis=0) - n
    n_blocks = (n_tiles * TD * TOP_K + MOE_BLOCK - 1) // MOE_BLOCK + N_EXPERTS
    block_row0 = jnp.arange(n_blocks, dtype=jnp.int32) * MOE_BLOCK
    block_expert = jnp.minimum(
        jnp.sum((pend[None, :] <= block_row0[:, None]).astype(jnp.int32), axis=1), N_EXPERTS - 1)
    pads = (pstart + counts, padded - counts, (pend[-1:] // MOE_BLOCK).astype(jnp.int32))
    return (n.reshape(-1), local.reshape(-1), first.reshape(-1)), pads, block_expert, n_blocks * MOE_BLOCK


def kernel(x, attn_norm_g, w_in, conv_w, conv_b, conv_ln_g, conv_ln_b, gk_w, gk_b, gla_norm_g, w_out, ffn_norm_g,
           router_group_w, router_group_b, router_expert_w, router_expert_b, expert_w_gate, expert_w_up,
           expert_w_down, final_norm_g):
    B, S, D = x.shape
    T = B * S
    depth = w_in.shape[0]
    x = x.reshape(T, D)
    for l in range(depth):
        win = jnp.pad(w_in[l], ((0, 0), (0, D_IN_PAD - w_in.shape[2]))).astype(BF16)
        cw = jnp.broadcast_to(conv_w[l][:, None, :], (CONV_WIDTH, CONV_ROWS, D_CONV))
        gkw = jnp.pad(gk_w[l], ((0, LANES - GATE_RANK), (0, 0))).astype(BF16)
        pad_g = EXPERT_ROW0 - N_GROUPS
        pad_e = LANES - EXPERT_ROW0 - N_EXPERTS
        rwt = jnp.concatenate([router_group_w[l].T, jnp.zeros((pad_g, D), F32), router_expert_w[l].T,
                               jnp.zeros((pad_e, D), F32)], axis=0).astype(BF16)
        rbt = jnp.concatenate([router_group_b[l], jnp.zeros((pad_g,), F32), router_expert_b[l].reshape(-1),
                               jnp.zeros((pad_e,), F32)])
        rbt = jnp.broadcast_to(rbt[:, None], (LANES, ROUTE_ROWS))
        xn, hf, meta, cnt = _mixer(
            x, S, attn_norm_g[l].reshape(1, D), win, cw, conv_b[l].reshape(1, -1), conv_ln_g[l].reshape(1, -1),
            conv_ln_b[l].reshape(1, -1), gkw, gk_b[l].reshape(1, -1), gla_norm_g[l].reshape(1, -1),
            w_out[l].astype(BF16), ffn_norm_g[l].reshape(1, D), rwt, rbt)
        tables, pads, block_expert, cap = _routing_tables(cnt)
        xs = _dispatch(tables, pads, hf, meta, cap)
        yb = _experts(block_expert, pads[2], xs, expert_w_gate, expert_w_up, expert_w_down, l)
        x = _combine(tables, xn, meta, final_norm_g.reshape(1, D), yb, final_norm=(l == depth - 1))
    return x.reshape(B, S, D)
```

```python
import functools

import jax
import jax.numpy as jnp
from jax import lax
from jax.experimental import pallas as pl
from jax.experimental.pallas import tpu as pltpu

D_MODEL = 1024
D_CONV = 512
D_GLA_V = 512
GLA_HEADS = 4
D_GLA_K = 256
HEAD_K = 64
HEAD_V = 128
GATE_RANK = 16
GATE_NORMALIZER = 16.0
CHUNK = 64
CONV_WIDTH = 31
N_GROUPS = 4
EXPERTS_PER_GROUP = 8
N_EXPERTS = 32
TOP_K = 2
D_EXPERT = 512
MOE_BLOCK = 512
EPS = 1e-6

LANES = 128
SUBLANES = 8
D_IN_MAIN = 2 * D_CONV + 2 * D_GLA_K + 2 * D_GLA_V
D_IN_PAD = D_IN_MAIN + LANES
TM = 512
GLA_ROWS = 256
HALO = 32
CONV_ROWS = 32
SHIFT_ROWS = TM + HALO - SUBLANES
TD = 512
EXPERT_ROW0 = 8
R_ROWS = 48
ROUTE_ROWS = 2048
ROUTE_TILES = ROUTE_ROWS // TM
VMEM_LIMIT = 48 * 1024 * 1024

F32 = jnp.float32
BF16 = jnp.bfloat16


def _dot(a, b):
    return jnp.dot(a, b, preferred_element_type=F32)


def _dot_nt(a, b):
    return lax.dot_general(a, b, (((1,), (1,)), ((), ())), preferred_element_type=F32)


def _dot_tn(a, b):
    return lax.dot_general(a, b, (((0,), (0,)), ((), ())), preferred_element_type=F32)


def _split_bf16(x):
    hi = x.astype(BF16)
    lo = (x - hi.astype(F32)).astype(BF16)
    return hi, lo


def _logistic(x):
    return 0.5 * jnp.tanh(0.5 * x) + 0.5


def _rms(x, g):
    return x * lax.rsqrt(jnp.mean(x * x, axis=-1, keepdims=True) + EPS) * g


def _mixer_kernel(x_ref, ang_ref, win_ref, cw_ref, cb_ref, lng_ref, lnb_ref, gkw_ref, gkb_ref, gng_ref,
                  wout_ref, fng_ref, rwt_ref, rbt_ref,
                  xo_ref, hf_ref, meta_ref, cnt_ref,
                  ubuf, sbuf, ybuf, st_ref, hf_scr, *, tiles_per_seq):
    @pl.when(pl.program_id(0) % tiles_per_seq == 0)
    def _():
        ubuf[0:HALO, :] = jnp.zeros((HALO, D_CONV), F32)
        st_ref[...] = jnp.zeros_like(st_ref)

    x = x_ref[...]
    proj = _dot(_rms(x, ang_ref[...]).astype(BF16), win_ref[...])

    def piece(o0, width, rows=slice(None)):
        return proj[rows, o0:o0 + width]
    o_q = 2 * D_CONV
    o_v = o_q + 2 * D_GLA_K
    conv_out = _conv_branch(piece(0, D_CONV), piece(D_CONV, D_CONV), cw_ref, cb_ref, lng_ref, lnb_ref,
                            ubuf, sbuf, ybuf)
    gla_out = jnp.concatenate([
        _gla_branch(piece(o_q, D_GLA_K, rows), piece(o_q + D_GLA_K, D_GLA_K, rows), piece(o_v, D_GLA_V, rows),
                    piece(o_v + D_GLA_V, D_GLA_V, rows), piece(D_IN_MAIN, LANES, rows),
                    gkw_ref, gkb_ref, gng_ref, st_ref)
        for rows in (slice(r, r + GLA_ROWS) for r in range(0, TM, GLA_ROWS))], axis=0)
    xn = (x + _dot(conv_out.astype(BF16), wout_ref[0:D_CONV, :])
          + _dot(gla_out.astype(BF16), wout_ref[D_CONV:, :]))
    xo_ref[...] = xn
    hf = _rms(xn, fng_ref[...]).astype(BF16)
    hf_ref[...] = hf
    sub = pl.program_id(0) % ROUTE_TILES
    hf_scr[pl.ds(pl.multiple_of(sub * TM, TM), TM), :] = hf

    @pl.when(sub == ROUTE_TILES - 1)
    def _():
        meta_ref[...], cnt_ref[...] = _router(hf_scr[...], rwt_ref, rbt_ref)


def _conv_branch(ua, ug, cw_ref, cb_ref, lng_ref, lnb_ref, ubuf, sbuf, ybuf):
    ubuf[HALO:HALO + TM, :] = ua * _logistic(ug)
    for r in range(1, SUBLANES):
        sbuf[r - 1] = ubuf[r:r + SHIFT_ROWS, :]
    for c in range(TM // CONV_ROWS):
        acc = jnp.broadcast_to(cb_ref[...], (CONV_ROWS, D_CONV))
        for j in range(CONV_WIDTH):
            off = HALO - (CONV_WIDTH - 1) + j + c * CONV_ROWS
            a0, r = off - off % SUBLANES, off % SUBLANES
            tap = ubuf[a0:a0 + CONV_ROWS, :] if r == 0 else sbuf[r - 1, a0:a0 + CONV_ROWS, :]
            acc = acc + cw_ref[j] * tap
        ybuf[c * CONV_ROWS:(c + 1) * CONV_ROWS, :] = acc
    ubuf[0:HALO, :] = ubuf[TM:TM + HALO, :]
    y = ybuf[...]
    mu = jnp.mean(y, axis=-1, keepdims=True)
    yc = y - mu
    var = jnp.mean(yc * yc, axis=-1, keepdims=True)
    yn = yc * lax.rsqrt(var + EPS) * lng_ref[...] + lnb_ref[...]
    return yn * _logistic(yn)


def _gla_branch(q, k, v, go, gkl, gkw_ref, gkb_ref, gng_ref, st_ref):
    gk = jax.nn.log_sigmoid(_dot(gkl.astype(BF16), gkw_ref[...]) + gkb_ref[...]) / GATE_NORMALIZER
    row = lax.broadcasted_iota(jnp.int32, (GLA_ROWS, GLA_ROWS), 0)
    col = lax.broadcasted_iota(jnp.int32, (GLA_ROWS, GLA_ROWS), 1)
    same_chunk = (row // CHUNK) == (col // CHUNK)
    causal = same_chunk & (col <= row)
    gk_hi, gk_lo = _split_bf16(gk)
    l_cum = causal.astype(BF16)
    b = _dot(l_cum, gk_hi) + _dot(l_cum, gk_lo)
    b_last = [b[c * CHUNK + CHUNK - 1:(c + 1) * CHUNK, :] for c in range(GLA_ROWS // CHUNK)]
    bl = jnp.concatenate([jnp.broadcast_to(t, (CHUNK, D_GLA_K)) for t in b_last], axis=0)
    qt = (q * (HEAD_K ** -0.5) * jnp.exp(b)).astype(BF16)
    kt = (k * jnp.exp(-b)).astype(BF16)
    ks = (k * jnp.exp(bl - b)).astype(BF16)
    vb = v.astype(BF16)

    klane = lax.broadcasted_iota(jnp.int32, (1, D_GLA_K), 1) // HEAD_K
    o_parts = []
    for h in range(GLA_HEADS):
        qh = jnp.where(klane == h, qt, jnp.zeros_like(qt))
        a = jnp.where(causal, _dot_nt(qh, kt), 0.0).astype(BF16)
        o_parts.append(_dot(a, vb[:, h * HEAD_V:(h + 1) * HEAD_V]))
    o_intra = jnp.concatenate(o_parts, axis=-1)

    srow = lax.broadcasted_iota(jnp.int32, (D_GLA_V, D_GLA_K), 0) // HEAD_V
    scol = lax.broadcasted_iota(jnp.int32, (D_GLA_V, D_GLA_K), 1) // HEAD_K
    head_diag = srow == scol
    o_inter = []
    for c in range(GLA_ROWS // CHUNK):
        r0 = c * CHUNK
        st = st_ref[...]
        o_inter.append(_dot_nt(qt[r0:r0 + CHUNK], st.astype(BF16)))
        ut = _dot_tn(vb[r0:r0 + CHUNK], ks[r0:r0 + CHUNK])
        st_ref[...] = st * jnp.exp(b_last[c]) + jnp.where(head_diag, ut, 0.0)
    o = o_intra + jnp.concatenate(o_inter, axis=0)
    o_n = []
    for h in range(GLA_HEADS):
        oh = o[:, h * HEAD_V:(h + 1) * HEAD_V]
        o_n.append(_rms(oh, gng_ref[...]))
    return jnp.concatenate(o_n, axis=-1) * (go * _logistic(go))


def _router(hf, rwt_ref, rbt_ref):
    lgt = _dot_nt(rwt_ref[...], hf)[0:R_ROWS] + rbt_ref[0:R_ROWS]
    rowf = lax.broadcasted_iota(jnp.int32, (R_ROWS, ROUTE_ROWS), 0).astype(F32)
    neg = jnp.float32(-1e30)
    big = jnp.float32(R_ROWS)
    is_g = rowf < N_GROUPS
    gl = jnp.where(is_g, lgt, neg)
    gm = jnp.max(gl, axis=0, keepdims=True)
    grp_p = 1.0 / jnp.sum(jnp.where(is_g, jnp.exp(gl - gm), 0.0), axis=0, keepdims=True)
    gidx = jnp.min(jnp.where(is_g & (gl == gm), rowf, big), axis=0, keepdims=True)
    lo = EXPERT_ROW0 + gidx * EXPERTS_PER_GROUP
    in_sel = (rowf >= lo) & (rowf < lo + EXPERTS_PER_GROUP)
    sl = jnp.where(in_sel, lgt, neg)
    sm = jnp.max(sl, axis=0, keepdims=True)
    sz = jnp.sum(jnp.where(in_sel, jnp.exp(sl - sm), 0.0), axis=0, keepdims=True)
    i1 = jnp.min(jnp.where(in_sel & (sl == sm), rowf, big), axis=0, keepdims=True)
    rest = in_sel & (rowf != i1)
    sl2 = jnp.where(rest, sl, neg)
    sm2 = jnp.max(sl2, axis=0, keepdims=True)
    i2 = jnp.min(jnp.where(rest & (sl2 == sm2), rowf, big), axis=0, keepdims=True)
    w1 = 1.0 / sz
    w2 = jnp.exp(sm2 - sm) / sz
    den = w1 + w2
    g1 = grp_p * (w1 / den)
    g2 = grp_p * (w2 / den)
    oh1 = rowf == i1
    oh2 = rowf == i2
    oh_any = jnp.where(oh1 | oh2, 1.0, 0.0).astype(BF16)
    trow = lax.broadcasted_iota(jnp.int32, (TD, TD), 0)
    tcol = lax.broadcasted_iota(jnp.int32, (TD, TD), 1)
    earlier = (trow < tcol).astype(BF16)
    ones_tt = jnp.ones((TD, TD), BF16)
    below = (jnp.where(i1 < rowf, 1.0, 0.0) + jnp.where(i2 < rowf, 1.0, 0.0)).astype(BF16)
    tiles = [slice(t * TD, (t + 1) * TD) for t in range(ROUTE_ROWS // TD)]
    base = jnp.concatenate([_dot(below[:, c], ones_tt) + _dot(oh_any[:, c], earlier) for c in tiles],
                           axis=1)
    p1 = jnp.sum(jnp.where(oh1, base, 0.0), axis=0, keepdims=True)
    p2 = jnp.sum(jnp.where(oh2, base, 0.0), axis=0, keepdims=True)
    counts = jnp.stack([_dot(oh_any[:, c], ones_tt)[:, 0:LANES] for c in tiles])
    mrow = lax.broadcasted_iota(jnp.int32, (SUBLANES, ROUTE_ROWS), 0)
    meta = jnp.where(mrow == 0, i1 - EXPERT_ROW0,
           jnp.where(mrow == 1, i2 - EXPERT_ROW0,
           jnp.where(mrow == 2, g1,
           jnp.where(mrow == 3, g2,
           jnp.where(mrow == 4, p1,
           jnp.where(mrow == 5, p2, 0.0))))))
    return meta, counts


def _mixer(x, seq_len, ang, win, cw, cb, lng, lnb, gkw, gkb, gng, wout, fng, rw, rb):
    T, D = x.shape
    n = T // TM
    const = lambda shape: pl.BlockSpec(shape, lambda g: (0,) * len(shape))
    tile = lambda w: pl.BlockSpec((TM, w), lambda g: (g, 0))
    return pl.pallas_call(
        functools.partial(_mixer_kernel, tiles_per_seq=seq_len // TM),
        grid=(n,),
        in_specs=[
            tile(D),
            const((1, D)), const((D, D_IN_PAD)), const((CONV_WIDTH, CONV_ROWS, D_CONV)), const((1, D_CONV)),
            const((1, D_CONV)), const((1, D_CONV)), const((LANES, D_GLA_K)), const((1, D_GLA_K)), const((1, HEAD_V)),
            const((D, D)), const((1, D)), const((LANES, D)), const((LANES, ROUTE_ROWS)),
        ],
        out_specs=[tile(D), tile(D),
                   pl.BlockSpec((SUBLANES, ROUTE_ROWS), lambda g: (0, g // ROUTE_TILES)),
                   pl.BlockSpec((ROUTE_ROWS // TD, R_ROWS, LANES), lambda g: (g // ROUTE_TILES, 0, 0))],
        out_shape=[
            jax.ShapeDtypeStruct((T, D), F32),
            jax.ShapeDtypeStruct((T, D), BF16),
            jax.ShapeDtypeStruct((SUBLANES, T), F32),
            jax.ShapeDtypeStruct((T // TD, R_ROWS, LANES), F32),
        ],
        scratch_shapes=[
            pltpu.VMEM((TM + HALO, D_CONV), F32),
            pltpu.VMEM((SUBLANES - 1, SHIFT_ROWS, D_CONV), F32),
            pltpu.VMEM((TM, D_CONV), F32),
            pltpu.VMEM((D_GLA_V, D_GLA_K), F32),
            pltpu.VMEM((ROUTE_ROWS, D), BF16),
        ],
        compiler_params=pltpu.CompilerParams(
            dimension_semantics=("arbitrary",), vmem_limit_bytes=VMEM_LIMIT),
        name="mixer",
    )(x, ang, win, cw, cb, lng, lnb, gkw, gkb, gng, wout, fng, rw, rb)


PIECES = D_MODEL // LANES
RING = 4
STEP_BLOCKS = 2
STEP_TILES = 2
DISPATCH_TILES = 4


def _to_row_tiles(ref, value):
    for c in range(PIECES):
        ref[pl.ds(c, value.shape[0], stride=PIECES), :] = value[:, c * LANES:(c + 1) * LANES]


def _from_row_tiles(ref, n_rows):
    return jnp.concatenate([ref[pl.ds(c, n_rows, stride=PIECES), :] for c in range(PIECES)], axis=1)


def _for_each_run(tile, enabled, n_ref, lo_ref, gs_ref, visit):
    for e in range(N_EXPERTS):
        lo = lo_ref[tile * N_EXPERTS + e]
        gs = gs_ref[tile * N_EXPERTS + e]
        n = n_ref[tile * N_EXPERTS + e]

        @pl.when((n > 0) & enabled)
        def _():
            visit(lo, gs, n)


def _slab(ref, row, rows):
    return ref.at[pl.ds(pl.multiple_of(row * PIECES, PIECES), rows * PIECES)]


def _dispatch_kernel(n_ref, lo_ref, gs_ref, pad0_ref, padn_ref, used_ref, hf_ref, meta_ref, xs_ref,
                     srt, zeros, sems, zsem):
    last = pl.num_programs(0) * DISPATCH_TILES - 1

    def wait_tile(s):
        pltpu.make_async_copy(srt.at[s], xs_ref.at[pl.ds(0, TOP_K * TD * PIECES)], sems.at[s]).wait()

    for k in range(DISPATCH_TILES):
        i = pl.program_id(0) * DISPATCH_TILES + k
        slot = i % RING
        pos = meta_ref[4:4 + TOP_K, k * TD:(k + 1) * TD]
        prow = lax.broadcasted_iota(jnp.int32, (TOP_K * TD, TD), 0).astype(F32)
        perm = jnp.where((prow == pos[0:1, :]) | (prow == pos[1:2, :]), 1.0, 0.0).astype(BF16)
        _to_row_tiles(srt.at[slot], _dot(perm, hf_ref[k * TD:(k + 1) * TD, :]))
        _for_each_run(
            i, True, n_ref, lo_ref, gs_ref,
            lambda lrow, grow, rows, slot=slot: pltpu.make_async_copy(
                _slab(srt.at[slot], lrow, rows), _slab(xs_ref, grow, rows), sems.at[slot]).start())

        @pl.when(i >= RING - 1)
        def _():
            wait_tile((i + 1) % RING)

    @pl.when(i == last)
    def _():
        for back in range(RING - 2, -1, -1):
            @pl.when(i >= back)
            def _():
                wait_tile((i - back) % RING)
        zeros[...] = jnp.zeros_like(zeros)
        n_blocks = xs_ref.shape[0] // (MOE_BLOCK * PIECES)

        def pad_copy(e):
            return pltpu.make_async_copy(_slab(zeros, 0, padn_ref[e]), _slab(xs_ref, pad0_ref[e], padn_ref[e]), zsem)

        def tail_copy(j):
            return pltpu.make_async_copy(zeros, _slab(xs_ref, j * MOE_BLOCK, MOE_BLOCK), zsem)

        def for_each_fill(act):
            for e in range(N_EXPERTS):
                @pl.when(padn_ref[e] > 0)
                def _():
                    act(pad_copy(e))

            def tail_block(j, carry):
                act(tail_copy(j))
                return carry
            lax.fori_loop(used_ref[0], n_blocks, tail_block, 0)

        for_each_fill(lambda cp: cp.start())
        for_each_fill(lambda cp: cp.wait())


def _dispatch(tables, pads, hf, meta, cap):
    T, D = hf.shape
    grid_spec = pltpu.PrefetchScalarGridSpec(
        num_scalar_prefetch=6,
        grid=(T // (DISPATCH_TILES * TD),),
        in_specs=[
            pl.BlockSpec((DISPATCH_TILES * TD, D), lambda i, *_: (i, 0)),
            pl.BlockSpec((SUBLANES, DISPATCH_TILES * TD), lambda i, *_: (0, i)),
        ],
        out_specs=pl.BlockSpec(memory_space=pl.ANY),
        scratch_shapes=[pltpu.VMEM((RING, TOP_K * TD * PIECES, LANES), F32),
                        pltpu.VMEM((MOE_BLOCK * PIECES, LANES), F32),
                        pltpu.SemaphoreType.DMA((RING,)), pltpu.SemaphoreType.DMA(())],
    )
    return pl.pallas_call(
        _dispatch_kernel,
        grid_spec=grid_spec,
        out_shape=jax.ShapeDtypeStruct((cap * PIECES, LANES), F32),
        compiler_params=pltpu.CompilerParams(dimension_semantics=("arbitrary",), vmem_limit_bytes=VMEM_LIMIT),
        name="dispatch",
    )(*tables, *pads, hf, meta)


def _expert_kernel(be_ref, used_ref, nxt_ref, seg_ref, xs_ref, wg_hbm, wu_hbm, wd_hbm, yb_ref,
                   wg_s, wu_s, wd_s, wg_b, wu_b, wd_b, sems, *, layer):
    def weight_copies(expert, s):
        return [pltpu.make_async_copy(src.at[layer, expert], dst.at[s], sems.at[s])
                for src, dst in ((wg_hbm, wg_s), (wu_hbm, wu_s), (wd_hbm, wd_s))]

    for k in range(STEP_BLOCKS):
        i = pl.program_id(0) * STEP_BLOCKS + k
        rows = pl.ds(k * MOE_BLOCK * PIECES, MOE_BLOCK * PIECES)
        e = be_ref[i]
        slot = seg_ref[i] % 2

        @pl.when(i == 0)
        def _():
            for cp in weight_copies(e, slot):
                cp.start()

        @pl.when((i == 0) | (e != be_ref[jnp.maximum(i - 1, 0)]))
        def _():
            for cp in weight_copies(e, slot):
                cp.wait()
            wg_b[...] = wg_s[slot].astype(BF16)
            wu_b[...] = wu_s[slot].astype(BF16)
            wd_b[...] = wd_s[slot].astype(BF16)

            @pl.when(nxt_ref[i] != e)
            def _():
                for cp in weight_copies(nxt_ref[i], 1 - slot):
                    cp.start()

        @pl.when(i < used_ref[0])
        def _():
            xb = _from_row_tiles(xs_ref.at[rows], MOE_BLOCK).astype(BF16)
            g = _dot(xb, wg_b[...])
            u = _dot(xb, wu_b[...])
            h = (g * _logistic(g) * u).astype(BF16)
            _to_row_tiles(yb_ref.at[rows], _dot(h, wd_b[...]))

        @pl.when(i >= used_ref[0])
        def _():
            yb_ref[rows, :] = jnp.zeros((MOE_BLOCK * PIECES, LANES), F32)


def _experts(block_expert, n_used, xs, wg, wu, wd, layer):
    D = D_MODEL
    n_blocks = xs.shape[0] // (MOE_BLOCK * PIECES)
    changed = jnp.concatenate([jnp.zeros((1,), jnp.int32), (block_expert[1:] != block_expert[:-1]).astype(jnp.int32)])
    segment = jnp.cumsum(changed)
    after = jnp.sum((block_expert[None, :] <= block_expert[:, None]).astype(jnp.int32), axis=1)
    next_expert = jnp.where(after < n_blocks, block_expert[jnp.minimum(after, n_blocks - 1)], block_expert)
    grid_spec = pltpu.PrefetchScalarGridSpec(
        num_scalar_prefetch=4,
        grid=(n_blocks // STEP_BLOCKS,),
        in_specs=[
            pl.BlockSpec((STEP_BLOCKS * MOE_BLOCK * PIECES, LANES),
                         lambda i, be, used, *_: (jnp.minimum(i, (used[0] - 1) // STEP_BLOCKS), 0)),
            pl.BlockSpec(memory_space=pl.ANY), pl.BlockSpec(memory_space=pl.ANY), pl.BlockSpec(memory_space=pl.ANY),
        ],
        out_specs=pl.BlockSpec((STEP_BLOCKS * MOE_BLOCK * PIECES, LANES), lambda i, *_: (i, 0)),
        scratch_shapes=[pltpu.VMEM((2, D, D_EXPERT), F32), pltpu.VMEM((2, D, D_EXPERT), F32),
                        pltpu.VMEM((2, D_EXPERT, D), F32),
                        pltpu.VMEM((D, D_EXPERT), BF16), pltpu.VMEM((D, D_EXPERT), BF16),
                        pltpu.VMEM((D_EXPERT, D), BF16), pltpu.SemaphoreType.DMA((2,))],
    )
    return pl.pallas_call(
        functools.partial(_expert_kernel, layer=layer),
        grid_spec=grid_spec,
        out_shape=jax.ShapeDtypeStruct(xs.shape, F32),
        compiler_params=pltpu.CompilerParams(
            dimension_semantics=("arbitrary",), vmem_limit_bytes=VMEM_LIMIT),
        name="experts",
    )(block_expert, n_used, next_expert, segment, xs, wg, wu, wd)


def _combine_kernel(n_ref, lo_ref, gs_ref, x_ref, meta_ref, fg_ref, yb_ref, o_ref, ys, sems, *, final_norm):
    n_tiles = pl.num_programs(0) * STEP_TILES

    def fetch(tile, enabled):
        s = tile % RING
        _for_each_run(
            jnp.minimum(tile, n_tiles - 1), enabled & (tile < n_tiles), n_ref, lo_ref, gs_ref,
            lambda lrow, grow, rows: pltpu.make_async_copy(
                _slab(yb_ref, grow, rows), _slab(ys.at[s], lrow, rows), sems.at[s]).start())

    for k in range(STEP_TILES):
        i = pl.program_id(0) * STEP_TILES + k
        slot = i % RING
        tokens = slice(k * TD, (k + 1) * TD)

        @pl.when(i == 0)
        def _():
            for ahead in range(RING - 1):
                fetch(i + ahead, True)

        fetch(i + RING - 1, True)
        meta = jnp.transpose(jnp.concatenate([meta_ref[:, tokens], jnp.zeros((LANES - SUBLANES, TD), F32)], axis=0))
        pcol = lax.broadcasted_iota(jnp.int32, (TD, TOP_K * TD), 1).astype(F32)
        gather = (jnp.where(pcol == meta[:, 4:5], meta[:, 2:3], 0.0)
                  + jnp.where(pcol == meta[:, 5:6], meta[:, 3:4], 0.0)).astype(BF16)
        pltpu.make_async_copy(yb_ref.at[pl.ds(0, TOP_K * TD * PIECES)], ys.at[slot], sems.at[slot]).wait()
        out = x_ref[tokens, :] + _dot(gather, _from_row_tiles(ys.at[slot], TOP_K * TD).astype(BF16))
        if final_norm:
            out = _rms(out, fg_ref[...])
        o_ref[tokens, :] = out


def _combine(tables, x, meta, fg, yb, final_norm):
    T, D = x.shape
    grid_spec = pltpu.PrefetchScalarGridSpec(
        num_scalar_prefetch=3,
        grid=(T // (STEP_TILES * TD),),
        in_specs=[
            pl.BlockSpec((STEP_TILES * TD, D), lambda i, *_: (i, 0)),
            pl.BlockSpec((SUBLANES, STEP_TILES * TD), lambda i, *_: (0, i)),
            pl.BlockSpec((1, D), lambda i, *_: (0, 0)),
            pl.BlockSpec(memory_space=pl.ANY),
        ],
        out_specs=pl.BlockSpec((STEP_TILES * TD, D), lambda i, *_: (i, 0)),
        scratch_shapes=[pltpu.VMEM((RING, TOP_K * TD * PIECES, LANES), F32), pltpu.SemaphoreType.DMA((RING,))],
    )
    return pl.pallas_call(
        functools.partial(_combine_kernel, final_norm=final_norm),
        grid_spec=grid_spec,
        out_shape=jax.ShapeDtypeStruct((T, D), F32),
        compiler_params=pltpu.CompilerParams(
            dimension_semantics=("arbitrary",), vmem_limit_bytes=VMEM_LIMIT),
        name="combine",
    )(*tables, x, meta, fg, yb)


def _routing_tables(cnt):
    n_tiles = cnt.shape[0]
    n = cnt[:, EXPERT_ROW0:EXPERT_ROW0 + N_EXPERTS, 0].astype(jnp.int32)
    counts = jnp.sum(n, axis=0)
    padded = (counts + MOE_BLOCK - 1) // MOE_BLOCK * MOE_BLOCK
    pend = jnp.cumsum(padded)
    pstart = pend - padded
    local = jnp.cumsum(n, axis=1) - n
    first = pstart[None, :] + jnp.cumsum(n, axis=0) - n
    n_blocks = (n_tiles * TD * TOP_K + MOE_BLOCK - 1) // MOE_BLOCK + N_EXPERTS
    block_row0 = jnp.arange(n_blocks, dtype=jnp.int32) * MOE_BLOCK
    block_expert = jnp.minimum(
        jnp.sum((pend[None, :] <= block_row0[:, None]).astype(jnp.int32), axis=1), N_EXPERTS - 1)
    pads = (pstart + counts, padded - counts, (pend[-1:] // MOE_BLOCK).astype(jnp.int32))
    return (n.reshape(-1), local.reshape(-1), first.reshape(-1)), pads, block_expert, n_blocks * MOE_BLOCK


def kernel(x, attn_norm_g, w_in, conv_w, conv_b, conv_ln_g, conv_ln_b, gk_w, gk_b, gla_norm_g, w_out, ffn_norm_g,
           router_group_w, router_group_b, router_expert_w, router_expert_b, expert_w_gate, expert_w_up,
           expert_w_down, final_norm_g):
    B, S, D = x.shape
    T = B * S
    depth = w_in.shape[0]
    x = x.reshape(T, D)
    for l in range(depth):
        win = jnp.pad(w_in[l], ((0, 0), (0, D_IN_PAD - w_in.shape[2]))).astype(BF16)
        cw = jnp.broadcast_to(conv_w[l][:, None, :], (CONV_WIDTH, CONV_ROWS, D_CONV))
        gkw = jnp.pad(gk_w[l], ((0, LANES - GATE_RANK), (0, 0))).astype(BF16)
        pad_g = EXPERT_ROW0 - N_GROUPS
        pad_e = LANES - EXPERT_ROW0 - N_EXPERTS
        rwt = jnp.concatenate([router_group_w[l].T, jnp.zeros((pad_g, D), F32), router_expert_w[l].T,
                               jnp.zeros((pad_e, D), F32)], axis=0).astype(BF16)
        rbt = jnp.concatenate([router_group_b[l], jnp.zeros((pad_g,), F32), router_expert_b[l].reshape(-1),
                               jnp.zeros((pad_e,), F32)])
        rbt = jnp.broadcast_to(rbt[:, None], (LANES, ROUTE_ROWS))
        xn, hf, meta, cnt = _mixer(
            x, S, attn_norm_g[l].reshape(1, D), win, cw, conv_b[l].reshape(1, -1), conv_ln_g[l].reshape(1, -1),
            conv_ln_b[l].reshape(1, -1), gkw, gk_b[l].reshape(1, -1), gla_norm_g[l].reshape(1, -1),
            w_out[l].astype(BF16), ffn_norm_g[l].reshape(1, D), rwt, rbt)
        tables, pads, block_expert, cap = _routing_tables(cnt)
        xs = _dispatch(tables, pads, hf, meta, cap)
        yb = _experts(block_expert, pads[2], xs, expert_w_gate, expert_w_up, expert_w_down, l)
        x = _combine(tables, xn, meta, final_norm_g.reshape(1, D), yb, final_norm=(l == depth - 1))
    return x.reshape(B, S, D)
```

```python
import functools

import jax
import jax.numpy as jnp
from jax import lax
from jax.experimental import pallas as pl
from jax.experimental.pallas import tpu as pltpu

D_MODEL = 1024
D_CONV = 512
D_GLA_V = 512
GLA_HEADS = 4
D_GLA_K = 256
HEAD_K = 64
HEAD_V = 128
GATE_RANK = 16
GATE_NORMALIZER = 16.0
CHUNK = 64
CONV_WIDTH = 31
N_GROUPS = 4
EXPERTS_PER_GROUP = 8
N_EXPERTS = 32
TOP_K = 2
D_EXPERT = 512
MOE_BLOCK = 512
EPS = 1e-6

LANES = 128
SUBLANES = 8
D_IN_MAIN = 2 * D_CONV + 2 * D_GLA_K + 2 * D_GLA_V
D_IN_PAD = D_IN_MAIN + LANES
TM = 512
GLA_ROWS = 256
HALO = 32
CONV_ROWS = 32
SHIFT_ROWS = TM + HALO - SUBLANES
TD = 512
EXPERT_ROW0 = 8
R_ROWS = 48
ROUTE_ROWS = 2048
ROUTE_TILES = ROUTE_ROWS // TM
VMEM_LIMIT = 48 * 1024 * 1024

F32 = jnp.float32
BF16 = jnp.bfloat16


def _dot(a, b):
    return jnp.dot(a, b, preferred_element_type=F32)


def _dot_nt(a, b):
    return lax.dot_general(a, b, (((1,), (1,)), ((), ())), preferred_element_type=F32)


def _dot_tn(a, b):
    return lax.dot_general(a, b, (((0,), (0,)), ((), ())), preferred_element_type=F32)


def _split_bf16(x):
    hi = x.astype(BF16)
    lo = (x - hi.astype(F32)).astype(BF16)
    return hi, lo


def _logistic(x):
    return 0.5 * jnp.tanh(0.5 * x) + 0.5


def _rms(x, g):
    return x * lax.rsqrt(jnp.mean(x * x, axis=-1, keepdims=True) + EPS) * g


def _mixer_kernel(x_ref, ang_ref, win_ref, cw_ref, cb_ref, lng_ref, lnb_ref, gkw_ref, gkb_ref, gng_ref,
                  wout_ref, fng_ref, rwt_ref, rbt_ref,
                  xo_ref, hf_ref, meta_ref, cnt_ref,
                  ubuf, sbuf, ybuf, st_ref, hf_scr, *, tiles_per_seq):
    @pl.when(pl.program_id(0) % tiles_per_seq == 0)
    def _():
        ubuf[0:HALO, :] = jnp.zeros((HALO, D_CONV), F32)
        st_ref[...] = jnp.zeros_like(st_ref)

    x = x_ref[...]
    proj = _dot(_rms(x, ang_ref[...]).astype(BF16), win_ref[...])

    def piece(o0, width, rows=slice(None)):
        return proj[rows, o0:o0 + width]
    o_q = 2 * D_CONV
    o_v = o_q + 2 * D_GLA_K
    conv_out = _conv_branch(piece(0, D_CONV), piece(D_CONV, D_CONV), cw_ref, cb_ref, lng_ref, lnb_ref,
                            ubuf, sbuf, ybuf)
    gla_out = jnp.concatenate([
        _gla_branch(piece(o_q, D_GLA_K, rows), piece(o_q + D_GLA_K, D_GLA_K, rows), piece(o_v, D_GLA_V, rows),
                    piece(o_v + D_GLA_V, D_GLA_V, rows), piece(D_IN_MAIN, LANES, rows),
                    gkw_ref, gkb_ref, gng_ref, st_ref)
        for rows in (slice(r, r + GLA_ROWS) for r in range(0, TM, GLA_ROWS))], axis=0)
    xn = (x + _dot(conv_out.astype(BF16), wout_ref[0:D_CONV, :])
          + _dot(gla_out.astype(BF16), wout_ref[D_CONV:, :]))
    xo_ref[...] = xn
    hf = _rms(xn, fng_ref[...]).astype(BF16)
    hf_ref[...] = hf
    sub = pl.program_id(0) % ROUTE_TILES
    hf_scr[pl.ds(pl.multiple_of(sub * TM, TM), TM), :] = hf

    @pl.when(sub == ROUTE_TILES - 1)
    def _():
        meta_ref[...], cnt_ref[...] = _router(hf_scr[...], rwt_ref, rbt_ref)


def _conv_branch(ua, ug, cw_ref, cb_ref, lng_ref, lnb_ref, ubuf, sbuf, ybuf):
    ubuf[HALO:HALO + TM, :] = ua * _logistic(ug)
    for r in range(1, SUBLANES):
        sbuf[r - 1] = ubuf[r:r + SHIFT_ROWS, :]
    for c in range(TM // CONV_ROWS):
        acc = jnp.broadcast_to(cb_ref[...], (CONV_ROWS, D_CONV))
        for j in range(CONV_WIDTH):
            off = HALO - (CONV_WIDTH - 1) + j + c * CONV_ROWS
            a0, r = off - off % SUBLANES, off % SUBLANES
            tap = ubuf[a0:a0 + CONV_ROWS, :] if r == 0 else sbuf[r - 1, a0:a0 + CONV_ROWS, :]
            acc = acc + cw_ref[j] * tap
        ybuf[c * CONV_ROWS:(c + 1) * CONV_ROWS, :] = acc
    ubuf[0:HALO, :] = ubuf[TM:TM + HALO, :]
    y = ybuf[...]
    mu = jnp.mean(y, axis=-1, keepdims=True)
    yc = y - mu
    var = jnp.mean(yc * yc, axis=-1, keepdims=True)
    yn = yc * lax.rsqrt(var + EPS) * lng_ref[...] + lnb_ref[...]
    return yn * _logistic(yn)


def _gla_branch(q, k, v, go, gkl, gkw_ref, gkb_ref, gng_ref, st_ref):
    gk = jax.nn.log_sigmoid(_dot(gkl.astype(BF16), gkw_ref[...]) + gkb_ref[...]) / GATE_NORMALIZER
    row = lax.broadcasted_iota(jnp.int32, (GLA_ROWS, GLA_ROWS), 0)
    col = lax.broadcasted_iota(jnp.int32, (GLA_ROWS, GLA_ROWS), 1)
    same_chunk = (row // CHUNK) == (col // CHUNK)
    causal = same_chunk & (col <= row)
    gk_hi, gk_lo = _split_bf16(gk)
    l_cum = causal.astype(BF16)
    b = _dot(l_cum, gk_hi) + _dot(l_cum, gk_lo)
    b_last = [b[c * CHUNK + CHUNK - 1:(c + 1) * CHUNK, :] for c in range(GLA_ROWS // CHUNK)]
    bl = jnp.concatenate([jnp.broadcast_to(t, (CHUNK, D_GLA_K)) for t in b_last], axis=0)
    qt = (q * (HEAD_K ** -0.5) * jnp.exp(b)).astype(BF16)
    kt = (k * jnp.exp(-b)).astype(BF16)
    ks = (k * jnp.exp(bl - b)).astype(BF16)
    vb = v.astype(BF16)

    klane = lax.broadcasted_iota(jnp.int32, (1, D_GLA_K), 1) // HEAD_K
    o_parts = []
    for h in range(GLA_HEADS):
        qh = jnp.where(klane == h, qt, jnp.zeros_like(qt))
        a = jnp.where(causal, _dot_nt(qh, kt), 0.0).astype(BF16)
        o_parts.append(_dot(a, vb[:, h * HEAD_V:(h + 1) * HEAD_V]))
    o_intra = jnp.concatenate(o_parts, axis=-1)

    srow = lax.broadcasted_iota(jnp.int32, (D_GLA_V, D_GLA_K), 0) // HEAD_V
    scol = lax.broadcasted_iota(jnp.int32, (D_GLA_V, D_GLA_K), 1) // HEAD_K
    head_diag = srow == scol
    o_inter = []
    for c in range(GLA_ROWS // CHUNK):
        r0 = c * CHUNK
        st = st_ref[...]
        o_inter.append(_dot_nt(qt[r0:r0 + CHUNK], st.astype(BF16)))
        ut = _dot_tn(vb[r0:r0 + CHUNK], ks[r0:r0 + CHUNK])
        st_ref[...] = st * jnp.exp(b_last[c]) + jnp.where(head_diag, ut, 0.0)
    o = o_intra + jnp.concatenate(o_inter, axis=0)
    o_n = []
    for h in range(GLA_HEADS):
        oh = o[:, h * HEAD_V:(h + 1) * HEAD_V]
        o_n.append(_rms(oh, gng_ref[...]))
    return jnp.concatenate(o_n, axis=-1) * (go * _logistic(go))


def _router(hf, rwt_ref, rbt_ref):
    lgt = _dot_nt(rwt_ref[...], hf)[0:R_ROWS] + rbt_ref[0:R_ROWS]
    rowf = lax.broadcasted_iota(jnp.int32, (R_ROWS, ROUTE_ROWS), 0).astype(F32)
    neg = jnp.float32(-1e30)
    big = jnp.float32(R_ROWS)
    is_g = rowf < N_GROUPS
    gl = jnp.where(is_g, lgt, neg)
    gm = jnp.max(gl, axis=0, keepdims=True)
    grp_p = 1.0 / jnp.sum(jnp.where(is_g, jnp.exp(gl - gm), 0.0), axis=0, keepdims=True)
    gidx = jnp.min(jnp.where(is_g & (gl == gm), rowf, big), axis=0, keepdims=True)
    lo = EXPERT_ROW0 + gidx * EXPERTS_PER_GROUP
    in_sel = (rowf >= lo) & (rowf < lo + EXPERTS_PER_GROUP)
    sl = jnp.where(in_sel, lgt, neg)
    sm = jnp.max(sl, axis=0, keepdims=True)
    sz = jnp.sum(jnp.where(in_sel, jnp.exp(sl - sm), 0.0), axis=0, keepdims=True)
    i1 = jnp.min(jnp.where(in_sel & (sl == sm), rowf, big), axis=0, keepdims=True)
    rest = in_sel & (rowf != i1)
    sl2 = jnp.where(rest, sl, neg)
    sm2 = jnp.max(sl2, axis=0, keepdims=True)
    i2 = jnp.min(jnp.where(rest & (sl2 == sm2), rowf, big), axis=0, keepdims=True)
    w1 = 1.0 / sz
    w2 = jnp.exp(sm2 - sm) / sz
    den = w1 + w2
    g1 = grp_p * (w1 / den)
    g2 = grp_p * (w2 / den)
    oh1 = rowf == i1
    oh2 = rowf == i2
    oh_any = jnp.where(oh1 | oh2, 1.0, 0.0).astype(BF16)
    trow = lax.broadcasted_iota(jnp.int32, (TD, TD), 0)
    tcol = lax.broadcasted_iota(jnp.int32, (TD, TD), 1)
    earlier = (trow < tcol).astype(BF16)
    ones_tt = jnp.ones((TD, TD), BF16)
    below = (jnp.where(i1 < rowf, 1.0, 0.0) + jnp.where(i2 < rowf, 1.0, 0.0)).astype(BF16)
    tiles = [slice(t * TD, (t + 1) * TD) for t in range(ROUTE_ROWS // TD)]
    base = jnp.concatenate([_dot(below[:, c], ones_tt) + _dot(oh_any[:, c], earlier) for c in tiles],
                           axis=1)
    p1 = jnp.sum(jnp.where(oh1, base, 0.0), axis=0, keepdims=True)
    p2 = jnp.sum(jnp.where(oh2, base, 0.0), axis=0, keepdims=True)
    counts = jnp.stack([_dot(oh_any[:, c], ones_tt)[:, 0:LANES] for c in tiles])
    mrow = lax.broadcasted_iota(jnp.int32, (SUBLANES, ROUTE_ROWS), 0)
    meta = jnp.where(mrow == 0, i1 - EXPERT_ROW0,
           jnp.where(mrow == 1, i2 - EXPERT_ROW0,
           jnp.where(mrow == 2, g1,
           jnp.where(mrow == 3, g2,
           jnp.where(mrow == 4, p1,
           jnp.where(mrow == 5, p2, 0.0))))))
    return meta, counts


def _mixer(x, seq_len, ang, win, cw, cb, lng, lnb, gkw, gkb, gng, wout, fng, rw, rb):
    T, D = x.shape
    n = T // TM
    const = lambda shape: pl.BlockSpec(shape, lambda g: (0,) * len(shape))
    tile = lambda w: pl.BlockSpec((TM, w), lambda g: (g, 0))
    return pl.pallas_call(
        functools.partial(_mixer_kernel, tiles_per_seq=seq_len // TM),
        grid=(n,),
        in_specs=[
            tile(D),
            const((1, D)), const((D, D_IN_PAD)), const((CONV_WIDTH, CONV_ROWS, D_CONV)), const((1, D_CONV)),
            const((1, D_CONV)), const((1, D_CONV)), const((LANES, D_GLA_K)), const((1, D_GLA_K)), const((1, HEAD_V)),
            const((D, D)), const((1, D)), const((LANES, D)), const((LANES, ROUTE_ROWS)),
        ],
        out_specs=[tile(D), tile(D),
                   pl.BlockSpec((SUBLANES, ROUTE_ROWS), lambda g: (0, g // ROUTE_TILES)),
                   pl.BlockSpec((ROUTE_ROWS // TD, R_ROWS, LANES), lambda g: (g // ROUTE_TILES, 0, 0))],
        out_shape=[
            jax.ShapeDtypeStruct((T, D), F32),
            jax.ShapeDtypeStruct((T, D), BF16),
            jax.ShapeDtypeStruct((SUBLANES, T), F32),
            jax.ShapeDtypeStruct((T // TD, R_ROWS, LANES), F32),
        ],
        scratch_shapes=[
            pltpu.VMEM((TM + HALO, D_CONV), F32),
            pltpu.VMEM((SUBLANES - 1, SHIFT_ROWS, D_CONV), F32),
            pltpu.VMEM((TM, D_CONV), F32),
            pltpu.VMEM((D_GLA_V, D_GLA_K), F32),
            pltpu.VMEM((ROUTE_ROWS, D), BF16),
        ],
        compiler_params=pltpu.CompilerParams(
            dimension_semantics=("arbitrary",), vmem_limit_bytes=VMEM_LIMIT),
        name="mixer",
    )(x, ang, win, cw, cb, lng, lnb, gkw, gkb, gng, wout, fng, rw, rb)


PIECES = D_MODEL // LANES
RING = 4
STEP_BLOCKS = 2
STEP_TILES = 2
DISPATCH_TILES = 4


def _to_row_tiles(ref, value):
    for c in range(PIECES):
        ref[pl.ds(c, value.shape[0], stride=PIECES), :] = value[:, c * LANES:(c + 1) * LANES]


def _from_row_tiles(ref, n_rows):
    return jnp.concatenate([ref[pl.ds(c, n_rows, stride=PIECES), :] for c in range(PIECES)], axis=1)


def _for_each_run(tile, enabled, n_ref, lo_ref, gs_ref, visit):
    for e in range(N_EXPERTS):
        lo = lo_ref[tile * N_EXPERTS + e]
        gs = gs_ref[tile * N_EXPERTS + e]
        n = n_ref[tile * N_EXPERTS + e]

        @pl.when((n > 0) & enabled)
        def _():
            visit(lo, gs, n)


def _slab(ref, row, rows):
    return ref.at[pl.ds(pl.multiple_of(row * PIECES, PIECES), rows * PIECES)]


def _dispatch_kernel(n_ref, lo_ref, gs_ref, pad0_ref, padn_ref, used_ref, hf_ref, meta_ref, xs_ref,
                     srt, zeros, sems, zsem):
    last = pl.num_programs(0) * DISPATCH_TILES - 1

    def wait_tile(s):
        pltpu.make_async_copy(srt.at[s], xs_ref.at[pl.ds(0, TOP_K * TD * PIECES)], sems.at[s]).wait()

    for k in range(DISPATCH_TILES):
        i = pl.program_id(0) * DISPATCH_TILES + k
        slot = i % RING
        pos = meta_ref[4:4 + TOP_K, k * TD:(k + 1) * TD]
        prow = lax.broadcasted_iota(jnp.int32, (TOP_K * TD, TD), 0).astype(F32)
        perm = jnp.where((prow == pos[0:1, :]) | (prow == pos[1:2, :]), 1.0, 0.0).astype(BF16)
        _to_row_tiles(srt.at[slot], _dot(perm, hf_ref[k * TD:(k + 1) * TD, :]))
        _for_each_run(
            i, True, n_ref, lo_ref, gs_ref,
            lambda lrow, grow, rows, slot=slot: pltpu.make_async_copy(
                _slab(srt.at[slot], lrow, rows), _slab(xs_ref, grow, rows), sems.at[slot]).start())

        @pl.when(i >= RING - 1)
        def _():
            wait_tile((i + 1) % RING)

    @pl.when(i == last)
    def _():
        for back in range(RING - 2, -1, -1):
            @pl.when(i >= back)
            def _():
                wait_tile((i - back) % RING)
        zeros[...] = jnp.zeros_like(zeros)
        n_blocks = xs_ref.shape[0] // (MOE_BLOCK * PIECES)

        def pad_copy(e):
            return pltpu.make_async_copy(_slab(zeros, 0, padn_ref[e]), _slab(xs_ref, pad0_ref[e], padn_ref[e]), zsem)

        def tail_copy(j):
            return pltpu.make_async_copy(zeros, _slab(xs_ref, j * MOE_BLOCK, MOE_BLOCK), zsem)

        def for_each_fill(act):
            for e in range(N_EXPERTS):
                @pl.when(padn_ref[e] > 0)
                def _():
                    act(pad_copy(e))

            def tail_block(j, carry):
                act(tail_copy(j))
                return carry
            lax.fori_loop(used_ref[0], n_blocks, tail_block, 0)

        for_each_fill(lambda cp: cp.start())
        for_each_fill(lambda cp: cp.wait())


def _dispatch(tables, pads, hf, meta, cap):
    T, D = hf.shape
    grid_spec = pltpu.PrefetchScalarGridSpec(
        num_scalar_prefetch=6,
        grid=(T // (DISPATCH_TILES * TD),),
        in_specs=[
            pl.BlockSpec((DISPATCH_TILES * TD, D), lambda i, *_: (i, 0)),
            pl.BlockSpec((SUBLANES, DISPATCH_TILES * TD), lambda i, *_: (0, i)),
        ],
        out_specs=pl.BlockSpec(memory_space=pl.ANY),
        scratch_shapes=[pltpu.VMEM((RING, TOP_K * TD * PIECES, LANES), F32),
                        pltpu.VMEM((MOE_BLOCK * PIECES, LANES), F32),
                        pltpu.SemaphoreType.DMA((RING,)), pltpu.SemaphoreType.DMA(())],
    )
    return pl.pallas_call(
        _dispatch_kernel,
        grid_spec=grid_spec,
        out_shape=jax.ShapeDtypeStruct((cap * PIECES, LANES), F32),
        compiler_params=pltpu.CompilerParams(dimension_semantics=("arbitrary",), vmem_limit_bytes=VMEM_LIMIT),
        name="dispatch",
    )(*tables, *pads, hf, meta)


def _expert_kernel(be_ref, used_ref, nxt_ref, seg_ref, valid_ref, xs_ref, wg_hbm, wu_hbm, wd_hbm, yb_ref,
                   wg_s, wu_s, wd_s, wg_b, wu_b, wd_b, sems, *, layer):
    def weight_copies(expert, s):
        return [pltpu.make_async_copy(src.at[layer, expert], dst.at[s], sems.at[s])
                for src, dst in ((wg_hbm, wg_s), (wu_hbm, wu_s), (wd_hbm, wd_s))]

    for k in range(STEP_BLOCKS):
        i = pl.program_id(0) * STEP_BLOCKS + k
        rows = pl.ds(k * MOE_BLOCK * PIECES, MOE_BLOCK * PIECES)
        e = be_ref[i]
        slot = seg_ref[i] % 2

        @pl.when(i == 0)
        def _():
            for cp in weight_copies(e, slot):
                cp.start()

        @pl.when((i == 0) | (e != be_ref[jnp.maximum(i - 1, 0)]))
        def _():
            for cp in weight_copies(e, slot):
                cp.wait()
            wg_b[...] = wg_s[slot].astype(BF16)
            wu_b[...] = wu_s[slot].astype(BF16)
            wd_b[...] = wd_s[slot].astype(BF16)

            @pl.when(nxt_ref[i] != e)
            def _():
                for cp in weight_copies(nxt_ref[i], 1 - slot):
                    cp.start()

        def ffn(n_rows):
            part = pl.ds(k * MOE_BLOCK * PIECES, n_rows * PIECES)
            xb = _from_row_tiles(xs_ref.at[part], n_rows).astype(BF16)
            g = _dot(xb, wg_b[...])
            u = _dot(xb, wu_b[...])
            h = (g * _logistic(g) * u).astype(BF16)
            _to_row_tiles(yb_ref.at[part], _dot(h, wd_b[...]))
            if n_rows < MOE_BLOCK:
                rest = pl.ds((k * MOE_BLOCK + n_rows) * PIECES, (MOE_BLOCK - n_rows) * PIECES)
                yb_ref[rest, :] = jnp.zeros(((MOE_BLOCK - n_rows) * PIECES, LANES), F32)

        valid = valid_ref[i]

        @pl.when(valid > MOE_BLOCK // 2)
        def _():
            ffn(MOE_BLOCK)

        @pl.when((valid > 0) & (valid <= MOE_BLOCK // 2))
        def _():
            ffn(MOE_BLOCK // 2)

        @pl.when(valid == 0)
        def _():
            yb_ref[rows, :] = jnp.zeros((MOE_BLOCK * PIECES, LANES), F32)


def _experts(block_expert, pads, xs, wg, wu, wd, layer):
    D = D_MODEL
    n_blocks = xs.shape[0] // (MOE_BLOCK * PIECES)
    n_used = pads[2]
    block_row0 = jnp.arange(n_blocks, dtype=jnp.int32) * MOE_BLOCK
    valid = jnp.clip(pads[0][block_expert] - block_row0, 0, MOE_BLOCK).astype(jnp.int32)
    changed = jnp.concatenate([jnp.zeros((1,), jnp.int32), (block_expert[1:] != block_expert[:-1]).astype(jnp.int32)])
    segment = jnp.cumsum(changed)
    after = jnp.sum((block_expert[None, :] <= block_expert[:, None]).astype(jnp.int32), axis=1)
    next_expert = jnp.where(after < n_blocks, block_expert[jnp.minimum(after, n_blocks - 1)], block_expert)
    grid_spec = pltpu.PrefetchScalarGridSpec(
        num_scalar_prefetch=5,
        grid=(n_blocks // STEP_BLOCKS,),
        in_specs=[
            pl.BlockSpec((STEP_BLOCKS * MOE_BLOCK * PIECES, LANES),
                         lambda i, be, used, *_: (jnp.minimum(i, (used[0] - 1) // STEP_BLOCKS), 0)),
            pl.BlockSpec(memory_space=pl.ANY), pl.BlockSpec(memory_space=pl.ANY), pl.BlockSpec(memory_space=pl.ANY),
        ],
        out_specs=pl.BlockSpec((STEP_BLOCKS * MOE_BLOCK * PIECES, LANES), lambda i, *_: (i, 0)),
        scratch_shapes=[pltpu.VMEM((2, D, D_EXPERT), F32), pltpu.VMEM((2, D, D_EXPERT), F32),
                        pltpu.VMEM((2, D_EXPERT, D), F32),
                        pltpu.VMEM((D, D_EXPERT), BF16), pltpu.VMEM((D, D_EXPERT), BF16),
                        pltpu.VMEM((D_EXPERT, D), BF16), pltpu.SemaphoreType.DMA((2,))],
    )
    return pl.pallas_call(
        functools.partial(_expert_kernel, layer=layer),
        grid_spec=grid_spec,
        out_shape=jax.ShapeDtypeStruct(xs.shape, F32),
        compiler_params=pltpu.CompilerParams(
            dimension_semantics=("arbitrary",), vmem_limit_bytes=VMEM_LIMIT),
        name="experts",
    )(block_expert, n_used, next_expert, segment, valid, xs, wg, wu, wd)


def _combine_kernel(n_ref, lo_ref, gs_ref, x_ref, meta_ref, fg_ref, yb_ref, o_ref, ys, sems, *, final_norm):
    n_tiles = pl.num_programs(0) * STEP_TILES

    def fetch(tile, enabled):
        s = tile % RING
        _for_each_run(
            jnp.minimum(tile, n_tiles - 1), enabled & (tile < n_tiles), n_ref, lo_ref, gs_ref,
            lambda lrow, grow, rows: pltpu.make_async_copy(
                _slab(yb_ref, grow, rows), _slab(ys.at[s], lrow, rows), sems.at[s]).start())

    for k in range(STEP_TILES):
        i = pl.program_id(0) * STEP_TILES + k
        slot = i % RING
        tokens = slice(k * TD, (k + 1) * TD)

        @pl.when(i == 0)
        def _():
            for ahead in range(RING - 1):
                fetch(i + ahead, True)

        fetch(i + RING - 1, True)
        meta = jnp.transpose(jnp.concatenate([meta_ref[:, tokens], jnp.zeros((LANES - SUBLANES, TD), F32)], axis=0))
        pcol = lax.broadcasted_iota(jnp.int32, (TD, TOP_K * TD), 1).astype(F32)
        gather = (jnp.where(pcol == meta[:, 4:5], meta[:, 2:3], 0.0)
                  + jnp.where(pcol == meta[:, 5:6], meta[:, 3:4], 0.0)).astype(BF16)
        pltpu.make_async_copy(yb_ref.at[pl.ds(0, TOP_K * TD * PIECES)], ys.at[slot], sems.at[slot]).wait()
        out = x_ref[tokens, :] + _dot(gather, _from_row_tiles(ys.at[slot], TOP_K * TD).astype(BF16))
        if final_norm:
            out = _rms(out, fg_ref[...])
        o_ref[tokens, :] = out


def _combine(tables, x, meta, fg, yb, final_norm):
    T, D = x.shape
    grid_spec = pltpu.PrefetchScalarGridSpec(
        num_scalar_prefetch=3,
        grid=(T // (STEP_TILES * TD),),
        in_specs=[
            pl.BlockSpec((STEP_TILES * TD, D), lambda i, *_: (i, 0)),
            pl.BlockSpec((SUBLANES, STEP_TILES * TD), lambda i, *_: (0, i)),
            pl.BlockSpec((1, D), lambda i, *_: (0, 0)),
            pl.BlockSpec(memory_space=pl.ANY),
        ],
        out_specs=pl.BlockSpec((STEP_TILES * TD, D), lambda i, *_: (i, 0)),
        scratch_shapes=[pltpu.VMEM((RING, TOP_K * TD * PIECES, LANES), F32), pltpu.SemaphoreType.DMA((RING,))],
    )
    return pl.pallas_call(
        functools.partial(_combine_kernel, final_norm=final_norm),
        grid_spec=grid_spec,
        out_shape=jax.ShapeDtypeStruct((T, D), F32),
        compiler_params=pltpu.CompilerParams(
            dimension_semantics=("arbitrary",), vmem_limit_bytes=VMEM_LIMIT),
        name="combine",
    )(*tables, x, meta, fg, yb)


def _routing_tables(cnt):
    n_tiles = cnt.shape[0]
    n = cnt[:, EXPERT_ROW0:EXPERT_ROW0 + N_EXPERTS, 0].astype(jnp.int32)
    counts = jnp.sum(n, axis=0)
    padded = (counts + MOE_BLOCK - 1) // MOE_BLOCK * MOE_BLOCK
    pend = jnp.cumsum(padded)
    pstart = pend - padded
    local = jnp.cumsum(n, axis=1) - n
    first = pstart[None, :] + jnp.cumsum(n, axis=0) - n
    n_blocks = (n_tiles * TD * TOP_K + MOE_BLOCK - 1) // MOE_BLOCK + N_EXPERTS
    block_row0 = jnp.arange(n_blocks, dtype=jnp.int32) * MOE_BLOCK
    block_expert = jnp.minimum(
        jnp.sum((pend[None, :] <= block_row0[:, None]).astype(jnp.int32), axis=1), N_EXPERTS - 1)
    pads = (pstart + counts, padded - counts, (pend[-1:] // MOE_BLOCK).astype(jnp.int32))
    return (n.reshape(-1), local.reshape(-1), first.reshape(-1)), pads, block_expert, n_blocks * MOE_BLOCK


def kernel(x, attn_norm_g, w_in, conv_w, conv_b, conv_ln_g, conv_ln_b, gk_w, gk_b, gla_norm_g, w_out, ffn_norm_g,
           router_group_w, router_group_b, router_expert_w, router_expert_b, expert_w_gate, expert_w_up,
           expert_w_down, final_norm_g):
    B, S, D = x.shape
    T = B * S
    depth = w_in.shape[0]
    x = x.reshape(T, D)
    for l in range(depth):
        win = jnp.pad(w_in[l], ((0, 0), (0, D_IN_PAD - w_in.shape[2]))).astype(BF16)
        cw = jnp.broadcast_to(conv_w[l][:, None, :], (CONV_WIDTH, CONV_ROWS, D_CONV))
        gkw = jnp.pad(gk_w[l], ((0, LANES - GATE_RANK), (0, 0))).astype(BF16)
        pad_g = EXPERT_ROW0 - N_GROUPS
        pad_e = LANES - EXPERT_ROW0 - N_EXPERTS
        rwt = jnp.concatenate([router_group_w[l].T, jnp.zeros((pad_g, D), F32), router_expert_w[l].T,
                               jnp.zeros((pad_e, D), F32)], axis=0).astype(BF16)
        rbt = jnp.concatenate([router_group_b[l], jnp.zeros((pad_g,), F32), router_expert_b[l].reshape(-1),
                               jnp.zeros((pad_e,), F32)])
        rbt = jnp.broadcast_to(rbt[:, None], (LANES, ROUTE_ROWS))
        xn, hf, meta, cnt = _mixer(
            x, S, attn_norm_g[l].reshape(1, D), win, cw, conv_b[l].reshape(1, -1), conv_ln_g[l].reshape(1, -1),
            conv_ln_b[l].reshape(1, -1), gkw, gk_b[l].reshape(1, -1), gla_norm_g[l].reshape(1, -1),
            w_out[l].astype(BF16), ffn_norm_g[l].reshape(1, D), rwt, rbt)
        tables, pads, block_expert, cap = _routing_tables(cnt)
        xs = _dispatch(tables, pads, hf, meta, cap)
        yb = _experts(block_expert, pads, xs, expert_w_gate, expert_w_up, expert_w_down, l)
        x = _combine(tables, xn, meta, final_norm_g.reshape(1, D), yb, final_norm=(l == depth - 1))
    return x.reshape(B, S, D)
```

```python
import functools

import jax
import jax.numpy as jnp
from jax import lax
from jax.experimental import pallas as pl
from jax.experimental.pallas import tpu as pltpu

D_MODEL = 1024
D_CONV = 512
D_GLA_V = 512
GLA_HEADS = 4
D_GLA_K = 256
HEAD_K = 64
HEAD_V = 128
GATE_RANK = 16
GATE_NORMALIZER = 16.0
CHUNK = 64
CONV_WIDTH = 31
N_GROUPS = 4
EXPERTS_PER_GROUP = 8
N_EXPERTS = 32
TOP_K = 2
D_EXPERT = 512
MOE_BLOCK = 512
EPS = 1e-6

LANES = 128
SUBLANES = 8
D_IN_MAIN = 2 * D_CONV + 2 * D_GLA_K + 2 * D_GLA_V
D_IN_PAD = D_IN_MAIN + LANES
TM = 512
GLA_ROWS = 256
HALO = 32
CONV_ROWS = 32
SHIFT_ROWS = TM + HALO - SUBLANES
TD = 512
EXPERT_ROW0 = 8
R_ROWS = 48
ROUTE_ROWS = 2048
ROUTE_TILES = ROUTE_ROWS // TM
VMEM_LIMIT = 48 * 1024 * 1024

F32 = jnp.float32
BF16 = jnp.bfloat16


def _dot(a, b):
    return jnp.dot(a, b, preferred_element_type=F32)


def _dot_nt(a, b):
    return lax.dot_general(a, b, (((1,), (1,)), ((), ())), preferred_element_type=F32)


def _dot_tn(a, b):
    return lax.dot_general(a, b, (((0,), (0,)), ((), ())), preferred_element_type=F32)


def _split_bf16(x):
    hi = x.astype(BF16)
    lo = (x - hi.astype(F32)).astype(BF16)
    return hi, lo


def _logistic(x):
    return 0.5 * jnp.tanh(0.5 * x) + 0.5


def _rms(x, g):
    return x * lax.rsqrt(jnp.mean(x * x, axis=-1, keepdims=True) + EPS) * g


def _mixer_kernel(x_ref, ang_ref, win_ref, cw_ref, cb_ref, lng_ref, lnb_ref, gkw_ref, gkb_ref, gng_ref,
                  wout_ref, fng_ref, rwt_ref, rbt_ref,
                  xo_ref, hf_ref, meta_ref, cnt_ref,
                  ubuf, sbuf, ybuf, st_ref, hf_scr, *, tiles_per_seq):
    @pl.when(pl.program_id(0) % tiles_per_seq == 0)
    def _():
        ubuf[0:HALO, :] = jnp.zeros((HALO, D_CONV), F32)
        st_ref[...] = jnp.zeros_like(st_ref)

    x = x_ref[...]
    proj = _dot(_rms(x, ang_ref[...]).astype(BF16), win_ref[...])

    def piece(o0, width, rows=slice(None)):
        return proj[rows, o0:o0 + width]
    o_q = 2 * D_CONV
    o_v = o_q + 2 * D_GLA_K
    conv_out = _conv_branch(piece(0, D_CONV), piece(D_CONV, D_CONV), cw_ref, cb_ref, lng_ref, lnb_ref,
                            ubuf, sbuf, ybuf)
    gla_out = jnp.concatenate([
        _gla_branch(piece(o_q, D_GLA_K, rows), piece(o_q + D_GLA_K, D_GLA_K, rows), piece(o_v, D_GLA_V, rows),
                    piece(o_v + D_GLA_V, D_GLA_V, rows), piece(D_IN_MAIN, LANES, rows),
                    gkw_ref, gkb_ref, gng_ref, st_ref)
        for rows in (slice(r, r + GLA_ROWS) for r in range(0, TM, GLA_ROWS))], axis=0)
    xn = (x + _dot(conv_out.astype(BF16), wout_ref[0:D_CONV, :])
          + _dot(gla_out.astype(BF16), wout_ref[D_CONV:, :]))
    xo_ref[...] = xn
    hf = _rms(xn, fng_ref[...]).astype(BF16)
    hf_ref[...] = hf
    sub = pl.program_id(0) % ROUTE_TILES
    hf_scr[pl.ds(pl.multiple_of(sub * TM, TM), TM), :] = hf

    @pl.when(sub == ROUTE_TILES - 1)
    def _():
        meta_ref[...], cnt_ref[...] = _router(hf_scr[...], rwt_ref, rbt_ref)


def _conv_branch(ua, ug, cw_ref, cb_ref, lng_ref, lnb_ref, ubuf, sbuf, ybuf):
    ubuf[HALO:HALO + TM, :] = ua * _logistic(ug)
    for r in range(1, SUBLANES):
        sbuf[r - 1] = ubuf[r:r + SHIFT_ROWS, :]
    for c in range(TM // CONV_ROWS):
        acc = jnp.broadcast_to(cb_ref[...], (CONV_ROWS, D_CONV))
        for j in range(CONV_WIDTH):
            off = HALO - (CONV_WIDTH - 1) + j + c * CONV_ROWS
            a0, r = off - off % SUBLANES, off % SUBLANES
            tap = ubuf[a0:a0 + CONV_ROWS, :] if r == 0 else sbuf[r - 1, a0:a0 + CONV_ROWS, :]
            acc = acc + cw_ref[j] * tap
        ybuf[c * CONV_ROWS:(c + 1) * CONV_ROWS, :] = acc
    ubuf[0:HALO, :] = ubuf[TM:TM + HALO, :]
    y = ybuf[...]
    mu = jnp.mean(y, axis=-1, keepdims=True)
    yc = y - mu
    var = jnp.mean(yc * yc, axis=-1, keepdims=True)
    yn = yc * lax.rsqrt(var + EPS) * lng_ref[...] + lnb_ref[...]
    return yn * _logistic(yn)


def _gla_branch(q, k, v, go, gkl, gkw_ref, gkb_ref, gng_ref, st_ref):
    gk = jax.nn.log_sigmoid(_dot(gkl.astype(BF16), gkw_ref[...]) + gkb_ref[...]) / GATE_NORMALIZER
    row = lax.broadcasted_iota(jnp.int32, (GLA_ROWS, GLA_ROWS), 0)
    col = lax.broadcasted_iota(jnp.int32, (GLA_ROWS, GLA_ROWS), 1)
    same_chunk = (row // CHUNK) == (col // CHUNK)
    causal = same_chunk & (col <= row)
    gk_hi, gk_lo = _split_bf16(gk)
    l_cum = causal.astype(BF16)
    b = _dot(l_cum, gk_hi) + _dot(l_cum, gk_lo)
    b_last = [b[c * CHUNK + CHUNK - 1:(c + 1) * CHUNK, :] for c in range(GLA_ROWS // CHUNK)]
    bl = jnp.concatenate([jnp.broadcast_to(t, (CHUNK, D_GLA_K)) for t in b_last], axis=0)
    qt = (q * (HEAD_K ** -0.5) * jnp.exp(b)).astype(BF16)
    kt = (k * jnp.exp(-b)).astype(BF16)
    ks = (k * jnp.exp(bl - b)).astype(BF16)
    vb = v.astype(BF16)

    klane = lax.broadcasted_iota(jnp.int32, (1, D_GLA_K), 1) // HEAD_K
    o_parts = []
    for h in range(GLA_HEADS):
        qh = jnp.where(klane == h, qt, jnp.zeros_like(qt))
        a = jnp.where(causal, _dot_nt(qh, kt), 0.0).astype(BF16)
        o_parts.append(_dot(a, vb[:, h * HEAD_V:(h + 1) * HEAD_V]))
    o_intra = jnp.concatenate(o_parts, axis=-1)

    srow = lax.broadcasted_iota(jnp.int32, (D_GLA_V, D_GLA_K), 0) // HEAD_V
    scol = lax.broadcasted_iota(jnp.int32, (D_GLA_V, D_GLA_K), 1) // HEAD_K
    head_diag = srow == scol
    o_inter = []
    for c in range(GLA_ROWS // CHUNK):
        r0 = c * CHUNK
        st = st_ref[...]
        o_inter.append(_dot_nt(qt[r0:r0 + CHUNK], st.astype(BF16)))
        ut = _dot_tn(vb[r0:r0 + CHUNK], ks[r0:r0 + CHUNK])
        st_ref[...] = st * jnp.exp(b_last[c]) + jnp.where(head_diag, ut, 0.0)
    o = o_intra + jnp.concatenate(o_inter, axis=0)
    o_n = []
    for h in range(GLA_HEADS):
        oh = o[:, h * HEAD_V:(h + 1) * HEAD_V]
        o_n.append(_rms(oh, gng_ref[...]))
    return jnp.concatenate(o_n, axis=-1) * (go * _logistic(go))


def _router(hf, rwt_ref, rbt_ref):
    lgt = _dot_nt(rwt_ref[...], hf)[0:R_ROWS] + rbt_ref[0:R_ROWS]
    rowf = lax.broadcasted_iota(jnp.int32, (R_ROWS, ROUTE_ROWS), 0).astype(F32)
    neg = jnp.float32(-1e30)
    big = jnp.float32(R_ROWS)
    is_g = rowf < N_GROUPS
    gl = jnp.where(is_g, lgt, neg)
    gm = jnp.max(gl, axis=0, keepdims=True)
    grp_p = 1.0 / jnp.sum(jnp.where(is_g, jnp.exp(gl - gm), 0.0), axis=0, keepdims=True)
    gidx = jnp.min(jnp.where(is_g & (gl == gm), rowf, big), axis=0, keepdims=True)
    lo = EXPERT_ROW0 + gidx * EXPERTS_PER_GROUP
    in_sel = (rowf >= lo) & (rowf < lo + EXPERTS_PER_GROUP)
    sl = jnp.where(in_sel, lgt, neg)
    sm = jnp.max(sl, axis=0, keepdims=True)
    sz = jnp.sum(jnp.where(in_sel, jnp.exp(sl - sm), 0.0), axis=0, keepdims=True)
    i1 = jnp.min(jnp.where(in_sel & (sl == sm), rowf, big), axis=0, keepdims=True)
    rest = in_sel & (rowf != i1)
    sl2 = jnp.where(rest, sl, neg)
    sm2 = jnp.max(sl2, axis=0, keepdims=True)
    i2 = jnp.min(jnp.where(rest & (sl2 == sm2), rowf, big), axis=0, keepdims=True)
    w1 = 1.0 / sz
    w2 = jnp.exp(sm2 - sm) / sz
    den = w1 + w2
    g1 = grp_p * (w1 / den)
    g2 = grp_p * (w2 / den)
    oh1 = rowf == i1
    oh2 = rowf == i2
    oh_any = jnp.where(oh1 | oh2, 1.0, 0.0).astype(BF16)
    trow = lax.broadcasted_iota(jnp.int32, (TD, TD), 0)
    tcol = lax.broadcasted_iota(jnp.int32, (TD, TD), 1)
    earlier = (trow < tcol).astype(BF16)
    ones_tt = jnp.ones((TD, TD), BF16)
    below = (jnp.where(i1 < rowf, 1.0, 0.0) + jnp.where(i2 < rowf, 1.0, 0.0)).astype(BF16)
    tiles = [slice(t * TD, (t + 1) * TD) for t in range(ROUTE_ROWS // TD)]
    base = jnp.concatenate([_dot(below[:, c], ones_tt) + _dot(oh_any[:, c], earlier) for c in tiles],
                           axis=1)
    p1 = jnp.sum(jnp.where(oh1, base, 0.0), axis=0, keepdims=True)
    p2 = jnp.sum(jnp.where(oh2, base, 0.0), axis=0, keepdims=True)
    counts = jnp.stack([_dot(oh_any[:, c], ones_tt)[:, 0:LANES] for c in tiles])
    mrow = lax.broadcasted_iota(jnp.int32, (SUBLANES, ROUTE_ROWS), 0)
    meta = jnp.where(mrow == 0, i1 - EXPERT_ROW0,
           jnp.where(mrow == 1, i2 - EXPERT_ROW0,
           jnp.where(mrow == 2, g1,
           jnp.where(mrow == 3, g2,
           jnp.where(mrow == 4, p1,
           jnp.where(mrow == 5, p2, 0.0))))))
    return meta, counts


def _mixer(x, seq_len, ang, win, cw, cb, lng, lnb, gkw, gkb, gng, wout, fng, rw, rb):
    T, D = x.shape
    n = T // TM
    const = lambda shape: pl.BlockSpec(shape, lambda g: (0,) * len(shape))
    tile = lambda w: pl.BlockSpec((TM, w), lambda g: (g, 0))
    return pl.pallas_call(
        functools.partial(_mixer_kernel, tiles_per_seq=seq_len // TM),
        grid=(n,),
        in_specs=[
            tile(D),
            const((1, D)), const((D, D_IN_PAD)), const((CONV_WIDTH, CONV_ROWS, D_CONV)), const((1, D_CONV)),
            const((1, D_CONV)), const((1, D_CONV)), const((LANES, D_GLA_K)), const((1, D_GLA_K)), const((1, HEAD_V)),
            const((D, D)), const((1, D)), const((LANES, D)), const((LANES, ROUTE_ROWS)),
        ],
        out_specs=[tile(D), tile(D),
                   pl.BlockSpec((SUBLANES, ROUTE_ROWS), lambda g: (0, g // ROUTE_TILES)),
                   pl.BlockSpec((ROUTE_ROWS // TD, R_ROWS, LANES), lambda g: (g // ROUTE_TILES, 0, 0))],
        out_shape=[
            jax.ShapeDtypeStruct((T, D), F32),
            jax.ShapeDtypeStruct((T, D), BF16),
            jax.ShapeDtypeStruct((SUBLANES, T), F32),
            jax.ShapeDtypeStruct((T // TD, R_ROWS, LANES), F32),
        ],
        scratch_shapes=[
            pltpu.VMEM((TM + HALO, D_CONV), F32),
            pltpu.VMEM((SUBLANES - 1, SHIFT_ROWS, D_CONV), F32),
            pltpu.VMEM((TM, D_CONV), F32),
            pltpu.VMEM((D_GLA_V, D_GLA_K), F32),
            pltpu.VMEM((ROUTE_ROWS, D), BF16),
        ],
        compiler_params=pltpu.CompilerParams(
            dimension_semantics=("arbitrary",), vmem_limit_bytes=VMEM_LIMIT),
        name="mixer",
    )(x, ang, win, cw, cb, lng, lnb, gkw, gkb, gng, wout, fng, rw, rb)


PIECES = D_MODEL // LANES
RING = 4
STEP_BLOCKS = 2
STEP_TILES = 2
DISPATCH_TILES = 4


def _to_row_tiles(ref, value):
    for c in range(PIECES):
        ref[pl.ds(c, value.shape[0], stride=PIECES), :] = value[:, c * LANES:(c + 1) * LANES]


def _from_row_tiles(ref, n_rows):
    return jnp.concatenate([ref[pl.ds(c, n_rows, stride=PIECES), :] for c in range(PIECES)], axis=1)


def _for_each_run(tile, enabled, n_ref, lo_ref, gs_ref, visit):
    for e in range(N_EXPERTS):
        lo = lo_ref[tile * N_EXPERTS + e]
        gs = gs_ref[tile * N_EXPERTS + e]
        n = n_ref[tile * N_EXPERTS + e]

        @pl.when((n > 0) & enabled)
        def _():
            visit(lo, gs, n)


def _slab(ref, row, rows):
    return ref.at[pl.ds(pl.multiple_of(row * PIECES, PIECES), rows * PIECES)]


def _dispatch_kernel(n_ref, lo_ref, gs_ref, pad0_ref, padn_ref, used_ref, hf_ref, meta_ref, xs_ref,
                     srt, zeros, sems, zsem):
    last = pl.num_programs(0) * DISPATCH_TILES - 1

    def wait_tile(s):
        pltpu.make_async_copy(srt.at[s], xs_ref.at[pl.ds(0, TOP_K * TD * PIECES)], sems.at[s]).wait()

    for k in range(DISPATCH_TILES):
        i = pl.program_id(0) * DISPATCH_TILES + k
        slot = i % RING
        pos = meta_ref[4:4 + TOP_K, k * TD:(k + 1) * TD]
        prow = lax.broadcasted_iota(jnp.int32, (TOP_K * TD, TD), 0).astype(F32)
        perm = jnp.where((prow == pos[0:1, :]) | (prow == pos[1:2, :]), 1.0, 0.0).astype(BF16)
        _to_row_tiles(srt.at[slot], _dot(perm, hf_ref[k * TD:(k + 1) * TD, :]))
        _for_each_run(
            i, True, n_ref, lo_ref, gs_ref,
            lambda lrow, grow, rows, slot=slot: pltpu.make_async_copy(
                _slab(srt.at[slot], lrow, rows), _slab(xs_ref, grow, rows), sems.at[slot]).start())

        @pl.when(i >= RING - 1)
        def _():
            wait_tile((i + 1) % RING)

    @pl.when(i == last)
    def _():
        for back in range(RING - 2, -1, -1):
            @pl.when(i >= back)
            def _():
                wait_tile((i - back) % RING)
        zeros[...] = jnp.zeros_like(zeros)
        n_blocks = xs_ref.shape[0] // (MOE_BLOCK * PIECES)

        def pad_copy(e):
            return pltpu.make_async_copy(_slab(zeros, 0, padn_ref[e]), _slab(xs_ref, pad0_ref[e], padn_ref[e]), zsem)

        def tail_copy(j):
            return pltpu.make_async_copy(zeros, _slab(xs_ref, j * MOE_BLOCK, MOE_BLOCK), zsem)

        def for_each_fill(act):
            for e in range(N_EXPERTS):
                @pl.when(padn_ref[e] > 0)
                def _():
                    act(pad_copy(e))

            def tail_block(j, carry):
                act(tail_copy(j))
                return carry
            lax.fori_loop(used_ref[0], n_blocks, tail_block, 0)

        for_each_fill(lambda cp: cp.start())
        for_each_fill(lambda cp: cp.wait())


def _dispatch(tables, pads, hf, meta, cap):
    T, D = hf.shape
    grid_spec = pltpu.PrefetchScalarGridSpec(
        num_scalar_prefetch=6,
        grid=(T // (DISPATCH_TILES * TD),),
        in_specs=[
            pl.BlockSpec((DISPATCH_TILES * TD, D), lambda i, *_: (i, 0)),
            pl.BlockSpec((SUBLANES, DISPATCH_TILES * TD), lambda i, *_: (0, i)),
        ],
        out_specs=pl.BlockSpec(memory_space=pl.ANY),
        scratch_shapes=[pltpu.VMEM((RING, TOP_K * TD * PIECES, LANES), F32),
                        pltpu.VMEM((MOE_BLOCK * PIECES, LANES), F32),
                        pltpu.SemaphoreType.DMA((RING,)), pltpu.SemaphoreType.DMA(())],
    )
    return pl.pallas_call(
        _dispatch_kernel,
        grid_spec=grid_spec,
        out_shape=jax.ShapeDtypeStruct((cap * PIECES, LANES), F32),
        compiler_params=pltpu.CompilerParams(dimension_semantics=("arbitrary",), vmem_limit_bytes=VMEM_LIMIT),
        name="dispatch",
    )(*tables, *pads, hf, meta)


def _expert_kernel(be_ref, used_ref, nxt_ref, seg_ref, valid_ref, xs_ref, wg_hbm, wu_hbm, wd_hbm, yb_ref,
                   wg_s, wu_s, wd_s, wg_b, wu_b, wd_b, sems, *, layer):
    def weight_copies(expert, s):
        return [pltpu.make_async_copy(src.at[layer, expert], dst.at[s], sems.at[s])
                for src, dst in ((wg_hbm, wg_s), (wu_hbm, wu_s), (wd_hbm, wd_s))]

    for k in range(STEP_BLOCKS):
        i = pl.program_id(0) * STEP_BLOCKS + k
        rows = pl.ds(k * MOE_BLOCK * PIECES, MOE_BLOCK * PIECES)
        e = be_ref[i]
        slot = seg_ref[i] % 2

        @pl.when(i == 0)
        def _():
            for cp in weight_copies(e, slot):
                cp.start()

        @pl.when((i == 0) | (e != be_ref[jnp.maximum(i - 1, 0)]))
        def _():
            for cp in weight_copies(e, slot):
                cp.wait()
            wg_b[...] = wg_s[slot].astype(BF16)
            wu_b[...] = wu_s[slot].astype(BF16)
            wd_b[...] = wd_s[slot].astype(BF16)

            @pl.when(nxt_ref[i] != e)
            def _():
                for cp in weight_copies(nxt_ref[i], 1 - slot):
                    cp.start()

        def ffn(n_rows):
            part = pl.ds(k * MOE_BLOCK * PIECES, n_rows * PIECES)
            xb = _from_row_tiles(xs_ref.at[part], n_rows).astype(BF16)
            g = _dot(xb, wg_b[...])
            u = _dot(xb, wu_b[...])
            h = (g * _logistic(g) * u).astype(BF16)
            _to_row_tiles(yb_ref.at[part], _dot(h, wd_b[...]))
            if n_rows < MOE_BLOCK:
                rest = pl.ds((k * MOE_BLOCK + n_rows) * PIECES, (MOE_BLOCK - n_rows) * PIECES)
                yb_ref[rest, :] = jnp.zeros(((MOE_BLOCK - n_rows) * PIECES, LANES), F32)

        valid = valid_ref[i]

        @pl.when(valid > MOE_BLOCK // 2)
        def _():
            ffn(MOE_BLOCK)

        @pl.when((valid > 0) & (valid <= MOE_BLOCK // 2))
        def _():
            ffn(MOE_BLOCK // 2)

        @pl.when(valid == 0)
        def _():
            yb_ref[rows, :] = jnp.zeros((MOE_BLOCK * PIECES, LANES), F32)


def _experts(block_expert, pads, xs, wg, wu, wd, layer):
    D = D_MODEL
    n_blocks = xs.shape[0] // (MOE_BLOCK * PIECES)
    n_used = pads[2]
    block_row0 = jnp.arange(n_blocks, dtype=jnp.int32) * MOE_BLOCK
    is_expert = block_expert[:, None] == jnp.arange(N_EXPERTS, dtype=jnp.int32)[None, :]
    row_end = jnp.sum(jnp.where(is_expert, pads[0][None, :], 0), axis=1)
    valid = jnp.clip(row_end - block_row0, 0, MOE_BLOCK).astype(jnp.int32)
    changed = jnp.concatenate([jnp.zeros((1,), jnp.int32), (block_expert[1:] != block_expert[:-1]).astype(jnp.int32)])
    segment = jnp.cumsum(changed)
    after = jnp.sum((block_expert[None, :] <= block_expert[:, None]).astype(jnp.int32), axis=1)
    next_expert = jnp.where(after < n_blocks, block_expert[jnp.minimum(after, n_blocks - 1)], block_expert)
    grid_spec = pltpu.PrefetchScalarGridSpec(
        num_scalar_prefetch=5,
        grid=(n_blocks // STEP_BLOCKS,),
        in_specs=[
            pl.BlockSpec((STEP_BLOCKS * MOE_BLOCK * PIECES, LANES),
                         lambda i, be, used, *_: (jnp.minimum(i, (used[0] - 1) // STEP_BLOCKS), 0)),
            pl.BlockSpec(memory_space=pl.ANY), pl.BlockSpec(memory_space=pl.ANY), pl.BlockSpec(memory_space=pl.ANY),
        ],
        out_specs=pl.BlockSpec((STEP_BLOCKS * MOE_BLOCK * PIECES, LANES), lambda i, *_: (i, 0)),
        scratch_shapes=[pltpu.VMEM((2, D, D_EXPERT), F32), pltpu.VMEM((2, D, D_EXPERT), F32),
                        pltpu.VMEM((2, D_EXPERT, D), F32),
                        pltpu.VMEM((D, D_EXPERT), BF16), pltpu.VMEM((D, D_EXPERT), BF16),
                        pltpu.VMEM((D_EXPERT, D), BF16), pltpu.SemaphoreType.DMA((2,))],
    )
    return pl.pallas_call(
        functools.partial(_expert_kernel, layer=layer),
        grid_spec=grid_spec,
        out_shape=jax.ShapeDtypeStruct(xs.shape, F32),
        compiler_params=pltpu.CompilerParams(
            dimension_semantics=("arbitrary",), vmem_limit_bytes=VMEM_LIMIT),
        name="experts",
    )(block_expert, n_used, next_expert, segment, valid, xs, wg, wu, wd)


def _combine_kernel(n_ref, lo_ref, gs_ref, x_ref, meta_ref, fg_ref, yb_ref, o_ref, ys, sems, *, final_norm):
    n_tiles = pl.num_programs(0) * STEP_TILES

    def fetch(tile, enabled):
        s = tile % RING
        _for_each_run(
            jnp.minimum(tile, n_tiles - 1), enabled & (tile < n_tiles), n_ref, lo_ref, gs_ref,
            lambda lrow, grow, rows: pltpu.make_async_copy(
                _slab(yb_ref, grow, rows), _slab(ys.at[s], lrow, rows), sems.at[s]).start())

    for k in range(STEP_TILES):
        i = pl.program_id(0) * STEP_TILES + k
        slot = i % RING
        tokens = slice(k * TD, (k + 1) * TD)

        @pl.when(i == 0)
        def _():
            for ahead in range(RING - 1):
                fetch(i + ahead, True)

        fetch(i + RING - 1, True)
        meta = jnp.transpose(jnp.concatenate([meta_ref[:, tokens], jnp.zeros((LANES - SUBLANES, TD), F32)], axis=0))
        pcol = lax.broadcasted_iota(jnp.int32, (TD, TOP_K * TD), 1).astype(F32)
        gather = (jnp.where(pcol == meta[:, 4:5], meta[:, 2:3], 0.0)
                  + jnp.where(pcol == meta[:, 5:6], meta[:, 3:4], 0.0)).astype(BF16)
        pltpu.make_async_copy(yb_ref.at[pl.ds(0, TOP_K * TD * PIECES)], ys.at[slot], sems.at[slot]).wait()
        out = x_ref[tokens, :] + _dot(gather, _from_row_tiles(ys.at[slot], TOP_K * TD).astype(BF16))
        if final_norm:
            out = _rms(out, fg_ref[...])
        o_ref[tokens, :] = out


def _combine(tables, x, meta, fg, yb, final_norm):
    T, D = x.shape
    grid_spec = pltpu.PrefetchScalarGridSpec(
        num_scalar_prefetch=3,
        grid=(T // (STEP_TILES * TD),),
        in_specs=[
            pl.BlockSpec((STEP_TILES * TD, D), lambda i, *_: (i, 0)),
            pl.BlockSpec((SUBLANES, STEP_TILES * TD), lambda i, *_: (0, i)),
            pl.BlockSpec((1, D), lambda i, *_: (0, 0)),
            pl.BlockSpec(memory_space=pl.ANY),
        ],
        out_specs=pl.BlockSpec((STEP_TILES * TD, D), lambda i, *_: (i, 0)),
        scratch_shapes=[pltpu.VMEM((RING, TOP_K * TD * PIECES, LANES), F32), pltpu.SemaphoreType.DMA((RING,))],
    )
    return pl.pallas_call(
        functools.partial(_combine_kernel, final_norm=final_norm),
        grid_spec=grid_spec,
        out_shape=jax.ShapeDtypeStruct((T, D), F32),
        compiler_params=pltpu.CompilerParams(
            dimension_semantics=("arbitrary",), vmem_limit_bytes=VMEM_LIMIT),
        name="combine",
    )(*tables, x, meta, fg, yb)


def _routing_tables(cnt):
    n_tiles = cnt.shape[0]
    n = cnt[:, EXPERT_ROW0:EXPERT_ROW0 + N_EXPERTS, 0].astype(jnp.int32)
    counts = jnp.sum(n, axis=0)
    padded = (counts + MOE_BLOCK - 1) // MOE_BLOCK * MOE_BLOCK
    pend = jnp.cumsum(padded)
    pstart = pend - padded
    local = jnp.cumsum(n, axis=1) - n
    first = pstart[None, :] + jnp.cumsum(n, axis=0) - n
    n_blocks = (n_tiles * TD * TOP_K + MOE_BLOCK - 1) // MOE_BLOCK + N_EXPERTS
    block_row0 = jnp.arange(n_blocks, dtype=jnp.int32) * MOE_BLOCK
    block_expert = jnp.minimum(
        jnp.sum((pend[None, :] <= block_row0[:, None]).astype(jnp.int32), axis=1), N_EXPERTS - 1)
    pads = (pstart + counts, padded - counts, (pend[-1:] // MOE_BLOCK).astype(jnp.int32))
    return (n.reshape(-1), local.reshape(-1), first.reshape(-1)), pads, block_expert, n_blocks * MOE_BLOCK


def kernel(x, attn_norm_g, w_in, conv_w, conv_b, conv_ln_g, conv_ln_b, gk_w, gk_b, gla_norm_g, w_out, ffn_norm_g,
           router_group_w, router_group_b, router_expert_w, router_expert_b, expert_w_gate, expert_w_up,
           expert_w_down, final_norm_g):
    B, S, D = x.shape
    T = B * S
    depth = w_in.shape[0]
    x = x.reshape(T, D)
    for l in range(depth):
        win = jnp.pad(w_in[l], ((0, 0), (0, D_IN_PAD - w_in.shape[2]))).astype(BF16)
        cw = jnp.broadcast_to(conv_w[l][:, None, :], (CONV_WIDTH, CONV_ROWS, D_CONV))
        gkw = jnp.pad(gk_w[l], ((0, LANES - GATE_RANK), (0, 0))).astype(BF16)
        pad_g = EXPERT_ROW0 - N_GROUPS
        pad_e = LANES - EXPERT_ROW0 - N_EXPERTS
        rwt = jnp.concatenate([router_group_w[l].T, jnp.zeros((pad_g, D), F32), router_expert_w[l].T,
                               jnp.zeros((pad_e, D), F32)], axis=0).astype(BF16)
        rbt = jnp.concatenate([router_group_b[l], jnp.zeros((pad_g,), F32), router_expert_b[l].reshape(-1),
                               jnp.zeros((pad_e,), F32)])
        rbt = jnp.broadcast_to(rbt[:, None], (LANES, ROUTE_ROWS))
        xn, hf, meta, cnt = _mixer(
            x, S, attn_norm_g[l].reshape(1, D), win, cw, conv_b[l].reshape(1, -1), conv_ln_g[l].reshape(1, -1),
            conv_ln_b[l].reshape(1, -1), gkw, gk_b[l].reshape(1, -1), gla_norm_g[l].reshape(1, -1),
            w_out[l].astype(BF16), ffn_norm_g[l].reshape(1, D), rwt, rbt)
        tables, pads, block_expert, cap = _routing_tables(cnt)
        xs = _dispatch(tables, pads, hf, meta, cap)
        yb = _experts(block_expert, pads, xs, expert_w_gate, expert_w_up, expert_w_down, l)
        x = _combine(tables, xn, meta, final_norm_g.reshape(1, D), yb, final_norm=(l == depth - 1))
    return x.reshape(B, S, D)
```

```python
import functools

import jax
import jax.numpy as jnp
from jax import lax
from jax.experimental import pallas as pl
from jax.experimental.pallas import tpu as pltpu

D_MODEL = 1024
D_CONV = 512
D_GLA_V = 512
GLA_HEADS = 4
D_GLA_K = 256
HEAD_K = 64
HEAD_V = 128
GATE_RANK = 16
GATE_NORMALIZER = 16.0
CHUNK = 64
CONV_WIDTH = 31
N_GROUPS = 4
EXPERTS_PER_GROUP = 8
N_EXPERTS = 32
TOP_K = 2
D_EXPERT = 512
MOE_BLOCK = 512
EPS = 1e-6

LANES = 128
SUBLANES = 8
D_IN_MAIN = 2 * D_CONV + 2 * D_GLA_K + 2 * D_GLA_V
D_IN_PAD = D_IN_MAIN + LANES
TM = 512
GLA_ROWS = 256
HALO = 32
CONV_ROWS = 32
SHIFT_ROWS = TM + HALO - SUBLANES
TD = 512
EXPERT_ROW0 = 8
R_ROWS = 48
ROUTE_ROWS = 2048
ROUTE_TILES = ROUTE_ROWS // TM
VMEM_LIMIT = 48 * 1024 * 1024

F32 = jnp.float32
BF16 = jnp.bfloat16


def _dot(a, b):
    return jnp.dot(a, b, preferred_element_type=F32)


def _dot_nt(a, b):
    return lax.dot_general(a, b, (((1,), (1,)), ((), ())), preferred_element_type=F32)


def _dot_tn(a, b):
    return lax.dot_general(a, b, (((0,), (0,)), ((), ())), preferred_element_type=F32)


def _split_bf16(x):
    hi = x.astype(BF16)
    lo = (x - hi.astype(F32)).astype(BF16)
    return hi, lo


def _logistic(x):
    return 0.5 * jnp.tanh(0.5 * x) + 0.5


def _rms(x, g):
    return x * lax.rsqrt(jnp.mean(x * x, axis=-1, keepdims=True) + EPS) * g


def _mixer_kernel(x_ref, ang_ref, win_ref, cw_ref, cb_ref, lng_ref, lnb_ref, gkw_ref, gkb_ref, gng_ref,
                  wout_ref, fng_ref, rwt_ref, rbt_ref,
                  xo_ref, hf_ref, meta_ref, cnt_ref,
                  ubuf, sbuf, ybuf, st_ref, hf_scr, *, tiles_per_seq):
    @pl.when(pl.program_id(0) % tiles_per_seq == 0)
    def _():
        ubuf[0:HALO, :] = jnp.zeros((HALO, D_CONV), F32)
        st_ref[...] = jnp.zeros_like(st_ref)

    x = x_ref[...]
    proj = _dot(_rms(x, ang_ref[...]).astype(BF16), win_ref[...])

    def piece(o0, width, rows=slice(None)):
        return proj[rows, o0:o0 + width]
    o_q = 2 * D_CONV
    o_v = o_q + 2 * D_GLA_K
    conv_out = _conv_branch(piece(0, D_CONV), piece(D_CONV, D_CONV), cw_ref, cb_ref, lng_ref, lnb_ref,
                            ubuf, sbuf, ybuf)
    gla_out = jnp.concatenate([
        _gla_branch(piece(o_q, D_GLA_K, rows), piece(o_q + D_GLA_K, D_GLA_K, rows), piece(o_v, D_GLA_V, rows),
                    piece(o_v + D_GLA_V, D_GLA_V, rows), piece(D_IN_MAIN, LANES, rows),
                    gkw_ref, gkb_ref, gng_ref, st_ref)
        for rows in (slice(r, r + GLA_ROWS) for r in range(0, TM, GLA_ROWS))], axis=0)
    xn = (x + _dot(conv_out.astype(BF16), wout_ref[0:D_CONV, :])
          + _dot(gla_out.astype(BF16), wout_ref[D_CONV:, :]))
    xo_ref[...] = xn
    hf = _rms(xn, fng_ref[...]).astype(BF16)
    hf_ref[...] = hf
    sub = pl.program_id(0) % ROUTE_TILES
    hf_scr[pl.ds(pl.multiple_of(sub * TM, TM), TM), :] = hf

    @pl.when(sub == ROUTE_TILES - 1)
    def _():
        meta_ref[...], cnt_ref[...] = _router(hf_scr[...], rwt_ref, rbt_ref)


def _conv_branch(ua, ug, cw_ref, cb_ref, lng_ref, lnb_ref, ubuf, sbuf, ybuf):
    ubuf[HALO:HALO + TM, :] = ua * _logistic(ug)
    for r in range(1, SUBLANES):
        sbuf[r - 1] = ubuf[r:r + SHIFT_ROWS, :]
    for c in range(TM // CONV_ROWS):
        acc = jnp.broadcast_to(cb_ref[...], (CONV_ROWS, D_CONV))
        for j in range(CONV_WIDTH):
            off = HALO - (CONV_WIDTH - 1) + j + c * CONV_ROWS
            a0, r = off - off % SUBLANES, off % SUBLANES
            tap = ubuf[a0:a0 + CONV_ROWS, :] if r == 0 else sbuf[r - 1, a0:a0 + CONV_ROWS, :]
            acc = acc + cw_ref[j] * tap
        ybuf[c * CONV_ROWS:(c + 1) * CONV_ROWS, :] = acc
    ubuf[0:HALO, :] = ubuf[TM:TM + HALO, :]
    y = ybuf[...]
    mu = jnp.mean(y, axis=-1, keepdims=True)
    yc = y - mu
    var = jnp.mean(yc * yc, axis=-1, keepdims=True)
    yn = yc * lax.rsqrt(var + EPS) * lng_ref[...] + lnb_ref[...]
    return yn * _logistic(yn)


def _gla_branch(q, k, v, go, gkl, gkw_ref, gkb_ref, gng_ref, st_ref):
    gk = jax.nn.log_sigmoid(_dot(gkl.astype(BF16), gkw_ref[...]) + gkb_ref[...]) / GATE_NORMALIZER
    row = lax.broadcasted_iota(jnp.int32, (GLA_ROWS, GLA_ROWS), 0)
    col = lax.broadcasted_iota(jnp.int32, (GLA_ROWS, GLA_ROWS), 1)
    same_chunk = (row // CHUNK) == (col // CHUNK)
    causal = same_chunk & (col <= row)
    gk_hi, gk_lo = _split_bf16(gk)
    l_cum = causal.astype(BF16)
    b = _dot(l_cum, gk_hi) + _dot(l_cum, gk_lo)
    b_last = [b[c * CHUNK + CHUNK - 1:(c + 1) * CHUNK, :] for c in range(GLA_ROWS // CHUNK)]
    bl = jnp.concatenate([jnp.broadcast_to(t, (CHUNK, D_GLA_K)) for t in b_last], axis=0)
    qt = (q * (HEAD_K ** -0.5) * jnp.exp(b)).astype(BF16)
    kt = (k * jnp.exp(-b)).astype(BF16)
    ks = (k * jnp.exp(bl - b)).astype(BF16)
    vb = v.astype(BF16)

    klane = lax.broadcasted_iota(jnp.int32, (1, D_GLA_K), 1) // HEAD_K
    o_parts = []
    for h in range(GLA_HEADS):
        qh = jnp.where(klane == h, qt, jnp.zeros_like(qt))
        a = jnp.where(causal, _dot_nt(qh, kt), 0.0).astype(BF16)
        o_parts.append(_dot(a, vb[:, h * HEAD_V:(h + 1) * HEAD_V]))
    o_intra = jnp.concatenate(o_parts, axis=-1)

    srow = lax.broadcasted_iota(jnp.int32, (D_GLA_V, D_GLA_K), 0) // HEAD_V
    scol = lax.broadcasted_iota(jnp.int32, (D_GLA_V, D_GLA_K), 1) // HEAD_K
    head_diag = srow == scol
    o_inter = []
    for c in range(GLA_ROWS // CHUNK):
        r0 = c * CHUNK
        st = st_ref[...]
        o_inter.append(_dot_nt(qt[r0:r0 + CHUNK], st.astype(BF16)))
        ut = _dot_tn(vb[r0:r0 + CHUNK], ks[r0:r0 + CHUNK])
        st_ref[...] = st * jnp.exp(b_last[c]) + jnp.where(head_diag, ut, 0.0)
    o = o_intra + jnp.concatenate(o_inter, axis=0)
    o_n = []
    for h in range(GLA_HEADS):
        oh = o[:, h * HEAD_V:(h + 1) * HEAD_V]
        o_n.append(_rms(oh, gng_ref[...]))
    return jnp.concatenate(o_n, axis=-1) * (go * _logistic(go))


def _router(hf, rwt_ref, rbt_ref):
    lgt = _dot_nt(rwt_ref[...], hf)[0:R_ROWS] + rbt_ref[0:R_ROWS]
    rowf = lax.broadcasted_iota(jnp.int32, (R_ROWS, ROUTE_ROWS), 0).astype(F32)
    neg = jnp.float32(-1e30)
    big = jnp.float32(R_ROWS)
    is_g = rowf < N_GROUPS
    gl = jnp.where(is_g, lgt, neg)
    gm = jnp.max(gl, axis=0, keepdims=True)
    grp_p = 1.0 / jnp.sum(jnp.where(is_g, jnp.exp(gl - gm), 0.0), axis=0, keepdims=True)
    gidx = jnp.min(jnp.where(is_g & (gl == gm), rowf, big), axis=0, keepdims=True)
    lo = EXPERT_ROW0 + gidx * EXPERTS_PER_GROUP
    in_sel = (rowf >= lo) & (rowf < lo + EXPERTS_PER_GROUP)
    sl = jnp.where(in_sel, lgt, neg)
    sm = jnp.max(sl, axis=0, keepdims=True)
    sz = jnp.sum(jnp.where(in_sel, jnp.exp(sl - sm), 0.0), axis=0, keepdims=True)
    i1 = jnp.min(jnp.where(in_sel & (sl == sm), rowf, big), axis=0, keepdims=True)
    rest = in_sel & (rowf != i1)
    sl2 = jnp.where(rest, sl, neg)
    sm2 = jnp.max(sl2, axis=0, keepdims=True)
    i2 = jnp.min(jnp.where(rest & (sl2 == sm2), rowf, big), axis=0, keepdims=True)
    w1 = 1.0 / sz
    w2 = jnp.exp(sm2 - sm) / sz
    den = w1 + w2
    g1 = grp_p * (w1 / den)
    g2 = grp_p * (w2 / den)
    oh1 = rowf == i1
    oh2 = rowf == i2
    oh_any = jnp.where(oh1 | oh2, 1.0, 0.0).astype(BF16)
    trow = lax.broadcasted_iota(jnp.int32, (TD, TD), 0)
    tcol = lax.broadcasted_iota(jnp.int32, (TD, TD), 1)
    earlier = (trow < tcol).astype(BF16)
    ones_tt = jnp.ones((TD, TD), BF16)
    below = (jnp.where(i1 < rowf, 1.0, 0.0) + jnp.where(i2 < rowf, 1.0, 0.0)).astype(BF16)
    tiles = [slice(t * TD, (t + 1) * TD) for t in range(ROUTE_ROWS // TD)]
    base = jnp.concatenate([_dot(below[:, c], ones_tt) + _dot(oh_any[:, c], earlier) for c in tiles],
                           axis=1)
    p1 = jnp.sum(jnp.where(oh1, base, 0.0), axis=0, keepdims=True)
    p2 = jnp.sum(jnp.where(oh2, base, 0.0), axis=0, keepdims=True)
    counts = jnp.stack([_dot(oh_any[:, c], ones_tt)[:, 0:LANES] for c in tiles])
    mrow = lax.broadcasted_iota(jnp.int32, (SUBLANES, ROUTE_ROWS), 0)
    meta = jnp.where(mrow == 0, i1 - EXPERT_ROW0,
           jnp.where(mrow == 1, i2 - EXPERT_ROW0,
           jnp.where(mrow == 2, g1,
           jnp.where(mrow == 3, g2,
           jnp.where(mrow == 4, p1,
           jnp.where(mrow == 5, p2, 0.0))))))
    return meta, counts


def _mixer(x, seq_len, ang, win, cw, cb, lng, lnb, gkw, gkb, gng, wout, fng, rw, rb):
    T, D = x.shape
    n = T // TM
    const = lambda shape: pl.BlockSpec(shape, lambda g: (0,) * len(shape))
    tile = lambda w: pl.BlockSpec((TM, w), lambda g: (g, 0))
    return pl.pallas_call(
        functools.partial(_mixer_kernel, tiles_per_seq=seq_len // TM),
        grid=(n,),
        in_specs=[
            tile(D),
            const((1, D)), const((D, D_IN_PAD)), const((CONV_WIDTH, CONV_ROWS, D_CONV)), const((1, D_CONV)),
            const((1, D_CONV)), const((1, D_CONV)), const((LANES, D_GLA_K)), const((1, D_GLA_K)), const((1, HEAD_V)),
            const((D, D)), const((1, D)), const((LANES, D)), const((LANES, ROUTE_ROWS)),
        ],
        out_specs=[tile(D), tile(D),
                   pl.BlockSpec((SUBLANES, ROUTE_ROWS), lambda g: (0, g // ROUTE_TILES)),
                   pl.BlockSpec((ROUTE_ROWS // TD, R_ROWS, LANES), lambda g: (g // ROUTE_TILES, 0, 0))],
        out_shape=[
            jax.ShapeDtypeStruct((T, D), F32),
            jax.ShapeDtypeStruct((T, D), BF16),
            jax.ShapeDtypeStruct((SUBLANES, T), F32),
            jax.ShapeDtypeStruct((T // TD, R_ROWS, LANES), F32),
        ],
        scratch_shapes=[
            pltpu.VMEM((TM + HALO, D_CONV), F32),
            pltpu.VMEM((SUBLANES - 1, SHIFT_ROWS, D_CONV), F32),
            pltpu.VMEM((TM, D_CONV), F32),
            pltpu.VMEM((D_GLA_V, D_GLA_K), F32),
            pltpu.VMEM((ROUTE_ROWS, D), BF16),
        ],
        compiler_params=pltpu.CompilerParams(
            dimension_semantics=("arbitrary",), vmem_limit_bytes=VMEM_LIMIT),
        name="mixer",
    )(x, ang, win, cw, cb, lng, lnb, gkw, gkb, gng, wout, fng, rw, rb)


PIECES = D_MODEL // LANES
RING = 4
STEP_BLOCKS = 2
FFN_ROWS = 128
STEP_TILES = 2
DISPATCH_TILES = 4


def _to_row_tiles(ref, value):
    for c in range(PIECES):
        ref[pl.ds(c, value.shape[0], stride=PIECES), :] = value[:, c * LANES:(c + 1) * LANES]


def _from_row_tiles(ref, n_rows):
    return jnp.concatenate([ref[pl.ds(c, n_rows, stride=PIECES), :] for c in range(PIECES)], axis=1)


def _for_each_run(tile, enabled, n_ref, lo_ref, gs_ref, visit):
    for e in range(N_EXPERTS):
        lo = lo_ref[tile * N_EXPERTS + e]
        gs = gs_ref[tile * N_EXPERTS + e]
        n = n_ref[tile * N_EXPERTS + e]

        @pl.when((n > 0) & enabled)
        def _():
            visit(lo, gs, n)


def _slab(ref, row, rows):
    return ref.at[pl.ds(pl.multiple_of(row * PIECES, PIECES), rows * PIECES)]


def _dispatch_kernel(n_ref, lo_ref, gs_ref, pad0_ref, padn_ref, used_ref, hf_ref, meta_ref, xs_ref,
                     srt, zeros, sems, zsem):
    last = pl.num_programs(0) * DISPATCH_TILES - 1

    def wait_tile(s):
        pltpu.make_async_copy(srt.at[s], xs_ref.at[pl.ds(0, TOP_K * TD * PIECES)], sems.at[s]).wait()

    for k in range(DISPATCH_TILES):
        i = pl.program_id(0) * DISPATCH_TILES + k
        slot = i % RING
        pos = meta_ref[4:4 + TOP_K, k * TD:(k + 1) * TD]
        prow = lax.broadcasted_iota(jnp.int32, (TOP_K * TD, TD), 0).astype(F32)
        perm = jnp.where((prow == pos[0:1, :]) | (prow == pos[1:2, :]), 1.0, 0.0).astype(BF16)
        _to_row_tiles(srt.at[slot], _dot(perm, hf_ref[k * TD:(k + 1) * TD, :]))
        _for_each_run(
            i, True, n_ref, lo_ref, gs_ref,
            lambda lrow, grow, rows, slot=slot: pltpu.make_async_copy(
                _slab(srt.at[slot], lrow, rows), _slab(xs_ref, grow, rows), sems.at[slot]).start())

        @pl.when(i >= RING - 1)
        def _():
            wait_tile((i + 1) % RING)

    @pl.when(i == last)
    def _():
        for back in range(RING - 2, -1, -1):
            @pl.when(i >= back)
            def _():
                wait_tile((i - back) % RING)
        zeros[...] = jnp.zeros_like(zeros)
        n_blocks = xs_ref.shape[0] // (MOE_BLOCK * PIECES)

        def pad_copy(e):
            return pltpu.make_async_copy(_slab(zeros, 0, padn_ref[e]), _slab(xs_ref, pad0_ref[e], padn_ref[e]), zsem)

        def tail_copy(j):
            return pltpu.make_async_copy(zeros, _slab(xs_ref, j * MOE_BLOCK, MOE_BLOCK), zsem)

        def for_each_fill(act):
            for e in range(N_EXPERTS):
                @pl.when(padn_ref[e] > 0)
                def _():
                    act(pad_copy(e))

            def tail_block(j, carry):
                act(tail_copy(j))
                return carry
            lax.fori_loop(used_ref[0], n_blocks, tail_block, 0)

        for_each_fill(lambda cp: cp.start())
        for_each_fill(lambda cp: cp.wait())


def _dispatch(tables, pads, hf, meta, cap):
    T, D = hf.shape
    grid_spec = pltpu.PrefetchScalarGridSpec(
        num_scalar_prefetch=6,
        grid=(T // (DISPATCH_TILES * TD),),
        in_specs=[
            pl.BlockSpec((DISPATCH_TILES * TD, D), lambda i, *_: (i, 0)),
            pl.BlockSpec((SUBLANES, DISPATCH_TILES * TD), lambda i, *_: (0, i)),
        ],
        out_specs=pl.BlockSpec(memory_space=pl.ANY),
        scratch_shapes=[pltpu.VMEM((RING, TOP_K * TD * PIECES, LANES), F32),
                        pltpu.VMEM((MOE_BLOCK * PIECES, LANES), F32),
                        pltpu.SemaphoreType.DMA((RING,)), pltpu.SemaphoreType.DMA(())],
    )
    return pl.pallas_call(
        _dispatch_kernel,
        grid_spec=grid_spec,
        out_shape=jax.ShapeDtypeStruct((cap * PIECES, LANES), F32),
        compiler_params=pltpu.CompilerParams(dimension_semantics=("arbitrary",), vmem_limit_bytes=VMEM_LIMIT),
        name="dispatch",
    )(*tables, *pads, hf, meta)


def _expert_kernel(be_ref, used_ref, nxt_ref, seg_ref, valid_ref, xs_ref, wg_hbm, wu_hbm, wd_hbm, yb_ref,
                   wg_s, wu_s, wd_s, wg_b, wu_b, wd_b, sems, *, layer):
    def weight_copies(expert, s):
        return [pltpu.make_async_copy(src.at[layer, expert], dst.at[s], sems.at[s])
                for src, dst in ((wg_hbm, wg_s), (wu_hbm, wu_s), (wd_hbm, wd_s))]

    for k in range(STEP_BLOCKS):
        i = pl.program_id(0) * STEP_BLOCKS + k
        rows = pl.ds(k * MOE_BLOCK * PIECES, MOE_BLOCK * PIECES)
        e = be_ref[i]
        slot = seg_ref[i] % 2

        @pl.when(i == 0)
        def _():
            for cp in weight_copies(e, slot):
                cp.start()

        @pl.when((i == 0) | (e != be_ref[jnp.maximum(i - 1, 0)]))
        def _():
            for cp in weight_copies(e, slot):
                cp.wait()
            wg_b[...] = wg_s[slot].astype(BF16)
            wu_b[...] = wu_s[slot].astype(BF16)
            wd_b[...] = wd_s[slot].astype(BF16)

            @pl.when(nxt_ref[i] != e)
            def _():
                for cp in weight_copies(nxt_ref[i], 1 - slot):
                    cp.start()

        def ffn(n_rows):
            part = pl.ds(k * MOE_BLOCK * PIECES, n_rows * PIECES)
            xb = _from_row_tiles(xs_ref.at[part], n_rows).astype(BF16)
            g = _dot(xb, wg_b[...])
            u = _dot(xb, wu_b[...])
            h = (g * _logistic(g) * u).astype(BF16)
            _to_row_tiles(yb_ref.at[part], _dot(h, wd_b[...]))
            if n_rows < MOE_BLOCK:
                rest = pl.ds((k * MOE_BLOCK + n_rows) * PIECES, (MOE_BLOCK - n_rows) * PIECES)
                yb_ref[rest, :] = jnp.zeros(((MOE_BLOCK - n_rows) * PIECES, LANES), F32)

        valid = valid_ref[i]

        for n_rows in range(FFN_ROWS, MOE_BLOCK + 1, FFN_ROWS):
            @pl.when((valid > n_rows - FFN_ROWS) & (valid <= n_rows))
            def _():
                ffn(n_rows)

        @pl.when(valid == 0)
        def _():
            yb_ref[rows, :] = jnp.zeros((MOE_BLOCK * PIECES, LANES), F32)


def _experts(block_expert, pads, xs, wg, wu, wd, layer):
    D = D_MODEL
    n_blocks = xs.shape[0] // (MOE_BLOCK * PIECES)
    n_used = pads[2]
    block_row0 = jnp.arange(n_blocks, dtype=jnp.int32) * MOE_BLOCK
    is_expert = block_expert[:, None] == jnp.arange(N_EXPERTS, dtype=jnp.int32)[None, :]
    row_end = jnp.sum(jnp.where(is_expert, pads[0][None, :], 0), axis=1)
    valid = jnp.clip(row_end - block_row0, 0, MOE_BLOCK).astype(jnp.int32)
    changed = jnp.concatenate([jnp.zeros((1,), jnp.int32), (block_expert[1:] != block_expert[:-1]).astype(jnp.int32)])
    segment = jnp.cumsum(changed)
    after = jnp.sum((block_expert[None, :] <= block_expert[:, None]).astype(jnp.int32), axis=1)
    next_expert = jnp.where(after < n_blocks, block_expert[jnp.minimum(after, n_blocks - 1)], block_expert)
    grid_spec = pltpu.PrefetchScalarGridSpec(
        num_scalar_prefetch=5,
        grid=(n_blocks // STEP_BLOCKS,),
        in_specs=[
            pl.BlockSpec((STEP_BLOCKS * MOE_BLOCK * PIECES, LANES),
                         lambda i, be, used, *_: (jnp.minimum(i, (used[0] - 1) // STEP_BLOCKS), 0)),
            pl.BlockSpec(memory_space=pl.ANY), pl.BlockSpec(memory_space=pl.ANY), pl.BlockSpec(memory_space=pl.ANY),
        ],
        out_specs=pl.BlockSpec((STEP_BLOCKS * MOE_BLOCK * PIECES, LANES), lambda i, *_: (i, 0)),
        scratch_shapes=[pltpu.VMEM((2, D, D_EXPERT), F32), pltpu.VMEM((2, D, D_EXPERT), F32),
                        pltpu.VMEM((2, D_EXPERT, D), F32),
                        pltpu.VMEM((D, D_EXPERT), BF16), pltpu.VMEM((D, D_EXPERT), BF16),
                        pltpu.VMEM((D_EXPERT, D), BF16), pltpu.SemaphoreType.DMA((2,))],
    )
    return pl.pallas_call(
        functools.partial(_expert_kernel, layer=layer),
        grid_spec=grid_spec,
        out_shape=jax.ShapeDtypeStruct(xs.shape, F32),
        compiler_params=pltpu.CompilerParams(
            dimension_semantics=("arbitrary",), vmem_limit_bytes=VMEM_LIMIT),
        name="experts",
    )(block_expert, n_used, next_expert, segment, valid, xs, wg, wu, wd)


def _combine_kernel(n_ref, lo_ref, gs_ref, x_ref, meta_ref, fg_ref, yb_ref, o_ref, ys, sems, *, final_norm):
    n_tiles = pl.num_programs(0) * STEP_TILES

    def fetch(tile, enabled):
        s = tile % RING
        _for_each_run(
            jnp.minimum(tile, n_tiles - 1), enabled & (tile < n_tiles), n_ref, lo_ref, gs_ref,
            lambda lrow, grow, rows: pltpu.make_async_copy(
                _slab(yb_ref, grow, rows), _slab(ys.at[s], lrow, rows), sems.at[s]).start())

    for k in range(STEP_TILES):
        i = pl.program_id(0) * STEP_TILES + k
        slot = i % RING
        tokens = slice(k * TD, (k + 1) * TD)

        @pl.when(i == 0)
        def _():
            for ahead in range(RING - 1):
                fetch(i + ahead, True)

        fetch(i + RING - 1, True)
        meta = jnp.transpose(jnp.concatenate([meta_ref[:, tokens], jnp.zeros((LANES - SUBLANES, TD), F32)], axis=0))
        pcol = lax.broadcasted_iota(jnp.int32, (TD, TOP_K * TD), 1).astype(F32)
        gather = (jnp.where(pcol == meta[:, 4:5], meta[:, 2:3], 0.0)
                  + jnp.where(pcol == meta[:, 5:6], meta[:, 3:4], 0.0)).astype(BF16)
        pltpu.make_async_copy(yb_ref.at[pl.ds(0, TOP_K * TD * PIECES)], ys.at[slot], sems.at[slot]).wait()
        out = x_ref[tokens, :] + _dot(gather, _from_row_tiles(ys.at[slot], TOP_K * TD).astype(BF16))
        if final_norm:
            out = _rms(out, fg_ref[...])
        o_ref[tokens, :] = out


def _combine(tables, x, meta, fg, yb, final_norm):
    T, D = x.shape
    grid_spec = pltpu.PrefetchScalarGridSpec(
        num_scalar_prefetch=3,
        grid=(T // (STEP_TILES * TD),),
        in_specs=[
            pl.BlockSpec((STEP_TILES * TD, D), lambda i, *_: (i, 0)),
            pl.BlockSpec((SUBLANES, STEP_TILES * TD), lambda i, *_: (0, i)),
            pl.BlockSpec((1, D), lambda i, *_: (0, 0)),
            pl.BlockSpec(memory_space=pl.ANY),
        ],
        out_specs=pl.BlockSpec((STEP_TILES * TD, D), lambda i, *_: (i, 0)),
        scratch_shapes=[pltpu.VMEM((RING, TOP_K * TD * PIECES, LANES), F32), pltpu.SemaphoreType.DMA((RING,))],
    )
    return pl.pallas_call(
        functools.partial(_combine_kernel, final_norm=final_norm),
        grid_spec=grid_spec,
        out_shape=jax.ShapeDtypeStruct((T, D), F32),
        compiler_params=pltpu.CompilerParams(
            dimension_semantics=("arbitrary",), vmem_limit_bytes=VMEM_LIMIT),
        name="combine",
    )(*tables, x, meta, fg, yb)


def _routing_tables(cnt):
    n_tiles = cnt.shape[0]
    n = cnt[:, EXPERT_ROW0:EXPERT_ROW0 + N_EXPERTS, 0].astype(jnp.int32)
    counts = jnp.sum(n, axis=0)
    padded = (counts + MOE_BLOCK - 1) // MOE_BLOCK * MOE_BLOCK
    pend = jnp.cumsum(padded)
    pstart = pend - padded
    local = jnp.cumsum(n, axis=1) - n
    first = pstart[None, :] + jnp.cumsum(n, axis=0) - n
    n_blocks = (n_tiles * TD * TOP_K + MOE_BLOCK - 1) // MOE_BLOCK + N_EXPERTS
    block_row0 = jnp.arange(n_blocks, dtype=jnp.int32) * MOE_BLOCK
    block_expert = jnp.minimum(
        jnp.sum((pend[None, :] <= block_row0[:, None]).astype(jnp.int32), axis=1), N_EXPERTS - 1)
    pads = (pstart + counts, padded - counts, (pend[-1:] // MOE_BLOCK).astype(jnp.int32))
    return (n.reshape(-1), local.reshape(-1), first.reshape(-1)), pads, block_expert, n_blocks * MOE_BLOCK


def kernel(x, attn_norm_g, w_in, conv_w, conv_b, conv_ln_g, conv_ln_b, gk_w, gk_b, gla_norm_g, w_out, ffn_norm_g,
           router_group_w, router_group_b, router_expert_w, router_expert_b, expert_w_gate, expert_w_up,
           expert_w_down, final_norm_g):
    B, S, D = x.shape
    T = B * S
    depth = w_in.shape[0]
    x = x.reshape(T, D)
    for l in range(depth):
        win = jnp.pad(w_in[l], ((0, 0), (0, D_IN_PAD - w_in.shape[2]))).astype(BF16)
        cw = jnp.broadcast_to(conv_w[l][:, None, :], (CONV_WIDTH, CONV_ROWS, D_CONV))
        gkw = jnp.pad(gk_w[l], ((0, LANES - GATE_RANK), (0, 0))).astype(BF16)
        pad_g = EXPERT_ROW0 - N_GROUPS
        pad_e = LANES - EXPERT_ROW0 - N_EXPERTS
        rwt = jnp.concatenate([router_group_w[l].T, jnp.zeros((pad_g, D), F32), router_expert_w[l].T,
                               jnp.zeros((pad_e, D), F32)], axis=0).astype(BF16)
        rbt = jnp.concatenate([router_group_b[l], jnp.zeros((pad_g,), F32), router_expert_b[l].reshape(-1),
                               jnp.zeros((pad_e,), F32)])
        rbt = jnp.broadcast_to(rbt[:, None], (LANES, ROUTE_ROWS))
        xn, hf, meta, cnt = _mixer(
            x, S, attn_norm_g[l].reshape(1, D), win, cw, conv_b[l].reshape(1, -1), conv_ln_g[l].reshape(1, -1),
            conv_ln_b[l].reshape(1, -1), gkw, gk_b[l].reshape(1, -1), gla_norm_g[l].reshape(1, -1),
            w_out[l].astype(BF16), ffn_norm_g[l].reshape(1, D), rwt, rbt)
        tables, pads, block_expert, cap = _routing_tables(cnt)
        xs = _dispatch(tables, pads, hf, meta, cap)
        yb = _experts(block_expert, pads, xs, expert_w_gate, expert_w_up, expert_w_down, l)
        x = _combine(tables, xn, meta, final_norm_g.reshape(1, D), yb, final_norm=(l == depth - 1))
    return x.reshape(B, S, D)
```

```python
import functools

import jax
import jax.numpy as jnp
from jax import lax
from jax.experimental import pallas as pl
from jax.experimental.pallas import tpu as pltpu

D_MODEL = 1024
D_CONV = 512
D_GLA_V = 512
GLA_HEADS = 4
D_GLA_K = 256
HEAD_K = 64
HEAD_V = 128
GATE_RANK = 16
GATE_NORMALIZER = 16.0
CHUNK = 64
CONV_WIDTH = 31
N_GROUPS = 4
EXPERTS_PER_GROUP = 8
N_EXPERTS = 32
TOP_K = 2
D_EXPERT = 512
MOE_BLOCK = 512
EPS = 1e-6

LANES = 128
SUBLANES = 8
D_IN_MAIN = 2 * D_CONV + 2 * D_GLA_K + 2 * D_GLA_V
D_IN_PAD = D_IN_MAIN + LANES
TM = 512
GLA_ROWS = 256
HALO = 32
CONV_ROWS = 32
SHIFT_ROWS = TM + HALO - SUBLANES
TD = 512
EXPERT_ROW0 = 8
R_ROWS = 48
ROUTE_ROWS = 2048
ROUTE_TILES = ROUTE_ROWS // TM
VMEM_LIMIT = 48 * 1024 * 1024

F32 = jnp.float32
BF16 = jnp.bfloat16


def _dot(a, b):
    return jnp.dot(a, b, preferred_element_type=F32)


def _dot_nt(a, b):
    return lax.dot_general(a, b, (((1,), (1,)), ((), ())), preferred_element_type=F32)


def _dot_tn(a, b):
    return lax.dot_general(a, b, (((0,), (0,)), ((), ())), preferred_element_type=F32)


def _split_bf16(x):
    hi = x.astype(BF16)
    lo = (x - hi.astype(F32)).astype(BF16)
    return hi, lo


def _logistic(x):
    return 0.5 * jnp.tanh(0.5 * x) + 0.5


def _rms(x, g):
    return x * lax.rsqrt(jnp.mean(x * x, axis=-1, keepdims=True) + EPS) * g


def _mixer_kernel(x_ref, ang_ref, win_ref, cw_ref, cb_ref, lng_ref, lnb_ref, gkw_ref, gkb_ref, gng_ref,
                  wout_ref, fng_ref, rwt_ref, rbt_ref,
                  xo_ref, hf_ref, meta_ref, cnt_ref,
                  ubuf, sbuf, ybuf, st_ref, hf_scr, *, tiles_per_seq):
    @pl.when(pl.program_id(0) % tiles_per_seq == 0)
    def _():
        ubuf[0:HALO, :] = jnp.zeros((HALO, D_CONV), F32)
        st_ref[...] = jnp.zeros_like(st_ref)

    x = x_ref[...]
    proj = _dot(_rms(x, ang_ref[...]).astype(BF16), win_ref[...])

    def piece(o0, width, rows=slice(None)):
        return proj[rows, o0:o0 + width]
    o_q = 2 * D_CONV
    o_v = o_q + 2 * D_GLA_K
    conv_out = _conv_branch(piece(0, D_CONV), piece(D_CONV, D_CONV), cw_ref, cb_ref, lng_ref, lnb_ref,
                            ubuf, sbuf, ybuf)
    gla_out = jnp.concatenate([
        _gla_branch(piece(o_q, D_GLA_K, rows), piece(o_q + D_GLA_K, D_GLA_K, rows), piece(o_v, D_GLA_V, rows),
                    piece(o_v + D_GLA_V, D_GLA_V, rows), piece(D_IN_MAIN, LANES, rows),
                    gkw_ref, gkb_ref, gng_ref, st_ref)
        for rows in (slice(r, r + GLA_ROWS) for r in range(0, TM, GLA_ROWS))], axis=0)
    xn = (x + _dot(conv_out.astype(BF16), wout_ref[0:D_CONV, :])
          + _dot(gla_out.astype(BF16), wout_ref[D_CONV:, :]))
    xo_ref[...] = xn
    hf = _rms(xn, fng_ref[...]).astype(BF16)
    hf_ref[...] = hf
    sub = pl.program_id(0) % ROUTE_TILES
    hf_scr[pl.ds(pl.multiple_of(sub * TM, TM), TM), :] = hf

    @pl.when(sub == ROUTE_TILES - 1)
    def _():
        meta_ref[...], cnt_ref[...] = _router(hf_scr[...], rwt_ref, rbt_ref)


def _conv_branch(ua, ug, cw_ref, cb_ref, lng_ref, lnb_ref, ubuf, sbuf, ybuf):
    ubuf[HALO:HALO + TM, :] = ua * _logistic(ug)
    for r in range(1, SUBLANES):
        sbuf[r - 1] = ubuf[r:r + SHIFT_ROWS, :]
    for c in range(TM // CONV_ROWS):
        acc = jnp.broadcast_to(cb_ref[...], (CONV_ROWS, D_CONV))
        for j in range(CONV_WIDTH):
            off = HALO - (CONV_WIDTH - 1) + j + c * CONV_ROWS
            a0, r = off - off % SUBLANES, off % SUBLANES
            tap = ubuf[a0:a0 + CONV_ROWS, :] if r == 0 else sbuf[r - 1, a0:a0 + CONV_ROWS, :]
            acc = acc + cw_ref[j] * tap
        ybuf[c * CONV_ROWS:(c + 1) * CONV_ROWS, :] = acc
    ubuf[0:HALO, :] = ubuf[TM:TM + HALO, :]
    y = ybuf[...]
    mu = jnp.mean(y, axis=-1, keepdims=True)
    yc = y - mu
    var = jnp.mean(yc * yc, axis=-1, keepdims=True)
    yn = yc * lax.rsqrt(var + EPS) * lng_ref[...] + lnb_ref[...]
    return yn * _logistic(yn)


def _gla_branch(q, k, v, go, gkl, gkw_ref, gkb_ref, gng_ref, st_ref):
    gk = jax.nn.log_sigmoid(_dot(gkl.astype(BF16), gkw_ref[...]) + gkb_ref[...]) / GATE_NORMALIZER
    row = lax.broadcasted_iota(jnp.int32, (GLA_ROWS, GLA_ROWS), 0)
    col = lax.broadcasted_iota(jnp.int32, (GLA_ROWS, GLA_ROWS), 1)
    same_chunk = (row // CHUNK) == (col // CHUNK)
    causal = same_chunk & (col <= row)
    gk_hi, gk_lo = _split_bf16(gk)
    l_cum = causal.astype(BF16)
    b = _dot(l_cum, gk_hi) + _dot(l_cum, gk_lo)
    b_last = [b[c * CHUNK + CHUNK - 1:(c + 1) * CHUNK, :] for c in range(GLA_ROWS // CHUNK)]
    bl = jnp.concatenate([jnp.broadcast_to(t, (CHUNK, D_GLA_K)) for t in b_last], axis=0)
    qt = (q * (HEAD_K ** -0.5) * jnp.exp(b)).astype(BF16)
    kt = (k * jnp.exp(-b)).astype(BF16)
    ks = (k * jnp.exp(bl - b)).astype(BF16)
    vb = v.astype(BF16)

    klane = lax.broadcasted_iota(jnp.int32, (1, D_GLA_K), 1) // HEAD_K
    o_parts = []
    for h in range(GLA_HEADS):
        qh = jnp.where(klane == h, qt, jnp.zeros_like(qt))
        a = jnp.where(causal, _dot_nt(qh, kt), 0.0).astype(BF16)
        o_parts.append(_dot(a, vb[:, h * HEAD_V:(h + 1) * HEAD_V]))
    o_intra = jnp.concatenate(o_parts, axis=-1)

    srow = lax.broadcasted_iota(jnp.int32, (D_GLA_V, D_GLA_K), 0) // HEAD_V
    scol = lax.broadcasted_iota(jnp.int32, (D_GLA_V, D_GLA_K), 1) // HEAD_K
    head_diag = srow == scol
    o_inter = []
    for c in range(GLA_ROWS // CHUNK):
        r0 = c * CHUNK
        st = st_ref[...]
        o_inter.append(_dot_nt(qt[r0:r0 + CHUNK], st.astype(BF16)))
        ut = _dot_tn(vb[r0:r0 + CHUNK], ks[r0:r0 + CHUNK])
        st_ref[...] = st * jnp.exp(b_last[c]) + jnp.where(head_diag, ut, 0.0)
    o = o_intra + jnp.concatenate(o_inter, axis=0)
    o_n = []
    for h in range(GLA_HEADS):
        oh = o[:, h * HEAD_V:(h + 1) * HEAD_V]
        o_n.append(_rms(oh, gng_ref[...]))
    return jnp.concatenate(o_n, axis=-1) * (go * _logistic(go))


def _router(hf, rwt_ref, rbt_ref):
    lgt = _dot_nt(rwt_ref[...], hf)[0:R_ROWS] + rbt_ref[0:R_ROWS]
    rowf = lax.broadcasted_iota(jnp.int32, (R_ROWS, ROUTE_ROWS), 0).astype(F32)
    neg = jnp.float32(-1e30)
    big = jnp.float32(R_ROWS)
    is_g = rowf < N_GROUPS
    gl = jnp.where(is_g, lgt, neg)
    gm = jnp.max(gl, axis=0, keepdims=True)
    grp_p = 1.0 / jnp.sum(jnp.where(is_g, jnp.exp(gl - gm), 0.0), axis=0, keepdims=True)
    gidx = jnp.min(jnp.where(is_g & (gl == gm), rowf, big), axis=0, keepdims=True)
    lo = EXPERT_ROW0 + gidx * EXPERTS_PER_GROUP
    in_sel = (rowf >= lo) & (rowf < lo + EXPERTS_PER_GROUP)
    sl = jnp.where(in_sel, lgt, neg)
    sm = jnp.max(sl, axis=0, keepdims=True)
    sz = jnp.sum(jnp.where(in_sel, jnp.exp(sl - sm), 0.0), axis=0, keepdims=True)
    i1 = jnp.min(jnp.where(in_sel & (sl == sm), rowf, big), axis=0, keepdims=True)
    rest = in_sel & (rowf != i1)
    sl2 = jnp.where(rest, sl, neg)
    sm2 = jnp.max(sl2, axis=0, keepdims=True)
    i2 = jnp.min(jnp.where(rest & (sl2 == sm2), rowf, big), axis=0, keepdims=True)
    w1 = 1.0 / sz
    w2 = jnp.exp(sm2 - sm) / sz
    den = w1 + w2
    g1 = grp_p * (w1 / den)
    g2 = grp_p * (w2 / den)
    oh1 = rowf == i1
    oh2 = rowf == i2
    oh_any = jnp.where(oh1 | oh2, 1.0, 0.0).astype(BF16)
    trow = lax.broadcasted_iota(jnp.int32, (TD, TD), 0)
    tcol = lax.broadcasted_iota(jnp.int32, (TD, TD), 1)
    earlier = (trow < tcol).astype(BF16)
    ones_tt = jnp.ones((TD, TD), BF16)
    below = (jnp.where(i1 < rowf, 1.0, 0.0) + jnp.where(i2 < rowf, 1.0, 0.0)).astype(BF16)
    tiles = [slice(t * TD, (t + 1) * TD) for t in range(ROUTE_ROWS // TD)]
    base = jnp.concatenate([_dot(below[:, c], ones_tt) + _dot(oh_any[:, c], earlier) for c in tiles],
                           axis=1)
    p1 = jnp.sum(jnp.where(oh1, base, 0.0), axis=0, keepdims=True)
    p2 = jnp.sum(jnp.where(oh2, base, 0.0), axis=0, keepdims=True)
    counts = jnp.stack([_dot(oh_any[:, c], ones_tt)[:, 0:LANES] for c in tiles])
    mrow = lax.broadcasted_iota(jnp.int32, (SUBLANES, ROUTE_ROWS), 0)
    meta = jnp.where(mrow == 0, i1 - EXPERT_ROW0,
           jnp.where(mrow == 1, i2 - EXPERT_ROW0,
           jnp.where(mrow == 2, g1,
           jnp.where(mrow == 3, g2,
           jnp.where(mrow == 4, p1,
           jnp.where(mrow == 5, p2, 0.0))))))
    return meta, counts


def _mixer(x, seq_len, ang, win, cw, cb, lng, lnb, gkw, gkb, gng, wout, fng, rw, rb):
    T, D = x.shape
    n = T // TM
    const = lambda shape: pl.BlockSpec(shape, lambda g: (0,) * len(shape))
    tile = lambda w: pl.BlockSpec((TM, w), lambda g: (g, 0))
    return pl.pallas_call(
        functools.partial(_mixer_kernel, tiles_per_seq=seq_len // TM),
        grid=(n,),
        in_specs=[
            tile(D),
            const((1, D)), const((D, D_IN_PAD)), const((CONV_WIDTH, CONV_ROWS, D_CONV)), const((1, D_CONV)),
            const((1, D_CONV)), const((1, D_CONV)), const((LANES, D_GLA_K)), const((1, D_GLA_K)), const((1, HEAD_V)),
            const((D, D)), const((1, D)), const((LANES, D)), const((LANES, ROUTE_ROWS)),
        ],
        out_specs=[tile(D), tile(D),
                   pl.BlockSpec((SUBLANES, ROUTE_ROWS), lambda g: (0, g // ROUTE_TILES)),
                   pl.BlockSpec((ROUTE_ROWS // TD, R_ROWS, LANES), lambda g: (g // ROUTE_TILES, 0, 0))],
        out_shape=[
            jax.ShapeDtypeStruct((T, D), F32),
            jax.ShapeDtypeStruct((T, D), BF16),
            jax.ShapeDtypeStruct((SUBLANES, T), F32),
            jax.ShapeDtypeStruct((T // TD, R_ROWS, LANES), F32),
        ],
        scratch_shapes=[
            pltpu.VMEM((TM + HALO, D_CONV), F32),
            pltpu.VMEM((SUBLANES - 1, SHIFT_ROWS, D_CONV), F32),
            pltpu.VMEM((TM, D_CONV), F32),
            pltpu.VMEM((D_GLA_V, D_GLA_K), F32),
            pltpu.VMEM((ROUTE_ROWS, D), BF16),
        ],
        compiler_params=pltpu.CompilerParams(
            dimension_semantics=("arbitrary",), vmem_limit_bytes=VMEM_LIMIT),
        name="mixer",
    )(x, ang, win, cw, cb, lng, lnb, gkw, gkb, gng, wout, fng, rw, rb)


PIECES = D_MODEL // LANES
RING = 4
STEP_BLOCKS = 2
STEP_TILES = 2
DISPATCH_TILES = 4


def _to_row_tiles(ref, value):
    for c in range(PIECES):
        ref[pl.ds(c, value.shape[0], stride=PIECES), :] = value[:, c * LANES:(c + 1) * LANES]


def _from_row_tiles(ref, n_rows):
    return jnp.concatenate([ref[pl.ds(c, n_rows, stride=PIECES), :] for c in range(PIECES)], axis=1)


def _for_each_run(tile, enabled, n_ref, lo_ref, gs_ref, visit):
    for e in range(N_EXPERTS):
        lo = lo_ref[tile * N_EXPERTS + e]
        gs = gs_ref[tile * N_EXPERTS + e]
        n = n_ref[tile * N_EXPERTS + e]

        @pl.when((n > 0) & enabled)
        def _():
            visit(lo, gs, n)


def _slab(ref, row, rows):
    return ref.at[pl.ds(pl.multiple_of(row * PIECES, PIECES), rows * PIECES)]


def _dispatch_kernel(n_ref, lo_ref, gs_ref, pad0_ref, padn_ref, used_ref, hf_ref, meta_ref, xs_ref,
                     srt, zeros, sems, zsem):
    last = pl.num_programs(0) * DISPATCH_TILES - 1

    def wait_tile(s):
        pltpu.make_async_copy(srt.at[s], xs_ref.at[pl.ds(0, TOP_K * TD * PIECES)], sems.at[s]).wait()

    for k in range(DISPATCH_TILES):
        i = pl.program_id(0) * DISPATCH_TILES + k
        slot = i % RING
        pos = meta_ref[4:4 + TOP_K, k * TD:(k + 1) * TD]
        prow = lax.broadcasted_iota(jnp.int32, (TOP_K * TD, TD), 0).astype(F32)
        perm = jnp.where((prow == pos[0:1, :]) | (prow == pos[1:2, :]), 1.0, 0.0).astype(BF16)
        _to_row_tiles(srt.at[slot], _dot(perm, hf_ref[k * TD:(k + 1) * TD, :]))
        _for_each_run(
            i, True, n_ref, lo_ref, gs_ref,
            lambda lrow, grow, rows, slot=slot: pltpu.make_async_copy(
                _slab(srt.at[slot], lrow, rows), _slab(xs_ref, grow, rows), sems.at[slot]).start())

        @pl.when(i >= RING - 1)
        def _():
            wait_tile((i + 1) % RING)

    @pl.when(i == last)
    def _():
        for back in range(RING - 2, -1, -1):
            @pl.when(i >= back)
            def _():
                wait_tile((i - back) % RING)
        zeros[...] = jnp.zeros_like(zeros)
        n_blocks = xs_ref.shape[0] // (MOE_BLOCK * PIECES)

        def pad_copy(e):
            return pltpu.make_async_copy(_slab(zeros, 0, padn_ref[e]), _slab(xs_ref, pad0_ref[e], padn_ref[e]), zsem)

        def tail_copy(j):
            return pltpu.make_async_copy(zeros, _slab(xs_ref, j * MOE_BLOCK, MOE_BLOCK), zsem)

        def for_each_fill(act):
            for e in range(N_EXPERTS):
                @pl.when(padn_ref[e] > 0)
                def _():
                    act(pad_copy(e))

            def tail_block(j, carry):
                act(tail_copy(j))
                return carry
            lax.fori_loop(used_ref[0], n_blocks, tail_block, 0)

        for_each_fill(lambda cp: cp.start())
        for_each_fill(lambda cp: cp.wait())


def _dispatch(tables, pads, hf, meta, cap):
    T, D = hf.shape
    grid_spec = pltpu.PrefetchScalarGridSpec(
        num_scalar_prefetch=6,
        grid=(T // (DISPATCH_TILES * TD),),
        in_specs=[
            pl.BlockSpec((DISPATCH_TILES * TD, D), lambda i, *_: (i, 0)),
            pl.BlockSpec((SUBLANES, DISPATCH_TILES * TD), lambda i, *_: (0, i)),
        ],
        out_specs=pl.BlockSpec(memory_space=pl.ANY),
        scratch_shapes=[pltpu.VMEM((RING, TOP_K * TD * PIECES, LANES), F32),
                        pltpu.VMEM((MOE_BLOCK * PIECES, LANES), F32),
                        pltpu.SemaphoreType.DMA((RING,)), pltpu.SemaphoreType.DMA(())],
    )
    return pl.pallas_call(
        _dispatch_kernel,
        grid_spec=grid_spec,
        out_shape=jax.ShapeDtypeStruct((cap * PIECES, LANES), F32),
        compiler_params=pltpu.CompilerParams(dimension_semantics=("arbitrary",), vmem_limit_bytes=VMEM_LIMIT),
        name="dispatch",
    )(*tables, *pads, hf, meta)


def _expert_kernel(be_ref, used_ref, nxt_ref, seg_ref, valid_ref, xs_ref, wg_hbm, wu_hbm, wd_hbm, yb_ref,
                   wg_s, wu_s, wd_s, wg_b, wu_b, wd_b, sems, *, layer):
    def weight_copies(expert, s):
        return [pltpu.make_async_copy(src.at[layer, expert], dst.at[s], sems.at[s])
                for src, dst in ((wg_hbm, wg_s), (wu_hbm, wu_s), (wd_hbm, wd_s))]

    for k in range(STEP_BLOCKS):
        i = pl.program_id(0) * STEP_BLOCKS + k
        rows = pl.ds(k * MOE_BLOCK * PIECES, MOE_BLOCK * PIECES)
        e = be_ref[i]
        slot = seg_ref[i] % 2

        @pl.when(i == 0)
        def _():
            for cp in weight_copies(e, slot):
                cp.start()

        @pl.when((i == 0) | (e != be_ref[jnp.maximum(i - 1, 0)]))
        def _():
            for cp in weight_copies(e, slot):
                cp.wait()
            wg_b[...] = wg_s[slot].astype(BF16)
            wu_b[...] = wu_s[slot].astype(BF16)
            wd_b[...] = wd_s[slot].astype(BF16)

            @pl.when(nxt_ref[i] != e)
            def _():
                for cp in weight_copies(nxt_ref[i], 1 - slot):
                    cp.start()

        def ffn(n_rows):
            part = pl.ds(k * MOE_BLOCK * PIECES, n_rows * PIECES)
            xb = _from_row_tiles(xs_ref.at[part], n_rows).astype(BF16)
            g = _dot(xb, wg_b[...])
            u = _dot(xb, wu_b[...])
            h = (g * _logistic(g) * u).astype(BF16)
            _to_row_tiles(yb_ref.at[part], _dot(h, wd_b[...]))
            if n_rows < MOE_BLOCK:
                rest = pl.ds((k * MOE_BLOCK + n_rows) * PIECES, (MOE_BLOCK - n_rows) * PIECES)
                yb_ref[rest, :] = jnp.zeros(((MOE_BLOCK - n_rows) * PIECES, LANES), F32)

        valid = valid_ref[i]

        @pl.when(valid > MOE_BLOCK // 2)
        def _():
            ffn(MOE_BLOCK)

        @pl.when((valid > 0) & (valid <= MOE_BLOCK // 2))
        def _():
            ffn(MOE_BLOCK // 2)

        @pl.when(valid == 0)
        def _():
            yb_ref[rows, :] = jnp.zeros((MOE_BLOCK * PIECES, LANES), F32)


def _experts(block_expert, pads, xs, wg, wu, wd, layer):
    D = D_MODEL
    n_blocks = xs.shape[0] // (MOE_BLOCK * PIECES)
    n_used = pads[2]
    block_row0 = jnp.arange(n_blocks, dtype=jnp.int32) * MOE_BLOCK
    is_expert = block_expert[:, None] == jnp.arange(N_EXPERTS, dtype=jnp.int32)[None, :]
    row_end = jnp.sum(jnp.where(is_expert, pads[0][None, :], 0), axis=1)
    valid = jnp.clip(row_end - block_row0, 0, MOE_BLOCK).astype(jnp.int32)
    changed = jnp.concatenate([jnp.zeros((1,), jnp.int32), (block_expert[1:] != block_expert[:-1]).astype(jnp.int32)])
    segment = jnp.cumsum(changed)
    after = jnp.sum((block_expert[None, :] <= block_expert[:, None]).astype(jnp.int32), axis=1)
    next_expert = jnp.where(after < n_blocks, block_expert[jnp.minimum(after, n_blocks - 1)], block_expert)
    grid_spec = pltpu.PrefetchScalarGridSpec(
        num_scalar_prefetch=5,
        grid=(n_blocks // STEP_BLOCKS,),
        in_specs=[
            pl.BlockSpec((STEP_BLOCKS * MOE_BLOCK * PIECES, LANES),
                         lambda i, be, used, *_: (jnp.minimum(i, (used[0] - 1) // STEP_BLOCKS), 0)),
            pl.BlockSpec(memory_space=pl.ANY), pl.BlockSpec(memory_space=pl.ANY), pl.BlockSpec(memory_space=pl.ANY),
        ],
        out_specs=pl.BlockSpec((STEP_BLOCKS * MOE_BLOCK * PIECES, LANES), lambda i, *_: (i, 0)),
        scratch_shapes=[pltpu.VMEM((2, D, D_EXPERT), F32), pltpu.VMEM((2, D, D_EXPERT), F32),
                        pltpu.VMEM((2, D_EXPERT, D), F32),
                        pltpu.VMEM((D, D_EXPERT), BF16), pltpu.VMEM((D, D_EXPERT), BF16),
                        pltpu.VMEM((D_EXPERT, D), BF16), pltpu.SemaphoreType.DMA((2,))],
    )
    return pl.pallas_call(
        functools.partial(_expert_kernel, layer=layer),
        grid_spec=grid_spec,
        out_shape=jax.ShapeDtypeStruct(xs.shape, F32),
        compiler_params=pltpu.CompilerParams(
            dimension_semantics=("arbitrary",), vmem_limit_bytes=VMEM_LIMIT),
        name="experts",
    )(block_expert, n_used, next_expert, segment, valid, xs, wg, wu, wd)


def _combine_kernel(n_ref, lo_ref, gs_ref, x_ref, meta_ref, fg_ref, yb_ref, o_ref, ys, sems, *, final_norm):
    n_tiles = pl.num_programs(0) * STEP_TILES

    def fetch(tile, enabled):
        s = tile % RING
        _for_each_run(
            jnp.minimum(tile, n_tiles - 1), enabled & (tile < n_tiles), n_ref, lo_ref, gs_ref,
            lambda lrow, grow, rows: pltpu.make_async_copy(
                _slab(yb_ref, grow, rows), _slab(ys.at[s], lrow, rows), sems.at[s]).start())

    for k in range(STEP_TILES):
        i = pl.program_id(0) * STEP_TILES + k
        slot = i % RING
        tokens = slice(k * TD, (k + 1) * TD)

        @pl.when(i == 0)
        def _():
            for ahead in range(RING - 1):
                fetch(i + ahead, True)

        fetch(i + RING - 1, True)
        meta = jnp.transpose(jnp.concatenate([meta_ref[:, tokens], jnp.zeros((LANES - SUBLANES, TD), F32)], axis=0))
        pcol = lax.broadcasted_iota(jnp.int32, (TD, TOP_K * TD), 1).astype(F32)
        gather = (jnp.where(pcol == meta[:, 4:5], meta[:, 2:3], 0.0)
                  + jnp.where(pcol == meta[:, 5:6], meta[:, 3:4], 0.0)).astype(BF16)
        pltpu.make_async_copy(yb_ref.at[pl.ds(0, TOP_K * TD * PIECES)], ys.at[slot], sems.at[slot]).wait()
        out = x_ref[tokens, :] + _dot(gather, _from_row_tiles(ys.at[slot], TOP_K * TD).astype(BF16))
        if final_norm:
            out = _rms(out, fg_ref[...])
        o_ref[tokens, :] = out


def _combine(tables, x, meta, fg, yb, final_norm):
    T, D = x.shape
    grid_spec = pltpu.PrefetchScalarGridSpec(
        num_scalar_prefetch=3,
        grid=(T // (STEP_TILES * TD),),
        in_specs=[
            pl.BlockSpec((STEP_TILES * TD, D), lambda i, *_: (i, 0)),
            pl.BlockSpec((SUBLANES, STEP_TILES * TD), lambda i, *_: (0, i)),
            pl.BlockSpec((1, D), lambda i, *_: (0, 0)),
            pl.BlockSpec(memory_space=pl.ANY),
        ],
        out_specs=pl.BlockSpec((STEP_TILES * TD, D), lambda i, *_: (i, 0)),
        scratch_shapes=[pltpu.VMEM((RING, TOP_K * TD * PIECES, LANES), F32), pltpu.SemaphoreType.DMA((RING,))],
    )
    return pl.pallas_call(
        functools.partial(_combine_kernel, final_norm=final_norm),
        grid_spec=grid_spec,
        out_shape=jax.ShapeDtypeStruct((T, D), F32),
        compiler_params=pltpu.CompilerParams(
            dimension_semantics=("arbitrary",), vmem_limit_bytes=VMEM_LIMIT),
        name="combine",
    )(*tables, x, meta, fg, yb)


def _routing_tables(cnt):
    n_tiles = cnt.shape[0]
    n = cnt[:, EXPERT_ROW0:EXPERT_ROW0 + N_EXPERTS, 0].astype(jnp.int32)
    counts = jnp.sum(n, axis=0)
    padded = (counts + MOE_BLOCK - 1) // MOE_BLOCK * MOE_BLOCK
    pend = jnp.cumsum(padded)
    pstart = pend - padded
    local = jnp.cumsum(n, axis=1) - n
    first = pstart[None, :] + jnp.cumsum(n, axis=0) - n
    n_blocks = (n_tiles * TD * TOP_K + MOE_BLOCK - 1) // MOE_BLOCK + N_EXPERTS
    block_row0 = jnp.arange(n_blocks, dtype=jnp.int32) * MOE_BLOCK
    block_expert = jnp.minimum(
        jnp.sum((pend[None, :] <= block_row0[:, None]).astype(jnp.int32), axis=1), N_EXPERTS - 1)
    pads = (pstart + counts, padded - counts, (pend[-1:] // MOE_BLOCK).astype(jnp.int32))
    return (n.reshape(-1), local.reshape(-1), first.reshape(-1)), pads, block_expert, n_blocks * MOE_BLOCK


def kernel(x, attn_norm_g, w_in, conv_w, conv_b, conv_ln_g, conv_ln_b, gk_w, gk_b, gla_norm_g, w_out, ffn_norm_g,
           router_group_w, router_group_b, router_expert_w, router_expert_b, expert_w_gate, expert_w_up,
           expert_w_down, final_norm_g):
    B, S, D = x.shape
    T = B * S
    depth = w_in.shape[0]
    x = x.reshape(T, D)
    w_in_padded = jnp.pad(w_in, ((0, 0), (0, 0), (0, D_IN_PAD - w_in.shape[2]))).astype(BF16)
    for l in range(depth):
        win = w_in_padded[l]
        cw = jnp.broadcast_to(conv_w[l][:, None, :], (CONV_WIDTH, CONV_ROWS, D_CONV))
        gkw = jnp.pad(gk_w[l], ((0, LANES - GATE_RANK), (0, 0))).astype(BF16)
        pad_g = EXPERT_ROW0 - N_GROUPS
        pad_e = LANES - EXPERT_ROW0 - N_EXPERTS
        rwt = jnp.concatenate([router_group_w[l].T, jnp.zeros((pad_g, D), F32), router_expert_w[l].T,
                               jnp.zeros((pad_e, D), F32)], axis=0).astype(BF16)
        rbt = jnp.concatenate([router_group_b[l], jnp.zeros((pad_g,), F32), router_expert_b[l].reshape(-1),
                               jnp.zeros((pad_e,), F32)])
        rbt = jnp.broadcast_to(rbt[:, None], (LANES, ROUTE_ROWS))
        xn, hf, meta, cnt = _mixer(
            x, S, attn_norm_g[l].reshape(1, D), win, cw, conv_b[l].reshape(1, -1), conv_ln_g[l].reshape(1, -1),
            conv_ln_b[l].reshape(1, -1), gkw, gk_b[l].reshape(1, -1), gla_norm_g[l].reshape(1, -1),
            w_out[l].astype(BF16), ffn_norm_g[l].reshape(1, D), rwt, rbt)
        tables, pads, block_expert, cap = _routing_tables(cnt)
        xs = _dispatch(tables, pads, hf, meta, cap)
        yb = _experts(block_expert, pads, xs, expert_w_gate, expert_w_up, expert_w_down, l)
        x = _combine(tables, xn, meta, final_norm_g.reshape(1, D), yb, final_norm=(l == depth - 1))
    return x.reshape(B, S, D)
```

```python
import functools

import jax
import jax.numpy as jnp
from jax import lax
from jax.experimental import pallas as pl
from jax.experimental.pallas import tpu as pltpu

D_MODEL = 1024
D_CONV = 512
D_GLA_V = 512
GLA_HEADS = 4
D_GLA_K = 256
HEAD_K = 64
HEAD_V = 128
GATE_RANK = 16
GATE_NORMALIZER = 16.0
CHUNK = 64
CONV_WIDTH = 31
N_GROUPS = 4
EXPERTS_PER_GROUP = 8
N_EXPERTS = 32
TOP_K = 2
D_EXPERT = 512
MOE_BLOCK = 512
EPS = 1e-6

LANES = 128
SUBLANES = 8
D_IN_MAIN = 2 * D_CONV + 2 * D_GLA_K + 2 * D_GLA_V
D_IN_PAD = D_IN_MAIN + LANES
TM = 512
GLA_ROWS = 256
HALO = 32
CONV_ROWS = 32
SHIFT_ROWS = TM + HALO - SUBLANES
TD = 512
EXPERT_ROW0 = 8
R_ROWS = 48
ROUTE_ROWS = 2048
ROUTE_TILES = ROUTE_ROWS // TM
VMEM_LIMIT = 48 * 1024 * 1024

F32 = jnp.float32
BF16 = jnp.bfloat16


def _dot(a, b):
    return jnp.dot(a, b, preferred_element_type=F32)


def _dot_nt(a, b):
    return lax.dot_general(a, b, (((1,), (1,)), ((), ())), preferred_element_type=F32)


def _dot_tn(a, b):
    return lax.dot_general(a, b, (((0,), (0,)), ((), ())), preferred_element_type=F32)


def _split_bf16(x):
    hi = x.astype(BF16)
    lo = (x - hi.astype(F32)).astype(BF16)
    return hi, lo


def _logistic(x):
    return 0.5 * jnp.tanh(0.5 * x) + 0.5


def _rms(x, g):
    return x * lax.rsqrt(jnp.mean(x * x, axis=-1, keepdims=True) + EPS) * g


def _mixer_kernel(x_ref, ang_ref, win_ref, cw_ref, cb_ref, lng_ref, lnb_ref, gkw_ref, gkb_ref, gng_ref,
                  wout_ref, fng_ref, rwt_ref, rbt_ref,
                  xo_ref, hf_ref, meta_ref, cnt_ref,
                  ubuf, sbuf, ybuf, st_ref, hf_scr, *, tiles_per_seq):
    @pl.when(pl.program_id(0) % tiles_per_seq == 0)
    def _():
        ubuf[0:HALO, :] = jnp.zeros((HALO, D_CONV), F32)
        st_ref[...] = jnp.zeros_like(st_ref)

    x = x_ref[...]
    proj = _dot(_rms(x, ang_ref[...]).astype(BF16), win_ref[...])

    def piece(o0, width, rows=slice(None)):
        return proj[rows, o0:o0 + width]
    o_q = 2 * D_CONV
    o_v = o_q + 2 * D_GLA_K
    conv_out = _conv_branch(piece(0, D_CONV), piece(D_CONV, D_CONV), cw_ref, cb_ref, lng_ref, lnb_ref,
                            ubuf, sbuf, ybuf)
    gla_out = jnp.concatenate([
        _gla_branch(piece(o_q, D_GLA_K, rows), piece(o_q + D_GLA_K, D_GLA_K, rows), piece(o_v, D_GLA_V, rows),
                    piece(o_v + D_GLA_V, D_GLA_V, rows), piece(D_IN_MAIN, LANES, rows),
                    gkw_ref, gkb_ref, gng_ref, st_ref)
        for rows in (slice(r, r + GLA_ROWS) for r in range(0, TM, GLA_ROWS))], axis=0)
    xn = (x + _dot(conv_out.astype(BF16), wout_ref[0:D_CONV, :])
          + _dot(gla_out.astype(BF16), wout_ref[D_CONV:, :]))
    xo_ref[...] = xn
    hf = _rms(xn, fng_ref[...]).astype(BF16)
    hf_ref[...] = hf
    sub = pl.program_id(0) % ROUTE_TILES
    hf_scr[pl.ds(pl.multiple_of(sub * TM, TM), TM), :] = hf

    @pl.when(sub == ROUTE_TILES - 1)
    def _():
        meta_ref[...], cnt_ref[...] = _router(hf_scr[...], rwt_ref, rbt_ref)


def _conv_branch(ua, ug, cw_ref, cb_ref, lng_ref, lnb_ref, ubuf, sbuf, ybuf):
    ubuf[HALO:HALO + TM, :] = ua * _logistic(ug)
    for r in range(1, SUBLANES):
        sbuf[r - 1] = ubuf[r:r + SHIFT_ROWS, :]
    for c in range(TM // CONV_ROWS):
        acc = jnp.broadcast_to(cb_ref[...], (CONV_ROWS, D_CONV))
        for j in range(CONV_WIDTH):
            off = HALO - (CONV_WIDTH - 1) + j + c * CONV_ROWS
            a0, r = off - off % SUBLANES, off % SUBLANES
            tap = ubuf[a0:a0 + CONV_ROWS, :] if r == 0 else sbuf[r - 1, a0:a0 + CONV_ROWS, :]
            acc = acc + cw_ref[j] * tap
        ybuf[c * CONV_ROWS:(c + 1) * CONV_ROWS, :] = acc
    ubuf[0:HALO, :] = ubuf[TM:TM + HALO, :]
    y = ybuf[...]
    mu = jnp.mean(y, axis=-1, keepdims=True)
    yc = y - mu
    var = jnp.mean(yc * yc, axis=-1, keepdims=True)
    yn = yc * lax.rsqrt(var + EPS) * lng_ref[...] + lnb_ref[...]
    return yn * _logistic(yn)


def _gla_branch(q, k, v, go, gkl, gkw_ref, gkb_ref, gng_ref, st_ref):
    gk = jax.nn.log_sigmoid(_dot(gkl.astype(BF16), gkw_ref[...]) + gkb_ref[...]) / GATE_NORMALIZER
    row = lax.broadcasted_iota(jnp.int32, (GLA_ROWS, GLA_ROWS), 0)
    col = lax.broadcasted_iota(jnp.int32, (GLA_ROWS, GLA_ROWS), 1)
    same_chunk = (row // CHUNK) == (col // CHUNK)
    causal = same_chunk & (col <= row)
    gk_hi, gk_lo = _split_bf16(gk)
    l_cum = causal.astype(BF16)
    b = _dot(l_cum, gk_hi) + _dot(l_cum, gk_lo)
    b_last = [b[c * CHUNK + CHUNK - 1:(c + 1) * CHUNK, :] for c in range(GLA_ROWS // CHUNK)]
    bl = jnp.concatenate([jnp.broadcast_to(t, (CHUNK, D_GLA_K)) for t in b_last], axis=0)
    qt = (q * (HEAD_K ** -0.5) * jnp.exp(b)).astype(BF16)
    kt = (k * jnp.exp(-b)).astype(BF16)
    ks = (k * jnp.exp(bl - b)).astype(BF16)
    vb = v.astype(BF16)

    klane = lax.broadcasted_iota(jnp.int32, (1, D_GLA_K), 1) // HEAD_K
    o_parts = []
    for h in range(GLA_HEADS):
        qh = jnp.where(klane == h, qt, jnp.zeros_like(qt))
        a = jnp.where(causal, _dot_nt(qh, kt), 0.0).astype(BF16)
        o_parts.append(_dot(a, vb[:, h * HEAD_V:(h + 1) * HEAD_V]))
    o_intra = jnp.concatenate(o_parts, axis=-1)

    srow = lax.broadcasted_iota(jnp.int32, (D_GLA_V, D_GLA_K), 0) // HEAD_V
    scol = lax.broadcasted_iota(jnp.int32, (D_GLA_V, D_GLA_K), 1) // HEAD_K
    head_diag = srow == scol
    o_inter = []
    for c in range(GLA_ROWS // CHUNK):
        r0 = c * CHUNK
        st = st_ref[...]
        o_inter.append(_dot_nt(qt[r0:r0 + CHUNK], st.astype(BF16)))
        ut = _dot_tn(vb[r0:r0 + CHUNK], ks[r0:r0 + CHUNK])
        st_ref[...] = st * jnp.exp(b_last[c]) + jnp.where(head_diag, ut, 0.0)
    o = o_intra + jnp.concatenate(o_inter, axis=0)
    o_n = []
    for h in range(GLA_HEADS):
        oh = o[:, h * HEAD_V:(h + 1) * HEAD_V]
        o_n.append(_rms(oh, gng_ref[...]))
    return jnp.concatenate(o_n, axis=-1) * (go * _logistic(go))


def _router(hf, rwt_ref, rbt_ref):
    lgt = _dot_nt(rwt_ref[...], hf)[0:R_ROWS] + rbt_ref[0:R_ROWS]
    rowf = lax.broadcasted_iota(jnp.int32, (R_ROWS, ROUTE_ROWS), 0).astype(F32)
    neg = jnp.float32(-1e30)
    big = jnp.float32(R_ROWS)
    is_g = rowf < N_GROUPS
    gl = jnp.where(is_g, lgt, neg)
    gm = jnp.max(gl, axis=0, keepdims=True)
    grp_p = 1.0 / jnp.sum(jnp.where(is_g, jnp.exp(gl - gm), 0.0), axis=0, keepdims=True)
    gidx = jnp.min(jnp.where(is_g & (gl == gm), rowf, big), axis=0, keepdims=True)
    lo = EXPERT_ROW0 + gidx * EXPERTS_PER_GROUP
    in_sel = (rowf >= lo) & (rowf < lo + EXPERTS_PER_GROUP)
    sl = jnp.where(in_sel, lgt, neg)
    sm = jnp.max(sl, axis=0, keepdims=True)
    sz = jnp.sum(jnp.where(in_sel, jnp.exp(sl - sm), 0.0), axis=0, keepdims=True)
    i1 = jnp.min(jnp.where(in_sel & (sl == sm), rowf, big), axis=0, keepdims=True)
    rest = in_sel & (rowf != i1)
    sl2 = jnp.where(rest, sl, neg)
    sm2 = jnp.max(sl2, axis=0, keepdims=True)
    i2 = jnp.min(jnp.where(rest & (sl2 == sm2), rowf, big), axis=0, keepdims=True)
    w1 = 1.0 / sz
    w2 = jnp.exp(sm2 - sm) / sz
    den = w1 + w2
    g1 = grp_p * (w1 / den)
    g2 = grp_p * (w2 / den)
    oh1 = rowf == i1
    oh2 = rowf == i2
    oh_any = jnp.where(oh1 | oh2, 1.0, 0.0).astype(BF16)
    trow = lax.broadcasted_iota(jnp.int32, (TD, TD), 0)
    tcol = lax.broadcasted_iota(jnp.int32, (TD, TD), 1)
    earlier = (trow < tcol).astype(BF16)
    ones_tt = jnp.ones((TD, TD), BF16)
    below = (jnp.where(i1 < rowf, 1.0, 0.0) + jnp.where(i2 < rowf, 1.0, 0.0)).astype(BF16)
    tiles = [slice(t * TD, (t + 1) * TD) for t in range(ROUTE_ROWS // TD)]
    base = jnp.concatenate([_dot(below[:, c], ones_tt) + _dot(oh_any[:, c], earlier) for c in tiles],
                           axis=1)
    p1 = jnp.sum(jnp.where(oh1, base, 0.0), axis=0, keepdims=True)
    p2 = jnp.sum(jnp.where(oh2, base, 0.0), axis=0, keepdims=True)
    counts = jnp.stack([_dot(oh_any[:, c], ones_tt)[:, 0:LANES] for c in tiles])
    mrow = lax.broadcasted_iota(jnp.int32, (SUBLANES, ROUTE_ROWS), 0)
    meta = jnp.where(mrow == 0, i1 - EXPERT_ROW0,
           jnp.where(mrow == 1, i2 - EXPERT_ROW0,
           jnp.where(mrow == 2, g1,
           jnp.where(mrow == 3, g2,
           jnp.where(mrow == 4, p1,
           jnp.where(mrow == 5, p2, 0.0))))))
    return meta, counts


def _mixer(x, seq_len, ang, win, cw, cb, lng, lnb, gkw, gkb, gng, wout, fng, rw, rb):
    T, D = x.shape
    n = T // TM
    const = lambda shape: pl.BlockSpec(shape, lambda g: (0,) * len(shape))
    tile = lambda w: pl.BlockSpec((TM, w), lambda g: (g, 0))
    return pl.pallas_call(
        functools.partial(_mixer_kernel, tiles_per_seq=seq_len // TM),
        grid=(n,),
        in_specs=[
            tile(D),
            const((1, D)), const((D, D_IN_PAD)), const((CONV_WIDTH, CONV_ROWS, D_CONV)), const((1, D_CONV)),
            const((1, D_CONV)), const((1, D_CONV)), const((LANES, D_GLA_K)), const((1, D_GLA_K)), const((1, HEAD_V)),
            const((D, D)), const((1, D)), const((LANES, D)), const((LANES, ROUTE_ROWS)),
        ],
        out_specs=[tile(D), tile(D),
                   pl.BlockSpec((SUBLANES, ROUTE_ROWS), lambda g: (0, g // ROUTE_TILES)),
                   pl.BlockSpec((ROUTE_ROWS // TD, R_ROWS, LANES), lambda g: (g // ROUTE_TILES, 0, 0))],
        out_shape=[
            jax.ShapeDtypeStruct((T, D), F32),
            jax.ShapeDtypeStruct((T, D), BF16),
            jax.ShapeDtypeStruct((SUBLANES, T), F32),
            jax.ShapeDtypeStruct((T // TD, R_ROWS, LANES), F32),
        ],
        scratch_shapes=[
            pltpu.VMEM((TM + HALO, D_CONV), F32),
            pltpu.VMEM((SUBLANES - 1, SHIFT_ROWS, D_CONV), F32),
            pltpu.VMEM((TM, D_CONV), F32),
            pltpu.VMEM((D_GLA_V, D_GLA_K), F32),
            pltpu.VMEM((ROUTE_ROWS, D), BF16),
        ],
        compiler_params=pltpu.CompilerParams(
            dimension_semantics=("arbitrary",), vmem_limit_bytes=VMEM_LIMIT),
        name="mixer",
    )(x, ang, win, cw, cb, lng, lnb, gkw, gkb, gng, wout, fng, rw, rb)


PIECES = D_MODEL // LANES
RING = 4
STEP_BLOCKS = 4
EXPERT_VMEM_LIMIT = 58 * 1024 * 1024
STEP_TILES = 2
DISPATCH_TILES = 4


def _to_row_tiles(ref, value):
    for c in range(PIECES):
        ref[pl.ds(c, value.shape[0], stride=PIECES), :] = value[:, c * LANES:(c + 1) * LANES]


def _from_row_tiles(ref, n_rows):
    return jnp.concatenate([ref[pl.ds(c, n_rows, stride=PIECES), :] for c in range(PIECES)], axis=1)


def _for_each_run(tile, enabled, n_ref, lo_ref, gs_ref, visit):
    for e in range(N_EXPERTS):
        lo = lo_ref[tile * N_EXPERTS + e]
        gs = gs_ref[tile * N_EXPERTS + e]
        n = n_ref[tile * N_EXPERTS + e]

        @pl.when((n > 0) & enabled)
        def _():
            visit(lo, gs, n)


def _slab(ref, row, rows):
    return ref.at[pl.ds(pl.multiple_of(row * PIECES, PIECES), rows * PIECES)]


def _dispatch_kernel(n_ref, lo_ref, gs_ref, pad0_ref, padn_ref, used_ref, hf_ref, meta_ref, xs_ref,
                     srt, zeros, sems, zsem):
    last = pl.num_programs(0) * DISPATCH_TILES - 1

    def wait_tile(s):
        pltpu.make_async_copy(srt.at[s], xs_ref.at[pl.ds(0, TOP_K * TD * PIECES)], sems.at[s]).wait()

    for k in range(DISPATCH_TILES):
        i = pl.program_id(0) * DISPATCH_TILES + k
        slot = i % RING
        pos = meta_ref[4:4 + TOP_K, k * TD:(k + 1) * TD]
        prow = lax.broadcasted_iota(jnp.int32, (TOP_K * TD, TD), 0).astype(F32)
        perm = jnp.where((prow == pos[0:1, :]) | (prow == pos[1:2, :]), 1.0, 0.0).astype(BF16)
        _to_row_tiles(srt.at[slot], _dot(perm, hf_ref[k * TD:(k + 1) * TD, :]))
        _for_each_run(
            i, True, n_ref, lo_ref, gs_ref,
            lambda lrow, grow, rows, slot=slot: pltpu.make_async_copy(
                _slab(srt.at[slot], lrow, rows), _slab(xs_ref, grow, rows), sems.at[slot]).start())

        @pl.when(i >= RING - 1)
        def _():
            wait_tile((i + 1) % RING)

    @pl.when(i == last)
    def _():
        for back in range(RING - 2, -1, -1):
            @pl.when(i >= back)
            def _():
                wait_tile((i - back) % RING)
        zeros[...] = jnp.zeros_like(zeros)
        n_blocks = xs_ref.shape[0] // (MOE_BLOCK * PIECES)

        def pad_copy(e):
            return pltpu.make_async_copy(_slab(zeros, 0, padn_ref[e]), _slab(xs_ref, pad0_ref[e], padn_ref[e]), zsem)

        def tail_copy(j):
            return pltpu.make_async_copy(zeros, _slab(xs_ref, j * MOE_BLOCK, MOE_BLOCK), zsem)

        def for_each_fill(act):
            for e in range(N_EXPERTS):
                @pl.when(padn_ref[e] > 0)
                def _():
                    act(pad_copy(e))

            def tail_block(j, carry):
                act(tail_copy(j))
                return carry
            lax.fori_loop(used_ref[0], n_blocks, tail_block, 0)

        for_each_fill(lambda cp: cp.start())
        for_each_fill(lambda cp: cp.wait())


def _dispatch(tables, pads, hf, meta, cap):
    T, D = hf.shape
    grid_spec = pltpu.PrefetchScalarGridSpec(
        num_scalar_prefetch=6,
        grid=(T // (DISPATCH_TILES * TD),),
        in_specs=[
            pl.BlockSpec((DISPATCH_TILES * TD, D), lambda i, *_: (i, 0)),
            pl.BlockSpec((SUBLANES, DISPATCH_TILES * TD), lambda i, *_: (0, i)),
        ],
        out_specs=pl.BlockSpec(memory_space=pl.ANY),
        scratch_shapes=[pltpu.VMEM((RING, TOP_K * TD * PIECES, LANES), F32),
                        pltpu.VMEM((MOE_BLOCK * PIECES, LANES), F32),
                        pltpu.SemaphoreType.DMA((RING,)), pltpu.SemaphoreType.DMA(())],
    )
    return pl.pallas_call(
        _dispatch_kernel,
        grid_spec=grid_spec,
        out_shape=jax.ShapeDtypeStruct((cap * PIECES, LANES), F32),
        compiler_params=pltpu.CompilerParams(dimension_semantics=("arbitrary",), vmem_limit_bytes=VMEM_LIMIT),
        name="dispatch",
    )(*tables, *pads, hf, meta)


def _expert_kernel(be_ref, used_ref, nxt_ref, seg_ref, valid_ref, xs_ref, wg_hbm, wu_hbm, wd_hbm, yb_ref,
                   wg_s, wu_s, wd_s, wg_b, wu_b, wd_b, sems, *, layer):
    def weight_copies(expert, s):
        return [pltpu.make_async_copy(src.at[layer, expert], dst.at[s], sems.at[s])
                for src, dst in ((wg_hbm, wg_s), (wu_hbm, wu_s), (wd_hbm, wd_s))]

    for k in range(STEP_BLOCKS):
        i = pl.program_id(0) * STEP_BLOCKS + k
        rows = pl.ds(k * MOE_BLOCK * PIECES, MOE_BLOCK * PIECES)
        e = be_ref[i]
        slot = seg_ref[i] % 2

        @pl.when(i == 0)
        def _():
            for cp in weight_copies(e, slot):
                cp.start()

        @pl.when((i == 0) | (e != be_ref[jnp.maximum(i - 1, 0)]))
        def _():
            for cp in weight_copies(e, slot):
                cp.wait()
            wg_b[...] = wg_s[slot].astype(BF16)
            wu_b[...] = wu_s[slot].astype(BF16)
            wd_b[...] = wd_s[slot].astype(BF16)

            @pl.when(nxt_ref[i] != e)
            def _():
                for cp in weight_copies(nxt_ref[i], 1 - slot):
                    cp.start()

        def ffn(n_rows):
            part = pl.ds(k * MOE_BLOCK * PIECES, n_rows * PIECES)
            xb = _from_row_tiles(xs_ref.at[part], n_rows).astype(BF16)
            g = _dot(xb, wg_b[...])
            u = _dot(xb, wu_b[...])
            h = (g * _logistic(g) * u).astype(BF16)
            _to_row_tiles(yb_ref.at[part], _dot(h, wd_b[...]))
            if n_rows < MOE_BLOCK:
                rest = pl.ds((k * MOE_BLOCK + n_rows) * PIECES, (MOE_BLOCK - n_rows) * PIECES)
                yb_ref[rest, :] = jnp.zeros(((MOE_BLOCK - n_rows) * PIECES, LANES), F32)

        valid = valid_ref[i]

        @pl.when(valid > MOE_BLOCK // 2)
        def _():
            ffn(MOE_BLOCK)

        @pl.when((valid > 0) & (valid <= MOE_BLOCK // 2))
        def _():
            ffn(MOE_BLOCK // 2)

        @pl.when(valid == 0)
        def _():
            yb_ref[rows, :] = jnp.zeros((MOE_BLOCK * PIECES, LANES), F32)


def _experts(block_expert, pads, xs, wg, wu, wd, layer):
    D = D_MODEL
    n_blocks = xs.shape[0] // (MOE_BLOCK * PIECES)
    n_used = pads[2]
    block_row0 = jnp.arange(n_blocks, dtype=jnp.int32) * MOE_BLOCK
    is_expert = block_expert[:, None] == jnp.arange(N_EXPERTS, dtype=jnp.int32)[None, :]
    row_end = jnp.sum(jnp.where(is_expert, pads[0][None, :], 0), axis=1)
    valid = jnp.clip(row_end - block_row0, 0, MOE_BLOCK).astype(jnp.int32)
    changed = jnp.concatenate([jnp.zeros((1,), jnp.int32), (block_expert[1:] != block_expert[:-1]).astype(jnp.int32)])
    segment = jnp.cumsum(changed)
    after = jnp.sum((block_expert[None, :] <= block_expert[:, None]).astype(jnp.int32), axis=1)
    next_expert = jnp.where(after < n_blocks, block_expert[jnp.minimum(after, n_blocks - 1)], block_expert)
    grid_spec = pltpu.PrefetchScalarGridSpec(
        num_scalar_prefetch=5,
        grid=(n_blocks // STEP_BLOCKS,),
        in_specs=[
            pl.BlockSpec((STEP_BLOCKS * MOE_BLOCK * PIECES, LANES),
                         lambda i, be, used, *_: (jnp.minimum(i, (used[0] - 1) // STEP_BLOCKS), 0)),
            pl.BlockSpec(memory_space=pl.ANY), pl.BlockSpec(memory_space=pl.ANY), pl.BlockSpec(memory_space=pl.ANY),
        ],
        out_specs=pl.BlockSpec((STEP_BLOCKS * MOE_BLOCK * PIECES, LANES), lambda i, *_: (i, 0)),
        scratch_shapes=[pltpu.VMEM((2, D, D_EXPERT), F32), pltpu.VMEM((2, D, D_EXPERT), F32),
                        pltpu.VMEM((2, D_EXPERT, D), F32),
                        pltpu.VMEM((D, D_EXPERT), BF16), pltpu.VMEM((D, D_EXPERT), BF16),
                        pltpu.VMEM((D_EXPERT, D), BF16), pltpu.SemaphoreType.DMA((2,))],
    )
    return pl.pallas_call(
        functools.partial(_expert_kernel, layer=layer),
        grid_spec=grid_spec,
        out_shape=jax.ShapeDtypeStruct(xs.shape, F32),
        compiler_params=pltpu.CompilerParams(
            dimension_semantics=("arbitrary",), vmem_limit_bytes=EXPERT_VMEM_LIMIT),
        name="experts",
    )(block_expert, n_used, next_expert, segment, valid, xs, wg, wu, wd)


def _combine_kernel(n_ref, lo_ref, gs_ref, x_ref, meta_ref, fg_ref, yb_ref, o_ref, ys, sems, *, final_norm):
    n_tiles = pl.num_programs(0) * STEP_TILES

    def fetch(tile, enabled):
        s = tile % RING
        _for_each_run(
            jnp.minimum(tile, n_tiles - 1), enabled & (tile < n_tiles), n_ref, lo_ref, gs_ref,
            lambda lrow, grow, rows: pltpu.make_async_copy(
                _slab(yb_ref, grow, rows), _slab(ys.at[s], lrow, rows), sems.at[s]).start())

    for k in range(STEP_TILES):
        i = pl.program_id(0) * STEP_TILES + k
        slot = i % RING
        tokens = slice(k * TD, (k + 1) * TD)

        @pl.when(i == 0)
        def _():
            for ahead in range(RING - 1):
                fetch(i + ahead, True)

        fetch(i + RING - 1, True)
        meta = jnp.transpose(jnp.concatenate([meta_ref[:, tokens], jnp.zeros((LANES - SUBLANES, TD), F32)], axis=0))
        pcol = lax.broadcasted_iota(jnp.int32, (TD, TOP_K * TD), 1).astype(F32)
        gather = (jnp.where(pcol == meta[:, 4:5], meta[:, 2:3], 0.0)
                  + jnp.where(pcol == meta[:, 5:6], meta[:, 3:4], 0.0)).astype(BF16)
        pltpu.make_async_copy(yb_ref.at[pl.ds(0, TOP_K * TD * PIECES)], ys.at[slot], sems.at[slot]).wait()
        out = x_ref[tokens, :] + _dot(gather, _from_row_tiles(ys.at[slot], TOP_K * TD).astype(BF16))
        if final_norm:
            out = _rms(out, fg_ref[...])
        o_ref[tokens, :] = out


def _combine(tables, x, meta, fg, yb, final_norm):
    T, D = x.shape
    grid_spec = pltpu.PrefetchScalarGridSpec(
        num_scalar_prefetch=3,
        grid=(T // (STEP_TILES * TD),),
        in_specs=[
            pl.BlockSpec((STEP_TILES * TD, D), lambda i, *_: (i, 0)),
            pl.BlockSpec((SUBLANES, STEP_TILES * TD), lambda i, *_: (0, i)),
            pl.BlockSpec((1, D), lambda i, *_: (0, 0)),
            pl.BlockSpec(memory_space=pl.ANY),
        ],
        out_specs=pl.BlockSpec((STEP_TILES * TD, D), lambda i, *_: (i, 0)),
        scratch_shapes=[pltpu.VMEM((RING, TOP_K * TD * PIECES, LANES), F32), pltpu.SemaphoreType.DMA((RING,))],
    )
    return pl.pallas_call(
        functools.partial(_combine_kernel, final_norm=final_norm),
        grid_spec=grid_spec,
        out_shape=jax.ShapeDtypeStruct((T, D), F32),
        compiler_params=pltpu.CompilerParams(
            dimension_semantics=("arbitrary",), vmem_limit_bytes=VMEM_LIMIT),
        name="combine",
    )(*tables, x, meta, fg, yb)


def _routing_tables(cnt):
    n_tiles = cnt.shape[0]
    n = cnt[:, EXPERT_ROW0:EXPERT_ROW0 + N_EXPERTS, 0].astype(jnp.int32)
    counts = jnp.sum(n, axis=0)
    padded = (counts + MOE_BLOCK - 1) // MOE_BLOCK * MOE_BLOCK
    pend = jnp.cumsum(padded)
    pstart = pend - padded
    local = jnp.cumsum(n, axis=1) - n
    first = pstart[None, :] + jnp.cumsum(n, axis=0) - n
    n_blocks = (n_tiles * TD * TOP_K + MOE_BLOCK - 1) // MOE_BLOCK + N_EXPERTS
    block_row0 = jnp.arange(n_blocks, dtype=jnp.int32) * MOE_BLOCK
    block_expert = jnp.minimum(
        jnp.sum((pend[None, :] <= block_row0[:, None]).astype(jnp.int32), axis=1), N_EXPERTS - 1)
    pads = (pstart + counts, padded - counts, (pend[-1:] // MOE_BLOCK).astype(jnp.int32))
    return (n.reshape(-1), local.reshape(-1), first.reshape(-1)), pads, block_expert, n_blocks * MOE_BLOCK


def kernel(x, attn_norm_g, w_in, conv_w, conv_b, conv_ln_g, conv_ln_b, gk_w, gk_b, gla_norm_g, w_out, ffn_norm_g,
           router_group_w, router_group_b, router_expert_w, router_expert_b, expert_w_gate, expert_w_up,
           expert_w_down, final_norm_g):
    B, S, D = x.shape
    T = B * S
    depth = w_in.shape[0]
    x = x.reshape(T, D)
    w_in_padded = jnp.pad(w_in, ((0, 0), (0, 0), (0, D_IN_PAD - w_in.shape[2]))).astype(BF16)
    for l in range(depth):
        win = w_in_padded[l]
        cw = jnp.broadcast_to(conv_w[l][:, None, :], (CONV_WIDTH, CONV_ROWS, D_CONV))
        gkw = jnp.pad(gk_w[l], ((0, LANES - GATE_RANK), (0, 0))).astype(BF16)
        pad_g = EXPERT_ROW0 - N_GROUPS
        pad_e = LANES - EXPERT_ROW0 - N_EXPERTS
        rwt = jnp.concatenate([router_group_w[l].T, jnp.zeros((pad_g, D), F32), router_expert_w[l].T,
                               jnp.zeros((pad_e, D), F32)], axis=0).astype(BF16)
        rbt = jnp.concatenate([router_group_b[l], jnp.zeros((pad_g,), F32), router_expert_b[l].reshape(-1),
                               jnp.zeros((pad_e,), F32)])
        rbt = jnp.broadcast_to(rbt[:, None], (LANES, ROUTE_ROWS))
        xn, hf, meta, cnt = _mixer(
            x, S, attn_norm_g[l].reshape(1, D), win, cw, conv_b[l].reshape(1, -1), conv_ln_g[l].reshape(1, -1),
            conv_ln_b[l].reshape(1, -1), gkw, gk_b[l].reshape(1, -1), gla_norm_g[l].reshape(1, -1),
            w_out[l].astype(BF16), ffn_norm_g[l].reshape(1, D), rwt, rbt)
        tables, pads, block_expert, cap = _routing_tables(cnt)
        xs = _dispatch(tables, pads, hf, meta, cap)
        yb = _experts(block_expert, pads, xs, expert_w_gate, expert_w_up, expert_w_down, l)
        x = _combine(tables, xn, meta, final_norm_g.reshape(1, D), yb, final_norm=(l == depth - 1))
    return x.reshape(B, S, D)
```

```python
import functools

import jax
import jax.numpy as jnp
from jax import lax
from jax.experimental import pallas as pl
from jax.experimental.pallas import tpu as pltpu

D_MODEL = 1024
D_CONV = 512
D_GLA_V = 512
GLA_HEADS = 4
D_GLA_K = 256
HEAD_K = 64
HEAD_V = 128
GATE_RANK = 16
GATE_NORMALIZER = 16.0
CHUNK = 64
CONV_WIDTH = 31
N_GROUPS = 4
EXPERTS_PER_GROUP = 8
N_EXPERTS = 32
TOP_K = 2
D_EXPERT = 512
MOE_BLOCK = 512
EPS = 1e-6

LANES = 128
SUBLANES = 8
D_IN_MAIN = 2 * D_CONV + 2 * D_GLA_K + 2 * D_GLA_V
D_IN_PAD = D_IN_MAIN + LANES
TM = 512
GLA_ROWS = 256
HALO = 32
CONV_ROWS = 32
SHIFT_ROWS = TM + HALO - SUBLANES
TD = 512
EXPERT_ROW0 = 8
R_ROWS = 48
ROUTE_ROWS = 2048
ROUTE_TILES = ROUTE_ROWS // TM
VMEM_LIMIT = 48 * 1024 * 1024

F32 = jnp.float32
BF16 = jnp.bfloat16


def _dot(a, b):
    return jnp.dot(a, b, preferred_element_type=F32)


def _dot_nt(a, b):
    return lax.dot_general(a, b, (((1,), (1,)), ((), ())), preferred_element_type=F32)


def _dot_tn(a, b):
    return lax.dot_general(a, b, (((0,), (0,)), ((), ())), preferred_element_type=F32)


def _split_bf16(x):
    hi = x.astype(BF16)
    lo = (x - hi.astype(F32)).astype(BF16)
    return hi, lo


def _logistic(x):
    return 0.5 * jnp.tanh(0.5 * x) + 0.5


def _rms(x, g):
    return x * lax.rsqrt(jnp.mean(x * x, axis=-1, keepdims=True) + EPS) * g


def _mixer_kernel(x_ref, ang_ref, win_ref, cw_ref, cb_ref, lng_ref, lnb_ref, gkw_ref, gkb_ref, gng_ref,
                  wout_ref, fng_ref, rwt_ref, rbt_ref,
                  xo_ref, hf_ref, meta_ref, cnt_ref,
                  ubuf, sbuf, ybuf, st_ref, hf_scr, *, tiles_per_seq):
    @pl.when(pl.program_id(0) % tiles_per_seq == 0)
    def _():
        ubuf[0:HALO, :] = jnp.zeros((HALO, D_CONV), F32)
        st_ref[...] = jnp.zeros_like(st_ref)

    x = x_ref[...]
    proj = _dot(_rms(x, ang_ref[...]).astype(BF16), win_ref[...])

    def piece(o0, width, rows=slice(None)):
        return proj[rows, o0:o0 + width]
    o_q = 2 * D_CONV
    o_v = o_q + 2 * D_GLA_K
    conv_out = _conv_branch(piece(0, D_CONV), piece(D_CONV, D_CONV), cw_ref, cb_ref, lng_ref, lnb_ref,
                            ubuf, sbuf, ybuf)
    gla_out = jnp.concatenate([
        _gla_branch(piece(o_q, D_GLA_K, rows), piece(o_q + D_GLA_K, D_GLA_K, rows), piece(o_v, D_GLA_V, rows),
                    piece(o_v + D_GLA_V, D_GLA_V, rows), piece(D_IN_MAIN, LANES, rows),
                    gkw_ref, gkb_ref, gng_ref, st_ref)
        for rows in (slice(r, r + GLA_ROWS) for r in range(0, TM, GLA_ROWS))], axis=0)
    xn = (x + _dot(conv_out.astype(BF16), wout_ref[0:D_CONV, :])
          + _dot(gla_out.astype(BF16), wout_ref[D_CONV:, :]))
    xo_ref[...] = xn
    hf = _rms(xn, fng_ref[...]).astype(BF16)
    hf_ref[...] = hf
    sub = pl.program_id(0) % ROUTE_TILES
    hf_scr[pl.ds(pl.multiple_of(sub * TM, TM), TM), :] = hf

    @pl.when(sub == ROUTE_TILES - 1)
    def _():
        meta_ref[...], cnt_ref[...] = _router(hf_scr[...], rwt_ref, rbt_ref)


def _conv_branch(ua, ug, cw_ref, cb_ref, lng_ref, lnb_ref, ubuf, sbuf, ybuf):
    ubuf[HALO:HALO + TM, :] = ua * _logistic(ug)
    for r in range(1, SUBLANES):
        sbuf[r - 1] = ubuf[r:r + SHIFT_ROWS, :]
    for c in range(TM // CONV_ROWS):
        acc = jnp.broadcast_to(cb_ref[...], (CONV_ROWS, D_CONV))
        for j in range(CONV_WIDTH):
            off = HALO - (CONV_WIDTH - 1) + j + c * CONV_ROWS
            a0, r = off - off % SUBLANES, off % SUBLANES
            tap = ubuf[a0:a0 + CONV_ROWS, :] if r == 0 else sbuf[r - 1, a0:a0 + CONV_ROWS, :]
            acc = acc + cw_ref[j] * tap
        ybuf[c * CONV_ROWS:(c + 1) * CONV_ROWS, :] = acc
    ubuf[0:HALO, :] = ubuf[TM:TM + HALO, :]
    y = ybuf[...]
    mu = jnp.mean(y, axis=-1, keepdims=True)
    yc = y - mu
    var = jnp.mean(yc * yc, axis=-1, keepdims=True)
    yn = yc * lax.rsqrt(var + EPS) * lng_ref[...] + lnb_ref[...]
    return yn * _logistic(yn)


def _gla_branch(q, k, v, go, gkl, gkw_ref, gkb_ref, gng_ref, st_ref):
    gk = jax.nn.log_sigmoid(_dot(gkl.astype(BF16), gkw_ref[...]) + gkb_ref[...]) / GATE_NORMALIZER
    row = lax.broadcasted_iota(jnp.int32, (GLA_ROWS, GLA_ROWS), 0)
    col = lax.broadcasted_iota(jnp.int32, (GLA_ROWS, GLA_ROWS), 1)
    same_chunk = (row // CHUNK) == (col // CHUNK)
    causal = same_chunk & (col <= row)
    gk_hi, gk_lo = _split_bf16(gk)
    l_cum = causal.astype(BF16)
    b = _dot(l_cum, gk_hi) + _dot(l_cum, gk_lo)
    b_last = [b[c * CHUNK + CHUNK - 1:(c + 1) * CHUNK, :] for c in range(GLA_ROWS // CHUNK)]
    bl = jnp.concatenate([jnp.broadcast_to(t, (CHUNK, D_GLA_K)) for t in b_last], axis=0)
    qt = (q * (HEAD_K ** -0.5) * jnp.exp(b)).astype(BF16)
    kt = (k * jnp.exp(-b)).astype(BF16)
    ks = (k * jnp.exp(bl - b)).astype(BF16)
    vb = v.astype(BF16)

    klane = lax.broadcasted_iota(jnp.int32, (1, D_GLA_K), 1) // HEAD_K
    o_parts = []
    for h in range(GLA_HEADS):
        qh = jnp.where(klane == h, qt, jnp.zeros_like(qt))
        a = jnp.where(causal, _dot_nt(qh, kt), 0.0).astype(BF16)
        o_parts.append(_dot(a, vb[:, h * HEAD_V:(h + 1) * HEAD_V]))
    o_intra = jnp.concatenate(o_parts, axis=-1)

    srow = lax.broadcasted_iota(jnp.int32, (D_GLA_V, D_GLA_K), 0) // HEAD_V
    scol = lax.broadcasted_iota(jnp.int32, (D_GLA_V, D_GLA_K), 1) // HEAD_K
    head_diag = srow == scol
    o_inter = []
    for c in range(GLA_ROWS // CHUNK):
        r0 = c * CHUNK
        st = st_ref[...]
        o_inter.append(_dot_nt(qt[r0:r0 + CHUNK], st.astype(BF16)))
        ut = _dot_tn(vb[r0:r0 + CHUNK], ks[r0:r0 + CHUNK])
        st_ref[...] = st * jnp.exp(b_last[c]) + jnp.where(head_diag, ut, 0.0)
    o = o_intra + jnp.concatenate(o_inter, axis=0)
    o_n = []
    for h in range(GLA_HEADS):
        oh = o[:, h * HEAD_V:(h + 1) * HEAD_V]
        o_n.append(_rms(oh, gng_ref[...]))
    return jnp.concatenate(o_n, axis=-1) * (go * _logistic(go))


def _router(hf, rwt_ref, rbt_ref):
    lgt = _dot_nt(rwt_ref[...], hf)[0:R_ROWS] + rbt_ref[0:R_ROWS]
    rowf = lax.broadcasted_iota(jnp.int32, (R_ROWS, ROUTE_ROWS), 0).astype(F32)
    neg = jnp.float32(-1e30)
    big = jnp.float32(R_ROWS)
    is_g = rowf < N_GROUPS
    gl = jnp.where(is_g, lgt, neg)
    gm = jnp.max(gl, axis=0, keepdims=True)
    grp_p = 1.0 / jnp.sum(jnp.where(is_g, jnp.exp(gl - gm), 0.0), axis=0, keepdims=True)
    gidx = jnp.min(jnp.where(is_g & (gl == gm), rowf, big), axis=0, keepdims=True)
    lo = EXPERT_ROW0 + gidx * EXPERTS_PER_GROUP
    in_sel = (rowf >= lo) & (rowf < lo + EXPERTS_PER_GROUP)
    sl = jnp.where(in_sel, lgt, neg)
    sm = jnp.max(sl, axis=0, keepdims=True)
    sz = jnp.sum(jnp.where(in_sel, jnp.exp(sl - sm), 0.0), axis=0, keepdims=True)
    i1 = jnp.min(jnp.where(in_sel & (sl == sm), rowf, big), axis=0, keepdims=True)
    rest = in_sel & (rowf != i1)
    sl2 = jnp.where(rest, sl, neg)
    sm2 = jnp.max(sl2, axis=0, keepdims=True)
    i2 = jnp.min(jnp.where(rest & (sl2 == sm2), rowf, big), axis=0, keepdims=True)
    w1 = 1.0 / sz
    w2 = jnp.exp(sm2 - sm) / sz
    den = w1 + w2
    g1 = grp_p * (w1 / den)
    g2 = grp_p * (w2 / den)
    oh1 = rowf == i1
    oh2 = rowf == i2
    oh_any = jnp.where(oh1 | oh2, 1.0, 0.0).astype(BF16)
    trow = lax.broadcasted_iota(jnp.int32, (TD, TD), 0)
    tcol = lax.broadcasted_iota(jnp.int32, (TD, TD), 1)
    earlier = (trow < tcol).astype(BF16)
    ones_tt = jnp.ones((TD, TD), BF16)
    below = (jnp.where(i1 < rowf, 1.0, 0.0) + jnp.where(i2 < rowf, 1.0, 0.0)).astype(BF16)
    tiles = [slice(t * TD, (t + 1) * TD) for t in range(ROUTE_ROWS // TD)]
    base = jnp.concatenate([_dot(below[:, c], ones_tt) + _dot(oh_any[:, c], earlier) for c in tiles],
                           axis=1)
    p1 = jnp.sum(jnp.where(oh1, base, 0.0), axis=0, keepdims=True)
    p2 = jnp.sum(jnp.where(oh2, base, 0.0), axis=0, keepdims=True)
    counts = jnp.stack([_dot(oh_any[:, c], ones_tt)[:, 0:LANES] for c in tiles])
    mrow = lax.broadcasted_iota(jnp.int32, (SUBLANES, ROUTE_ROWS), 0)
    meta = jnp.where(mrow == 0, i1 - EXPERT_ROW0,
           jnp.where(mrow == 1, i2 - EXPERT_ROW0,
           jnp.where(mrow == 2, g1,
           jnp.where(mrow == 3, g2,
           jnp.where(mrow == 4, p1,
           jnp.where(mrow == 5, p2, 0.0))))))
    return meta, counts


def _mixer(x, seq_len, ang, win, cw, cb, lng, lnb, gkw, gkb, gng, wout, fng, rw, rb):
    T, D = x.shape
    n = T // TM
    const = lambda shape: pl.BlockSpec(shape, lambda g: (0,) * len(shape))
    tile = lambda w: pl.BlockSpec((TM, w), lambda g: (g, 0))
    return pl.pallas_call(
        functools.partial(_mixer_kernel, tiles_per_seq=seq_len // TM),
        grid=(n,),
        in_specs=[
            tile(D),
            const((1, D)), const((D, D_IN_PAD)), const((CONV_WIDTH, CONV_ROWS, D_CONV)), const((1, D_CONV)),
            const((1, D_CONV)), const((1, D_CONV)), const((LANES, D_GLA_K)), const((1, D_GLA_K)), const((1, HEAD_V)),
            const((D, D)), const((1, D)), const((LANES, D)), const((LANES, ROUTE_ROWS)),
        ],
        out_specs=[tile(D), tile(D),
                   pl.BlockSpec((SUBLANES, ROUTE_ROWS), lambda g: (0, g // ROUTE_TILES)),
                   pl.BlockSpec((ROUTE_ROWS // TD, R_ROWS, LANES), lambda g: (g // ROUTE_TILES, 0, 0))],
        out_shape=[
            jax.ShapeDtypeStruct((T, D), F32),
            jax.ShapeDtypeStruct((T, D), BF16),
            jax.ShapeDtypeStruct((SUBLANES, T), F32),
            jax.ShapeDtypeStruct((T // TD, R_ROWS, LANES), F32),
        ],
        scratch_shapes=[
            pltpu.VMEM((TM + HALO, D_CONV), F32),
            pltpu.VMEM((SUBLANES - 1, SHIFT_ROWS, D_CONV), F32),
            pltpu.VMEM((TM, D_CONV), F32),
            pltpu.VMEM((D_GLA_V, D_GLA_K), F32),
            pltpu.VMEM((ROUTE_ROWS, D), BF16),
        ],
        compiler_params=pltpu.CompilerParams(
            dimension_semantics=("arbitrary",), vmem_limit_bytes=VMEM_LIMIT),
        name="mixer",
    )(x, ang, win, cw, cb, lng, lnb, gkw, gkb, gng, wout, fng, rw, rb)


PIECES = D_MODEL // LANES
RING = 4
STEP_BLOCKS = 2
HIDDEN_TILE = 256
STEP_TILES = 2
DISPATCH_TILES = 4


def _to_row_tiles(ref, value):
    for c in range(PIECES):
        ref[pl.ds(c, value.shape[0], stride=PIECES), :] = value[:, c * LANES:(c + 1) * LANES]


def _from_row_tiles(ref, n_rows):
    return jnp.concatenate([ref[pl.ds(c, n_rows, stride=PIECES), :] for c in range(PIECES)], axis=1)


def _for_each_run(tile, enabled, n_ref, lo_ref, gs_ref, visit):
    for e in range(N_EXPERTS):
        lo = lo_ref[tile * N_EXPERTS + e]
        gs = gs_ref[tile * N_EXPERTS + e]
        n = n_ref[tile * N_EXPERTS + e]

        @pl.when((n > 0) & enabled)
        def _():
            visit(lo, gs, n)


def _slab(ref, row, rows):
    return ref.at[pl.ds(pl.multiple_of(row * PIECES, PIECES), rows * PIECES)]


def _dispatch_kernel(n_ref, lo_ref, gs_ref, pad0_ref, padn_ref, used_ref, hf_ref, meta_ref, xs_ref,
                     srt, zeros, sems, zsem):
    last = pl.num_programs(0) * DISPATCH_TILES - 1

    def wait_tile(s):
        pltpu.make_async_copy(srt.at[s], xs_ref.at[pl.ds(0, TOP_K * TD * PIECES)], sems.at[s]).wait()

    for k in range(DISPATCH_TILES):
        i = pl.program_id(0) * DISPATCH_TILES + k
        slot = i % RING
        pos = meta_ref[4:4 + TOP_K, k * TD:(k + 1) * TD]
        prow = lax.broadcasted_iota(jnp.int32, (TOP_K * TD, TD), 0).astype(F32)
        perm = jnp.where((prow == pos[0:1, :]) | (prow == pos[1:2, :]), 1.0, 0.0).astype(BF16)
        _to_row_tiles(srt.at[slot], _dot(perm, hf_ref[k * TD:(k + 1) * TD, :]))
        _for_each_run(
            i, True, n_ref, lo_ref, gs_ref,
            lambda lrow, grow, rows, slot=slot: pltpu.make_async_copy(
                _slab(srt.at[slot], lrow, rows), _slab(xs_ref, grow, rows), sems.at[slot]).start())

        @pl.when(i >= RING - 1)
        def _():
            wait_tile((i + 1) % RING)

    @pl.when(i == last)
    def _():
        for back in range(RING - 2, -1, -1):
            @pl.when(i >= back)
            def _():
                wait_tile((i - back) % RING)
        zeros[...] = jnp.zeros_like(zeros)
        n_blocks = xs_ref.shape[0] // (MOE_BLOCK * PIECES)

        def pad_copy(e):
            return pltpu.make_async_copy(_slab(zeros, 0, padn_ref[e]), _slab(xs_ref, pad0_ref[e], padn_ref[e]), zsem)

        def tail_copy(j):
            return pltpu.make_async_copy(zeros, _slab(xs_ref, j * MOE_BLOCK, MOE_BLOCK), zsem)

        def for_each_fill(act):
            for e in range(N_EXPERTS):
                @pl.when(padn_ref[e] > 0)
                def _():
                    act(pad_copy(e))

            def tail_block(j, carry):
                act(tail_copy(j))
                return carry
            lax.fori_loop(used_ref[0], n_blocks, tail_block, 0)

        for_each_fill(lambda cp: cp.start())
        for_each_fill(lambda cp: cp.wait())


def _dispatch(tables, pads, hf, meta, cap):
    T, D = hf.shape
    grid_spec = pltpu.PrefetchScalarGridSpec(
        num_scalar_prefetch=6,
        grid=(T // (DISPATCH_TILES * TD),),
        in_specs=[
            pl.BlockSpec((DISPATCH_TILES * TD, D), lambda i, *_: (i, 0)),
            pl.BlockSpec((SUBLANES, DISPATCH_TILES * TD), lambda i, *_: (0, i)),
        ],
        out_specs=pl.BlockSpec(memory_space=pl.ANY),
        scratch_shapes=[pltpu.VMEM((RING, TOP_K * TD * PIECES, LANES), F32),
                        pltpu.VMEM((MOE_BLOCK * PIECES, LANES), F32),
                        pltpu.SemaphoreType.DMA((RING,)), pltpu.SemaphoreType.DMA(())],
    )
    return pl.pallas_call(
        _dispatch_kernel,
        grid_spec=grid_spec,
        out_shape=jax.ShapeDtypeStruct((cap * PIECES, LANES), F32),
        compiler_params=pltpu.CompilerParams(dimension_semantics=("arbitrary",), vmem_limit_bytes=VMEM_LIMIT),
        name="dispatch",
    )(*tables, *pads, hf, meta)


def _expert_kernel(be_ref, used_ref, nxt_ref, seg_ref, valid_ref, xs_ref, wg_hbm, wu_hbm, wd_hbm, yb_ref,
                   wg_s, wu_s, wd_s, wg_b, wu_b, wd_b, sems, *, layer):
    def weight_copies(expert, s):
        return [pltpu.make_async_copy(src.at[layer, expert], dst.at[s], sems.at[s])
                for src, dst in ((wg_hbm, wg_s), (wu_hbm, wu_s), (wd_hbm, wd_s))]

    for k in range(STEP_BLOCKS):
        i = pl.program_id(0) * STEP_BLOCKS + k
        rows = pl.ds(k * MOE_BLOCK * PIECES, MOE_BLOCK * PIECES)
        e = be_ref[i]
        slot = seg_ref[i] % 2

        @pl.when(i == 0)
        def _():
            for cp in weight_copies(e, slot):
                cp.start()

        @pl.when((i == 0) | (e != be_ref[jnp.maximum(i - 1, 0)]))
        def _():
            for cp in weight_copies(e, slot):
                cp.wait()
            wg_b[...] = wg_s[slot].astype(BF16)
            wu_b[...] = wu_s[slot].astype(BF16)
            wd_b[...] = wd_s[slot].astype(BF16)

            @pl.when(nxt_ref[i] != e)
            def _():
                for cp in weight_copies(nxt_ref[i], 1 - slot):
                    cp.start()

        def ffn(n_rows):
            part = pl.ds(k * MOE_BLOCK * PIECES, n_rows * PIECES)
            xb = _from_row_tiles(xs_ref.at[part], n_rows).astype(BF16)
            y = None
            for c0 in range(0, D_EXPERT, HIDDEN_TILE):
                g = _dot(xb, wg_b[:, c0:c0 + HIDDEN_TILE])
                u = _dot(xb, wu_b[:, c0:c0 + HIDDEN_TILE])
                h = (g * _logistic(g) * u).astype(BF16)
                down = _dot(h, wd_b[c0:c0 + HIDDEN_TILE, :])
                y = down if y is None else y + down
            _to_row_tiles(yb_ref.at[part], y)
            if n_rows < MOE_BLOCK:
                rest = pl.ds((k * MOE_BLOCK + n_rows) * PIECES, (MOE_BLOCK - n_rows) * PIECES)
                yb_ref[rest, :] = jnp.zeros(((MOE_BLOCK - n_rows) * PIECES, LANES), F32)

        valid = valid_ref[i]

        @pl.when(valid > MOE_BLOCK // 2)
        def _():
            ffn(MOE_BLOCK)

        @pl.when((valid > 0) & (valid <= MOE_BLOCK // 2))
        def _():
            ffn(MOE_BLOCK // 2)

        @pl.when(valid == 0)
        def _():
            yb_ref[rows, :] = jnp.zeros((MOE_BLOCK * PIECES, LANES), F32)


def _experts(block_expert, pads, xs, wg, wu, wd, layer):
    D = D_MODEL
    n_blocks = xs.shape[0] // (MOE_BLOCK * PIECES)
    n_used = pads[2]
    block_row0 = jnp.arange(n_blocks, dtype=jnp.int32) * MOE_BLOCK
    is_expert = block_expert[:, None] == jnp.arange(N_EXPERTS, dtype=jnp.int32)[None, :]
    row_end = jnp.sum(jnp.where(is_expert, pads[0][None, :], 0), axis=1)
    valid = jnp.clip(row_end - block_row0, 0, MOE_BLOCK).astype(jnp.int32)
    changed = jnp.concatenate([jnp.zeros((1,), jnp.int32), (block_expert[1:] != block_expert[:-1]).astype(jnp.int32)])
    segment = jnp.cumsum(changed)
    after = jnp.sum((block_expert[None, :] <= block_expert[:, None]).astype(jnp.int32), axis=1)
    next_expert = jnp.where(after < n_blocks, block_expert[jnp.minimum(after, n_blocks - 1)], block_expert)
    grid_spec = pltpu.PrefetchScalarGridSpec(
        num_scalar_prefetch=5,
        grid=(n_blocks // STEP_BLOCKS,),
        in_specs=[
            pl.BlockSpec((STEP_BLOCKS * MOE_BLOCK * PIECES, LANES),
                         lambda i, be, used, *_: (jnp.minimum(i, (used[0] - 1) // STEP_BLOCKS), 0)),
            pl.BlockSpec(memory_space=pl.ANY), pl.BlockSpec(memory_space=pl.ANY), pl.BlockSpec(memory_space=pl.ANY),
        ],
        out_specs=pl.BlockSpec((STEP_BLOCKS * MOE_BLOCK * PIECES, LANES), lambda i, *_: (i, 0)),
        scratch_shapes=[pltpu.VMEM((2, D, D_EXPERT), F32), pltpu.VMEM((2, D, D_EXPERT), F32),
                        pltpu.VMEM((2, D_EXPERT, D), F32),
                        pltpu.VMEM((D, D_EXPERT), BF16), pltpu.VMEM((D, D_EXPERT), BF16),
                        pltpu.VMEM((D_EXPERT, D), BF16), pltpu.SemaphoreType.DMA((2,))],
    )
    return pl.pallas_call(
        functools.partial(_expert_kernel, layer=layer),
        grid_spec=grid_spec,
        out_shape=jax.ShapeDtypeStruct(xs.shape, F32),
        compiler_params=pltpu.CompilerParams(
            dimension_semantics=("arbitrary",), vmem_limit_bytes=VMEM_LIMIT),
        name="experts",
    )(block_expert, n_used, next_expert, segment, valid, xs, wg, wu, wd)


def _combine_kernel(n_ref, lo_ref, gs_ref, x_ref, meta_ref, fg_ref, yb_ref, o_ref, ys, sems, *, final_norm):
    n_tiles = pl.num_programs(0) * STEP_TILES

    def fetch(tile, enabled):
        s = tile % RING
        _for_each_run(
            jnp.minimum(tile, n_tiles - 1), enabled & (tile < n_tiles), n_ref, lo_ref, gs_ref,
            lambda lrow, grow, rows: pltpu.make_async_copy(
                _slab(yb_ref, grow, rows), _slab(ys.at[s], lrow, rows), sems.at[s]).start())

    for k in range(STEP_TILES):
        i = pl.program_id(0) * STEP_TILES + k
        slot = i % RING
        tokens = slice(k * TD, (k + 1) * TD)

        @pl.when(i == 0)
        def _():
            for ahead in range(RING - 1):
                fetch(i + ahead, True)

        fetch(i + RING - 1, True)
        meta = jnp.transpose(jnp.concatenate([meta_ref[:, tokens], jnp.zeros((LANES - SUBLANES, TD), F32)], axis=0))
        pcol = lax.broadcasted_iota(jnp.int32, (TD, TOP_K * TD), 1).astype(F32)
        gather = (jnp.where(pcol == meta[:, 4:5], meta[:, 2:3], 0.0)
                  + jnp.where(pcol == meta[:, 5:6], meta[:, 3:4], 0.0)).astype(BF16)
        pltpu.make_async_copy(yb_ref.at[pl.ds(0, TOP_K * TD * PIECES)], ys.at[slot], sems.at[slot]).wait()
        out = x_ref[tokens, :] + _dot(gather, _from_row_tiles(ys.at[slot], TOP_K * TD).astype(BF16))
        if final_norm:
            out = _rms(out, fg_ref[...])
        o_ref[tokens, :] = out


def _combine(tables, x, meta, fg, yb, final_norm):
    T, D = x.shape
    grid_spec = pltpu.PrefetchScalarGridSpec(
        num_scalar_prefetch=3,
        grid=(T // (STEP_TILES * TD),),
        in_specs=[
            pl.BlockSpec((STEP_TILES * TD, D), lambda i, *_: (i, 0)),
            pl.BlockSpec((SUBLANES, STEP_TILES * TD), lambda i, *_: (0, i)),
            pl.BlockSpec((1, D), lambda i, *_: (0, 0)),
            pl.BlockSpec(memory_space=pl.ANY),
        ],
        out_specs=pl.BlockSpec((STEP_TILES * TD, D), lambda i, *_: (i, 0)),
        scratch_shapes=[pltpu.VMEM((RING, TOP_K * TD * PIECES, LANES), F32), pltpu.SemaphoreType.DMA((RING,))],
    )
    return pl.pallas_call(
        functools.partial(_combine_kernel, final_norm=final_norm),
        grid_spec=grid_spec,
        out_shape=jax.ShapeDtypeStruct((T, D), F32),
        compiler_params=pltpu.CompilerParams(
            dimension_semantics=("arbitrary",), vmem_limit_bytes=VMEM_LIMIT),
        name="combine",
    )(*tables, x, meta, fg, yb)


def _routing_tables(cnt):
    n_tiles = cnt.shape[0]
    n = cnt[:, EXPERT_ROW0:EXPERT_ROW0 + N_EXPERTS, 0].astype(jnp.int32)
    counts = jnp.sum(n, axis=0)
    padded = (counts + MOE_BLOCK - 1) // MOE_BLOCK * MOE_BLOCK
    pend = jnp.cumsum(padded)
    pstart = pend - padded
    local = jnp.cumsum(n, axis=1) - n
    first = pstart[None, :] + jnp.cumsum(n, axis=0) - n
    n_blocks = (n_tiles * TD * TOP_K + MOE_BLOCK - 1) // MOE_BLOCK + N_EXPERTS
    block_row0 = jnp.arange(n_blocks, dtype=jnp.int32) * MOE_BLOCK
    block_expert = jnp.minimum(
        jnp.sum((pend[None, :] <= block_row0[:, None]).astype(jnp.int32), axis=1), N_EXPERTS - 1)
    pads = (pstart + counts, padded - counts, (pend[-1:] // MOE_BLOCK).astype(jnp.int32))
    return (n.reshape(-1), local.reshape(-1), first.reshape(-1)), pads, block_expert, n_blocks * MOE_BLOCK


def kernel(x, attn_norm_g, w_in, conv_w, conv_b, conv_ln_g, conv_ln_b, gk_w, gk_b, gla_norm_g, w_out, ffn_norm_g,
           router_group_w, router_group_b, router_expert_w, router_expert_b, expert_w_gate, expert_w_up,
           expert_w_down, final_norm_g):
    B, S, D = x.shape
    T = B * S
    depth = w_in.shape[0]
    x = x.reshape(T, D)
    w_in_padded = jnp.pad(w_in, ((0, 0), (0, 0), (0, D_IN_PAD - w_in.shape[2]))).astype(BF16)
    for l in range(depth):
        win = w_in_padded[l]
        cw = jnp.broadcast_to(conv_w[l][:, None, :], (CONV_WIDTH, CONV_ROWS, D_CONV))
        gkw = jnp.pad(gk_w[l], ((0, LANES - GATE_RANK), (0, 0))).astype(BF16)
        pad_g = EXPERT_ROW0 - N_GROUPS
        pad_e = LANES - EXPERT_ROW0 - N_EXPERTS
        rwt = jnp.concatenate([router_group_w[l].T, jnp.zeros((pad_g, D), F32), router_expert_w[l].T,
                               jnp.zeros((pad_e, D), F32)], axis=0).astype(BF16)
        rbt = jnp.concatenate([router_group_b[l], jnp.zeros((pad_g,), F32), router_expert_b[l].reshape(-1),
                               jnp.zeros((pad_e,), F32)])
        rbt = jnp.broadcast_to(rbt[:, None], (LANES, ROUTE_ROWS))
        xn, hf, meta, cnt = _mixer(
            x, S, attn_norm_g[l].reshape(1, D), win, cw, conv_b[l].reshape(1, -1), conv_ln_g[l].reshape(1, -1),
            conv_ln_b[l].reshape(1, -1), gkw, gk_b[l].reshape(1, -1), gla_norm_g[l].reshape(1, -1),
            w_out[l].astype(BF16), ffn_norm_g[l].reshape(1, D), rwt, rbt)
        tables, pads, block_expert, cap = _routing_tables(cnt)
        xs = _dispatch(tables, pads, hf, meta, cap)
        yb = _experts(block_expert, pads, xs, expert_w_gate, expert_w_up, expert_w_down, l)
        x = _combine(tables, xn, meta, final_norm_g.reshape(1, D), yb, final_norm=(l == depth - 1))
    return x.reshape(B, S, D)
```
